```python
import jax, jax.numpy as jnp
from jax import lax
import numpy as np

D_MODEL = 1024
BATCH = 16
SEQ = 2048
DEPTH = 1

GRID_W = 64
CTX_LEN = 256
D_MIX = D_MODEL
NDIR = 2
A_HEADS = 4
A_DK = 128
A_DV = 128
B_HEADS = 4
B_DK = 64
B_DV = 128
GLA_RANK = 16
GLA_NORMALIZER = 16.0
CONV_K = 3
DELTA_CHUNK = 64
GLA_CHUNK = 16
D_FF = -(-8 * D_MODEL // (3 * 256)) * 256
EPS = 1e-6
IN_SIZES = (A_HEADS * A_DK, A_HEADS * A_DK, A_HEADS * A_DV, A_HEADS * A_DV, NDIR * A_HEADS, NDIR * A_HEADS,
            B_HEADS * B_DK, B_HEADS * B_DK, B_HEADS * B_DV, B_HEADS * B_DV, NDIR * GLA_RANK)
IN_COLS = sum(IN_SIZES)

kernel_name = "hybrid_gdn_gla_prefix_block"


def rmsnorm(x, w):
    xf = x.astype(jnp.float32)
    y = xf * lax.rsqrt(jnp.mean(xf * xf, axis=-1, keepdims=True) + EPS) * w.astype(jnp.float32)
    return y.astype(x.dtype)


def modulate(h, shift, scale):
    return h * (1.0 + scale) + shift


def l2norm(t):
    return t * lax.rsqrt(jnp.sum(t * t, axis=-1, keepdims=True) + EPS)


def to_heads(t, n_heads):
    b, l, _ = t.shape
    return t.reshape(b, l, n_heads, -1).transpose(0, 2, 1, 3)


def from_heads(t):
    b, h, l, d = t.shape
    return t.transpose(0, 2, 1, 3).reshape(b, l, h * d)


def centred_conv_rows(u, w, n_rows):
    b, l, ch = u.shape
    row_len = l // n_rows
    pad = CONV_K // 2
    u = u.reshape(b, n_rows, row_len, ch)
    up = jnp.pad(u, ((0, 0), (0, 0), (pad, pad), (0, 0)))
    out = sum(w[tap] * up[:, :, tap:tap + row_len, :] for tap in range(CONV_K))
    return out.reshape(b, l, ch)


def gated_delta_chunked(q, k, v, beta, g, state0):
    bsz, h, l, dk = q.shape
    dv = v.shape[-1]
    c = DELTA_CHUNK
    n = l // c
    q = q.reshape(bsz, h, n, c, dk)
    k = k.reshape(bsz, h, n, c, dk)
    v = v.reshape(bsz, h, n, c, dv)
    beta = beta.reshape(bsz, h, n, c)
    gc = jnp.cumsum(g.reshape(bsz, h, n, c), axis=-1)
    idx = jnp.arange(c)
    incl = idx[:, None] >= idx[None, :]
    strict = idx[:, None] > idx[None, :]
    decay = jnp.exp(jnp.where(incl, gc[..., :, None] - gc[..., None, :], -jnp.inf))
    kb = k * beta[..., None]
    lower = jnp.where(strict, jnp.einsum('bhnid,bhnjd->bhnij', kb, k) * decay, 0.0)
    system = lower + jnp.eye(c, dtype=q.dtype)
    rhs = jnp.concatenate([v * beta[..., None], kb * jnp.exp(gc)[..., None]], axis=-1)
    sol = lax.linalg.triangular_solve(system, rhs, left_side=True, lower=True)
    u, w = sol[..., :dv], sol[..., dv:]
    qk = jnp.einsum('bhnid,bhnjd->bhnij', q, k) * decay
    q_dec = q * jnp.exp(gc)[..., None]
    k_dec = k * jnp.exp(gc[..., -1:] - gc)[..., None]
    g_last = jnp.exp(gc[..., -1])

    def step(s, xs):
        u_n, w_n, qd_n, kd_n, qk_n, gl_n = xs
        v_new = u_n - jnp.einsum('bhck,bhkv->bhcv', w_n, s)
        o_n = jnp.einsum('bhck,bhkv->bhcv', qd_n, s) + jnp.einsum('bhij,bhjv->bhiv', qk_n, v_new)
        s = s * gl_n[..., None, None] + jnp.einsum('bhck,bhcv->bhkv', kd_n, v_new)
        return s, o_n

    xs = tuple(jnp.moveaxis(t, 2, 0) for t in (u, w, q_dec, k_dec, qk, g_last))
    s_final, o = lax.scan(step, state0, xs)
    return jnp.moveaxis(o, 0, 2).reshape(bsz, h, l, dv), s_final


def gla_chunked(q, k, v, gk, state0):
    bsz, h, l, dk = q.shape
    dv = v.shape[-1]
    c = GLA_CHUNK
    n = l // c
    q = q.reshape(bsz, h, n, c, dk)
    k = k.reshape(bsz, h, n, c, dk)
    v = v.reshape(bsz, h, n, c, dv)
    bc = jnp.cumsum(gk.reshape(bsz, h, n, c, dk), axis=-2)
    idx = jnp.arange(c)
    incl = (idx[:, None] >= idx[None, :])[:, :, None]
    rel = jnp.exp(jnp.where(incl, bc[..., :, None, :] - bc[..., None, :, :], -jnp.inf))
    scores = jnp.sum(q[..., :, None, :] * k[..., None, :, :] * rel, axis=-1)
    o_intra = jnp.einsum('bhnij,bhnjv->bhniv', scores, v)
    q_dec = q * jnp.exp(bc)
    k_dec = k * jnp.exp(bc[..., -1:, :] - bc)
    g_last = jnp.exp(bc[..., -1, :])

    def step(s, xs):
        qd_n, kd_n, v_n, gl_n = xs
        o_n = jnp.einsum('bhck,bhkv->bhcv', qd_n, s)
        s = s * gl_n[..., None] + jnp.einsum('bhck,bhcv->bhkv', kd_n, v_n)
        return s, o_n

    xs = tuple(jnp.moveaxis(t, 2, 0) for t in (q_dec, k_dec, v, g_last))
    s_final, o_inter = lax.scan(step, state0, xs)
    o = jnp.moveaxis(o_inter, 0, 2) + o_intra
    return o.reshape(bsz, h, l, dv), s_final


def bidirectional_scan(scan_fn, ctx_dirs, lat_dirs, state0):
    o_ctx, o_lat = 0.0, 0.0
    for d in range(NDIR):
        c_in, x_in = ctx_dirs[d], lat_dirs[d]
        if d == 1:
            c_in = tuple(jnp.flip(t, axis=2) for t in c_in)
            x_in = tuple(jnp.flip(t, axis=2) for t in x_in)
        oc, s_ctx = scan_fn(*c_in, state0)
        ox, _ = scan_fn(*x_in, s_ctx)
        if d == 1:
            oc, ox = jnp.flip(oc, axis=2), jnp.flip(ox, axis=2)
        o_ctx = o_ctx + oc
        o_lat = o_lat + ox
    return o_ctx, o_lat


def project_side(h, n_rows, w_in, conv_w, a_log, dt_bias, gla_w2, gla_b):
    bsz, l, _ = h.shape
    p = (h @ w_in).astype(jnp.float32)
    points = [int(s) for s in np.cumsum(IN_SIZES)[:-1]]
    aq, ak, av, az, abeta, aa, bq, bk, bv, bg, blr = jnp.split(p, points, axis=-1)
    qkv = jax.nn.silu(centred_conv_rows(jnp.concatenate([aq, ak, av], axis=-1), conv_w.astype(jnp.float32), n_rows))
    aq, ak, av = jnp.split(qkv, 3, axis=-1)
    beta = jax.nn.sigmoid(abeta).reshape(bsz, l, NDIR, A_HEADS).transpose(2, 0, 3, 1)
    g = -jnp.exp(a_log.astype(jnp.float32)) * jax.nn.softplus(aa.reshape(bsz, l, NDIR, A_HEADS) + dt_bias.astype(jnp.float32))
    g = g.transpose(2, 0, 3, 1)
    gk = jnp.einsum('blnr,nrk->nblk', blr.reshape(bsz, l, NDIR, GLA_RANK), gla_w2.astype(jnp.float32))
    gk = jax.nn.log_sigmoid(gk + gla_b.astype(jnp.float32)[:, None, None, :]) / GLA_NORMALIZER
    gk = gk.reshape(NDIR, bsz, l, B_HEADS, B_DK).transpose(0, 1, 3, 2, 4)
    return {
        'a_q': l2norm(to_heads(aq, A_HEADS)) * A_DK ** -0.5,
        'a_k': l2norm(to_heads(ak, A_HEADS)),
        'a_v': to_heads(av, A_HEADS),
        'a_beta': beta, 'a_g': g, 'a_z': az,
        'b_q': to_heads(bq, B_HEADS) * B_DK ** -0.5,
        'b_k': to_heads(bk, B_HEADS),
        'b_v': to_heads(bv, B_HEADS),
        'b_gk': gk, 'b_g': bg,
    }


def gated_head_norm(o, w, z, n_heads):
    on = o * lax.rsqrt(jnp.mean(o * o, axis=-1, keepdims=True) + EPS) * w.astype(jnp.float32)
    return from_heads(on * jax.nn.silu(to_heads(z, n_heads)))


def token_mixers(h_x, h_c, n_rows, w_in, conv_w, a_log, dt_bias, gdn_norm, gla_w2, gla_b, gla_norm):
    sx = project_side(h_x, n_rows, w_in, conv_w, a_log, dt_bias, gla_w2, gla_b)
    sc = project_side(h_c, 1, w_in, conv_w, a_log, dt_bias, gla_w2, gla_b)
    bsz = h_x.shape[0]
    zero_a = jnp.zeros((bsz, A_HEADS, A_DK, A_DV), jnp.float32)
    zero_b = jnp.zeros((bsz, B_HEADS, B_DK, B_DV), jnp.float32)
    gdn_c, gdn_x = bidirectional_scan(
        gated_delta_chunked,
        [(sc['a_q'], sc['a_k'], sc['a_v'], sc['a_beta'][d], sc['a_g'][d]) for d in range(NDIR)],
        [(sx['a_q'], sx['a_k'], sx['a_v'], sx['a_beta'][d], sx['a_g'][d]) for d in range(NDIR)],
        zero_a)
    gla_c, gla_x = bidirectional_scan(
        gla_chunked,
        [(sc['b_q'], sc['b_k'], sc['b_v'], sc['b_gk'][d]) for d in range(NDIR)],
        [(sx['b_q'], sx['b_k'], sx['b_v'], sx['b_gk'][d]) for d in range(NDIR)],
        zero_b)
    mix_x = jnp.concatenate([gated_head_norm(gdn_x, gdn_norm, sx['a_z'], A_HEADS),
                             gated_head_norm(gla_x, gla_norm, sx['b_g'], B_HEADS)], axis=-1)
    mix_c = jnp.concatenate([gated_head_norm(gdn_c, gdn_norm, sc['a_z'], A_HEADS),
                             gated_head_norm(gla_c, gla_norm, sc['b_g'], B_HEADS)], axis=-1)
    return mix_x.astype(h_x.dtype), mix_c.astype(h_c.dtype)


def swiglu(h, w_gate, w_up, w_down):
    return (jax.nn.silu(h @ w_gate) * (h @ w_up)) @ w_down


def setup_inputs(seed: int = 0) -> dict:
    key = jax.random.key(seed)
    ks = jax.random.split(key, 24)
    f32 = jnp.float32
    nrm = lambda k, shape, scale: jax.random.normal(k, shape, f32) * scale
    gain = lambda k, shape: 1.0 + 0.02 * jax.random.normal(k, shape, f32)
    dt = jnp.exp(jax.random.uniform(ks[9], (DEPTH, NDIR, A_HEADS), f32, np.log(0.001), np.log(0.1)))
    return {
        'x': nrm(ks[0], (BATCH, SEQ, D_MODEL), 1.0),
        'c': nrm(ks[1], (BATCH, D_MODEL), 1.0),
        'ctx': nrm(ks[2], (BATCH, CTX_LEN, D_MODEL), 1.0),
        'c_ctx': nrm(ks[3], (D_MODEL,), 1.0),
        'w_mod': nrm(ks[4], (DEPTH, D_MODEL, 6 * D_MODEL), D_MODEL ** -0.5),
        'b_mod': nrm(ks[5], (DEPTH, 6 * D_MODEL), 0.02),
        'attn_norm': gain(ks[6], (DEPTH, D_MODEL)),
        'w_in': nrm(ks[7], (DEPTH, D_MODEL, IN_COLS), D_MODEL ** -0.5),
        'conv_w': nrm(ks[8], (DEPTH, CONV_K, 3 * A_HEADS * A_DK), CONV_K ** -0.5),
        'a_log': jnp.log(jax.random.uniform(ks[10], (DEPTH, NDIR, A_HEADS), f32, 1.0, 16.0)),
        'dt_bias': dt + jnp.log(-jnp.expm1(-dt)),
        'gdn_norm': gain(ks[11], (DEPTH, A_DV)),
        'gla_w2': nrm(ks[12], (DEPTH, NDIR, GLA_RANK, B_HEADS * B_DK), GLA_RANK ** -0.5),
        'gla_b': nrm(ks[13], (DEPTH, NDIR, B_HEADS * B_DK), 0.1),
        'gla_norm': gain(ks[14], (DEPTH, B_DV)),
        'w_out': nrm(ks[15], (DEPTH, D_MIX, D_MODEL), D_MIX ** -0.5),
        'ffn_norm': gain(ks[16], (DEPTH, D_MODEL)),
        'w_gate': nrm(ks[17], (DEPTH, D_MODEL, D_FF), D_MODEL ** -0.5),
        'w_up': nrm(ks[18], (DEPTH, D_MODEL, D_FF), D_MODEL ** -0.5),
        'w_down': nrm(ks[19], (DEPTH, D_FF, D_MODEL), D_FF ** -0.5),
        'final_norm': gain(ks[20], (D_MODEL,)),
    }


def reference(x, c, ctx, c_ctx, w_mod, b_mod, attn_norm, w_in, conv_w, a_log, dt_bias, gdn_norm,
              gla_w2, gla_b, gla_norm, w_out, ffn_norm, w_gate, w_up, w_down, final_norm):
    n_rows = x.shape[1] // GRID_W
    silu_c = jax.nn.silu(c)
    silu_cc = jax.nn.silu(c_ctx)
    for layer in range(DEPTH):
        mod_x = (silu_c @ w_mod[layer] + b_mod[layer])[:, None, :]
        mod_c = silu_cc @ w_mod[layer] + b_mod[layer]
        sh1, sc1, g1, sh2, sc2, g2 = jnp.split(mod_x, 6, axis=-1)
        csh1, csc1, cg1, csh2, csc2, cg2 = jnp.split(mod_c, 6, axis=-1)
        h_x = modulate(rmsnorm(x, attn_norm[layer]), sh1, sc1)
        h_c = modulate(rmsnorm(ctx, attn_norm[layer]), csh1, csc1)
        mix_x, mix_c = token_mixers(h_x, h_c, n_rows, w_in[layer], conv_w[layer], a_log[layer], dt_bias[layer],
                                    gdn_norm[layer], gla_w2[layer], gla_b[layer], gla_norm[layer])
        x = x + g1 * (mix_x @ w_out[layer])
        x = x + g2 * swiglu(modulate(rmsnorm(x, ffn_norm[layer]), sh2, sc2), w_gate[layer], w_up[layer], w_down[layer])
        if layer + 1 < DEPTH:
            ctx = ctx + cg1 * (mix_c @ w_out[layer])
            ctx = ctx + cg2 * swiglu(modulate(rmsnorm(ctx, ffn_norm[layer]), csh2, csc2), w_gate[layer], w_up[layer], w_down[layer])
    return rmsnorm(x, final_norm)
```

```python
import functools

import numpy as np
import jax
import jax.numpy as jnp
from jax import lax
from jax.experimental import pallas as pl
from jax.experimental.pallas import tpu as pltpu

F32 = jnp.float32
BF16 = jnp.bfloat16

GRID_W = 64
A_HEADS, A_DK, A_DV = 4, 128, 128
B_HEADS, B_DK, B_DV = 4, 64, 128
NDIR = 2
GLA_RANK = 16
GLA_NORMALIZER = 16.0
EPS = 1e-6

CHUNK = 64
TOK_TILE = 256
SMALL_W = 128
LEVELS = int(np.log2(CHUNK))

A_W = A_HEADS * A_DK
BQ_W = B_HEADS * B_DK
BV_W = B_HEADS * B_DV
P_COLS = 4 * A_W + 2 * BQ_W + 2 * BV_W + SMALL_W

VMEM_LIMIT = 56 * 1024 * 1024


def _bf(x):
    return x.astype(BF16)


def _dot(a, b):
    return jnp.dot(_bf(a), _bf(b), preferred_element_type=F32)


def _dot_nt(a, b):
    return lax.dot_general(_bf(a), _bf(b), (((1,), (1,)), ((), ())), preferred_element_type=F32)


def _dot_tn(a, b):
    return lax.dot_general(_bf(a), _bf(b), (((0,), (0,)), ((), ())), preferred_element_type=F32)


def _split3(x):
    hi = _bf(x)
    r1 = x - hi.astype(F32)
    mid = _bf(r1)
    lo = _bf(r1 - mid.astype(F32))
    return hi, mid, lo


def _dot_exact_lhs(m, x):
    hi, mid, lo = _split3(x)
    d = lambda p: jnp.dot(m, p, preferred_element_type=F32)
    return d(hi) + d(mid) + d(lo)


def _dot_exact_rhs_nt(x, m):
    hi, mid, lo = _split3(x)
    d = lambda p: lax.dot_general(p, m, (((1,), (1,)), ((), ())), preferred_element_type=F32)
    return d(hi) + d(mid) + d(lo)


def _sigmoid(x):
    return 1.0 / (1.0 + jnp.exp(-x))


def _silu(x):
    return x * _sigmoid(x)


def _softplus(x):
    return jnp.maximum(x, 0.0) + jnp.log(1.0 + jnp.exp(-jnp.abs(x)))


def _mod_kernel(cc_ref, w_ref, b_ref, o_ref):
    o_ref[...] = _dot(_silu(cc_ref[...]), w_ref[...]) + b_ref[...]


def _modulation(cc, w_mod, b_mod):
    rows, d = cc.shape
    n = w_mod.shape[1]
    tn = 1536
    return pl.pallas_call(
        _mod_kernel,
        grid=(n // tn,),
        in_specs=[pl.BlockSpec((rows, d), lambda j: (0, 0)),
                  pl.BlockSpec((d, tn), lambda j: (0, j)),
                  pl.BlockSpec((1, tn), lambda j: (0, j))],
        out_specs=pl.BlockSpec((rows, tn), lambda j: (0, j)),
        out_shape=jax.ShapeDtypeStruct((rows, n), F32),
        compiler_params=pltpu.CompilerParams(dimension_semantics=("arbitrary",),
                                             vmem_limit_bytes=VMEM_LIMIT),
        name="mod",
    )(cc, w_mod, b_mod)


def _proj_kernel(n_ctx_tiles, ctx_len,
                 x_ref, ctx_ref, sh_ref, sc_ref, nw_ref, win_ref, cw_ref, gp_ref, w2h_ref, w2m_ref, gb_ref,
                 aq_ref, ak_ref, av_ref, az_ref, bq_ref, bk_ref, bv_ref, bg_ref, gk_ref, sm_ref):
    j = pl.program_id(1)
    is_ctx = j < n_ctx_tiles
    xin = jnp.where(is_ctx, ctx_ref[0], x_ref[0])
    ms = jnp.mean(xin * xin, axis=-1, keepdims=True)
    h = xin * lax.rsqrt(ms + EPS) * nw_ref[...]
    h = h * (1.0 + sc_ref[...]) + sh_ref[...]
    p = _dot(h, win_ref[...])

    tm = xin.shape[0]
    u = p[:, :3 * A_W]
    row_len = jnp.where(is_ctx, ctx_len, GRID_W)
    pos = lax.broadcasted_iota(jnp.int32, (tm, 1), 0) % row_len
    prev = jnp.where(pos != 0, pltpu.roll(u, 1, 0), 0.0)
    nxt = jnp.where(pos != row_len - 1, pltpu.roll(u, tm - 1, 0), 0.0)
    cw = cw_ref[...]
    qkv = _silu(cw[0:1] * prev + cw[1:2] * u + cw[2:3] * nxt)

    def l2n(t, scale):
        parts = []
        for hd in range(A_HEADS):
            th = t[:, hd * A_DK:(hd + 1) * A_DK]
            parts.append(th * (lax.rsqrt(jnp.sum(th * th, axis=-1, keepdims=True) + EPS) * scale))
        return jnp.concatenate(parts, axis=-1)

    aq_ref[0] = l2n(qkv[:, :A_W], A_DK ** -0.5)
    ak_ref[0] = l2n(qkv[:, A_W:2 * A_W], 1.0)
    av_ref[0] = qkv[:, 2 * A_W:]
    az_ref[0] = p[:, 3 * A_W:4 * A_W]
    o = 4 * A_W
    bq_ref[0] = p[:, o:o + BQ_W] * (B_DK ** -0.5)
    bk_ref[0] = p[:, o + BQ_W:o + 2 * BQ_W]
    bv_ref[0] = p[:, o + 2 * BQ_W:o + 2 * BQ_W + BV_W]
    bg_ref[0] = p[:, o + 2 * BQ_W + BV_W:o + 2 * BQ_W + 2 * BV_W]

    ps = p[:, P_COLS - SMALL_W:]
    gp = gp_ref[...]
    lane = lax.broadcasted_iota(jnp.int32, (1, SMALL_W), 1)
    beta = _sigmoid(ps)
    g = -jnp.exp(gp[1:2]) * _softplus(ps + gp[0:1])
    nb = NDIR * A_HEADS
    sm_ref[0] = jnp.where(lane < nb, beta, jnp.where(lane < 2 * nb, g, 0.0))

    ph = _bf(ps)
    pm = _bf(ps - ph.astype(F32))
    d = lambda a, b: jnp.dot(a, b, preferred_element_type=F32)
    pre = d(ph, w2h_ref[...]) + d(ph, w2m_ref[...]) + d(pm, w2h_ref[...]) + gb_ref[...]
    gk_ref[0] = -_softplus(-pre) * (1.0 / GLA_NORMALIZER)


def _projection(x, ctx, mod4, attn_norm, win, conv_w, gparams, w2h, w2m, gla_b):
    bsz, seq, d = x.shape
    ctx_len = ctx.shape[1]
    tm = TOK_TILE
    nct = ctx_len // tm
    nxt = seq // tm
    ttot = ctx_len + seq
    mod_rows = mod4.shape[0]

    def tok(w):
        return pl.BlockSpec((1, tm, w), lambda b, j: (b, j, 0))

    const = lambda shape: pl.BlockSpec(shape, lambda b, j: tuple(0 for _ in shape))
    widths = [A_W, A_W, A_W, A_W, BQ_W, BQ_W, BV_W, BV_W, NDIR * BQ_W, SMALL_W]
    return pl.pallas_call(
        functools.partial(_proj_kernel, nct, ctx_len),
        grid=(bsz, nct + nxt),
        in_specs=[
            pl.BlockSpec((1, tm, d), lambda b, j: (b, jnp.maximum(j - nct, 0), 0)),
            pl.BlockSpec((1, tm, d), lambda b, j: (b, jnp.minimum(j, nct - 1), 0)),
            pl.BlockSpec((None, None, 1, d), lambda b, j: (jnp.where(j < nct, mod_rows - 1, b), 0, 0, 0)),
            pl.BlockSpec((None, None, 1, d), lambda b, j: (jnp.where(j < nct, mod_rows - 1, b), 1, 0, 0)),
            const((1, d)), const((d, P_COLS)), const(conv_w.shape), const(gparams.shape),
            const(w2h.shape), const(w2m.shape), const(gla_b.shape),
        ],
        out_specs=[tok(w) for w in widths],
        out_shape=[jax.ShapeDtypeStruct((bsz, ttot, w), F32) for w in widths],
        compiler_params=pltpu.CompilerParams(dimension_semantics=("arbitrary", "arbitrary"),
                                             vmem_limit_bytes=VMEM_LIMIT),
        name="proj",
    )(x, ctx, mod4, mod4, attn_norm, win, conv_w, gparams, w2h, w2m, gla_b)


def _scan_consts():
    c = CHUNK
    i = np.arange(c)[:, None]
    t = np.arange(c)[None, :]
    out = []
    for d in range(NDIR):
        incl = (t <= i) if d == 0 else (t >= i)
        strict = (t < i) if d == 0 else (t > i)
        tri = incl.astype(np.float32)
        mats = [tri]
        masks = []
        for lv in range(LEVELS):
            m = c >> (lv + 1)
            start = (np.arange(c) // (2 * m)) * (2 * m)
            ref = start + (m - 1 if d == 0 else m)
            mats.append(tri - tri[ref, :])
            later = (np.arange(c) % (2 * m) >= m) if d == 0 else (np.arange(c) % (2 * m) < m)
            same = (start[:, None] == start[None, :])
            masks.append((same & later[:, None] & ~later[None, :]).astype(np.float32))
        masks.append(np.eye(c, dtype=np.float32))
        out.append(dict(
            tri=jnp.asarray(tri, BF16),
            mst=jnp.asarray(np.concatenate(mats, 0), BF16),
            lmask=jnp.asarray(np.stack([np.tile(mk, (1, B_HEADS)) for mk in masks]), F32),
            incl=jnp.asarray(incl.astype(np.float32)),
            strict=jnp.asarray(strict.astype(np.float32)),
        ))
    e = np.arange(BV_W)[:, None] // B_DV
    dd = np.arange(BQ_W)[None, :] // B_DK
    bdmask = jnp.asarray((e == dd).astype(np.float32))
    return out, bdmask


def _gdn_chunk(d, aq, ak, av, sm, smT, tri, incl, strict, lmask_ref, s_ref, o_ref):
    c = CHUNK
    eye = lmask_ref[LEVELS, :, :c]
    last = c - 1 if d == 0 else 0
    gc_all = _dot_exact_lhs(tri, sm)
    gcr_all = _dot_exact_rhs_nt(smT, tri)
    inc = incl > 0.5
    for h in range(A_HEADS):
        col = NDIR * A_HEADS + A_HEADS * d + h
        bcol = A_HEADS * d + h
        gcc = gc_all[:, col:col + 1]
        gcr = gcr_all[col:col + 1, :]
        beta = sm[:, bcol:bcol + 1]
        decay = jnp.where(inc, jnp.exp(jnp.where(inc, gcc - gcr, 0.0)), 0.0)
        q = aq[:, h * A_DK:(h + 1) * A_DK]
        k = ak[:, h * A_DK:(h + 1) * A_DK]
        v = av[:, h * A_DV:(h + 1) * A_DV]
        kb = k * beta
        m = _dot_nt(jnp.concatenate([kb, q], axis=0), k)
        a = m[:c] * decay * strict
        qk = m[c:] * decay
        egc = jnp.exp(gcc)
        rhs = jnp.concatenate([v * beta, kb * egc], axis=1)
        t = eye - a * lmask_ref[LEVELS - 1, :, :c]
        for lv in range(LEVELS - 2, -1, -1):
            low = a * lmask_ref[lv, :, :c]
            t = t - _dot(_dot(t, low), t)
        r = _dot(t, rhs)
        u = r[:, :A_DV]
        w = r[:, A_DV:]
        s = s_ref[d, h]
        ws = _dot(jnp.concatenate([w, q * egc], axis=0), s)
        v_new = u - ws[:c]
        o_ref[0, :, h * A_DV:(h + 1) * A_DV] = ws[c:] + _dot(qk, v_new)
        gtot = gc_all[last:last + 1, col:col + 1]
        k_dec = k * jnp.exp(gtot - gcc)
        s_ref[d, h] = s * jnp.exp(gtot) + _dot_tn(k_dec, v_new)


def _gla_chunk(d, bq, bk, bv, gk, mst, lmask_ref, bdmask, st_ref, o_ref):
    c = CHUNK
    last = c - 1 if d == 0 else 0
    e_all = _dot_exact_lhs(mst, gk)
    bc = e_all[:c]
    lane_q = lax.broadcasted_iota(jnp.int32, (1, BQ_W), 1) // B_DK
    lane_v = lax.broadcasted_iota(jnp.int32, (1, BV_W), 1) // B_DV
    sacc = jnp.zeros((c, B_HEADS * c), F32)
    for lv in range(LEVELS + 1):
        if lv < LEVELS:
            wgt = jnp.exp(-jnp.abs(e_all[(lv + 1) * c:(lv + 2) * c]))
            ql, kl = bq * wgt, bk * wgt
        else:
            ql, kl = bq, bk
        kst = jnp.concatenate([jnp.where(lane_q == h, kl, 0.0) for h in range(B_HEADS)], axis=0)
        sacc = sacc + _dot_nt(ql, kst) * lmask_ref[lv]
    vbd = jnp.concatenate([jnp.where(lane_v == h, bv, 0.0) for h in range(B_HEADS)], axis=0)
    st = st_ref[d]
    bct = bc[last:last + 1]
    o_ref[0] = _dot(sacc, vbd) + _dot_nt(bq * jnp.exp(bc), st)
    kd = bk * jnp.exp(bct - bc)
    st_ref[d] = st * jnp.exp(bct) + _dot_tn(bv, kd) * bdmask


def _scan_kernel(*refs):
    n_in = 9
    fwd = refs[0:n_in]
    bwd = refs[n_in:2 * n_in]
    (tri_f, mst_f, lmask_f, incl_f, strict_f,
     tri_b, mst_b, lmask_b, incl_b, strict_b, bdmask_ref) = refs[2 * n_in:2 * n_in + 11]
    og_f, og_b, ol_f, ol_b = refs[2 * n_in + 11:2 * n_in + 15]
    s_ref, st_ref = refs[2 * n_in + 15:]

    @pl.when(pl.program_id(1) == 0)
    def _():
        s_ref[...] = jnp.zeros_like(s_ref)
        st_ref[...] = jnp.zeros_like(st_ref)

    bdmask = bdmask_ref[...]
    for d, (ins, tri, mst, lmask, incl, strict, og, ol) in enumerate((
            (fwd, tri_f, mst_f, lmask_f, incl_f, strict_f, og_f, ol_f),
            (bwd, tri_b, mst_b, lmask_b, incl_b, strict_b, og_b, ol_b))):
        aq, ak, av, sm, smT, bq, bk, bv, gk = ins
        _gdn_chunk(d, aq[0], ak[0], av[0], sm[0], smT[...], tri[...], incl[...], strict[...], lmask, s_ref, og)
        _gla_chunk(d, bq[0], bk[0], bv[0], gk[0], mst[...], lmask, bdmask, st_ref, ol)


def _scans(aq, ak, av, sm, smT, bq, bk, bv, gk, ctx_len):
    bsz, ttot, _ = aq.shape
    c = CHUNK
    nch = ttot // c
    ncc = ctx_len // c
    consts, bdmask = _scan_consts()

    def fidx(b, s):
        return s

    def bidx(b, s):
        return jnp.where(s < ncc, ncc - 1 - s, nch - 1 - s + ncc)

    def tok(w, idx, lane_blk=0):
        return pl.BlockSpec((1, c, w), lambda b, s: (b, idx(b, s), lane_blk))

    def dir_specs(idx, d):
        return [tok(A_W, idx), tok(A_W, idx), tok(A_W, idx), tok(SMALL_W, idx),
                pl.BlockSpec((None, None, 16, c), lambda b, s: (b, idx(b, s), 0, 0)),
                tok(BQ_W, idx), tok(BQ_W, idx), tok(BV_W, idx), tok(BQ_W, idx, d)]

    const = lambda a: pl.BlockSpec(a.shape, lambda b, s: tuple(0 for _ in a.shape))
    cargs = []
    for d in range(NDIR):
        cd = consts[d]
        cargs += [cd["tri"], cd["mst"], cd["lmask"], cd["incl"], cd["strict"]]
    cargs.append(bdmask)
    ins = [aq, ak, av, sm, smT, bq, bk, bv, gk]
    out_w = [A_HEADS * A_DV, A_HEADS * A_DV, BV_W, BV_W]
    return pl.pallas_call(
        _scan_kernel,
        grid=(bsz, nch),
        in_specs=dir_specs(fidx, 0) + dir_specs(bidx, 1) + [const(a) for a in cargs],
        out_specs=[tok(out_w[0], fidx), tok(out_w[1], bidx), tok(out_w[2], fidx), tok(out_w[3], bidx)],
        out_shape=[jax.ShapeDtypeStruct((bsz, ttot, w), F32) for w in out_w],
        scratch_shapes=[pltpu.VMEM((NDIR, A_HEADS, A_DK, A_DV), F32),
                        pltpu.VMEM((NDIR, BV_W, BQ_W), F32)],
        compiler_params=pltpu.CompilerParams(dimension_semantics=("arbitrary", "arbitrary"),
                                             vmem_limit_bytes=VMEM_LIMIT),
        name="scan",
    )(*ins, *ins, *cargs)


def _head_norm(o, w, z, n_heads, dv):
    parts = []
    for h in range(n_heads):
        oh = o[:, h * dv:(h + 1) * dv]
        parts.append(oh * lax.rsqrt(jnp.mean(oh * oh, axis=-1, keepdims=True) + EPS))
    return jnp.concatenate(parts, axis=-1) * w * _silu(z)


def _rms(x, w):
    return x * lax.rsqrt(jnp.mean(x * x, axis=-1, keepdims=True) + EPS) * w


def _post_kernel(x_ref, ogf_ref, ogb_ref, olf_ref, olb_ref, az_ref, bg_ref,
                 g1_ref, sh2_ref, sc2_ref, g2_ref, gn_ref, ln_ref, wo_ref, fn_ref,
                 wg_ref, wu_ref, wd_ref, fin_ref, o_ref):
    gdn = _head_norm(ogf_ref[0] + ogb_ref[0], gn_ref[...], az_ref[0], A_HEADS, A_DV)
    gla = _head_norm(olf_ref[0] + olb_ref[0], ln_ref[...], bg_ref[0], B_HEADS, B_DV)
    mix = jnp.concatenate([gdn, gla], axis=-1)
    x1 = x_ref[0] + g1_ref[...] * _dot(mix, wo_ref[...])
    h2 = _rms(x1, fn_ref[...]) * (1.0 + sc2_ref[...]) + sh2_ref[...]
    h2b = _bf(h2)
    gate = jnp.dot(h2b, wg_ref[...], preferred_element_type=F32)
    up = jnp.dot(h2b, wu_ref[...], preferred_element_type=F32)
    y = _dot(_silu(gate) * up, wd_ref[...])
    x2 = x1 + g2_ref[...] * y
    o_ref[0] = _rms(x2, fin_ref[...])


def _post(x, ogf, ogb, olf, olb, az, bg, mod4, gn, ln, wo, fn, wg, wu, wd, fin, ctx_len):
    bsz, seq, d = x.shape
    tm = TOK_TILE
    off = ctx_len // tm
    dff = wg.shape[1]

    def tok(w, o):
        return pl.BlockSpec((1, tm, w), lambda b, j: (b, j + o, 0))

    def modspec(k):
        return pl.BlockSpec((None, None, 1, d), lambda b, j: (b, k, 0, 0))

    def const(shape, single=True):
        idx = lambda b, j: tuple(0 for _ in shape)
        if single:
            return pl.BlockSpec(shape, idx, pipeline_mode=pl.Buffered(1))
        return pl.BlockSpec(shape, idx)

    return pl.pallas_call(
        _post_kernel,
        grid=(bsz, seq // tm),
        in_specs=[tok(d, 0), tok(A_W, off), tok(A_W, off), tok(BV_W, off), tok(BV_W, off),
                  tok(A_W, off), tok(BV_W, off),
                  modspec(2), modspec(3), modspec(4), modspec(5),
                  const((1, A_W), False), const((1, BV_W), False), const((d, d)), const((1, d), False),
                  const((d, dff)), const((d, dff)), const((dff, d)), const((1, d), False)],
        out_specs=tok(d, 0),
        out_shape=jax.ShapeDtypeStruct((bsz, seq, d), F32),
        compiler_params=pltpu.CompilerParams(dimension_semantics=("arbitrary", "arbitrary"),
                                             vmem_limit_bytes=VMEM_LIMIT),
        name="post",
    )(x, ogf, ogb, olf, olb, az, bg, mod4, mod4, mod4, mod4, gn, ln, wo, fn, wg, wu, wd, fin)


def kernel(x, c, ctx, c_ctx, w_mod, b_mod, attn_norm, w_in, conv_w, a_log, dt_bias, gdn_norm, gla_w2, gla_b,
           gla_norm, w_out, ffn_norm, w_gate, w_up, w_down, final_norm):
    bsz, seq, d = x.shape
    ctx_len = ctx.shape[1]
    assert w_mod.shape[0] == 1, "single-layer block"
    assert ctx_len == TOK_TILE and seq % TOK_TILE == 0 and TOK_TILE % GRID_W == 0 and TOK_TILE % CHUNK == 0

    rows = -(-(bsz + 1) // 8) * 8
    cc = jnp.zeros((rows, d), F32).at[:bsz].set(c).at[rows - 1].set(c_ctx)
    mod = _modulation(cc, w_mod[0], b_mod[0][None, :])
    mod4 = mod.reshape(rows, 6, 1, d)

    w = w_in[0]
    o1 = 4 * A_W
    nb = NDIR * A_HEADS
    o2 = o1 + 2 * nb
    o3 = o2 + 2 * BQ_W + 2 * BV_W
    nlr = NDIR * GLA_RANK
    win = jnp.concatenate([w[:, :o1], w[:, o2:o3], w[:, o1:o2], w[:, o3:o3 + nlr],
                           jnp.zeros((d, SMALL_W - 2 * nb - nlr), F32)], axis=1).astype(BF16)
    gparams = jnp.zeros((2, SMALL_W), F32)
    gparams = gparams.at[0, nb:2 * nb].set(dt_bias[0].reshape(-1)).at[1, nb:2 * nb].set(a_log[0].reshape(-1))
    w2 = jnp.zeros((SMALL_W, NDIR * BQ_W), F32)
    for n in range(NDIR):
        w2 = w2.at[2 * nb + n * GLA_RANK:2 * nb + (n + 1) * GLA_RANK, n * BQ_W:(n + 1) * BQ_W].set(gla_w2[0, n])
    w2h = w2.astype(BF16)
    w2m = (w2 - w2h.astype(F32)).astype(BF16)

    aq, ak, av, az, bq, bk, bv, bg, gk, sm = _projection(
        x, ctx, mod4, attn_norm, win, conv_w[0], gparams, w2h, w2m, gla_b[0].reshape(1, -1))

    ttot = ctx_len + seq
    smT = sm[:, :, :16].reshape(bsz, ttot // CHUNK, CHUNK, 16).transpose(0, 1, 3, 2)
    ogf, ogb, olf, olb = _scans(aq, ak, av, sm, smT, bq, bk, bv, gk, ctx_len)

    tile_w = lambda v, n: jnp.tile(v.reshape(1, -1), (1, n))
    return _post(x, ogf, ogb, olf, olb, az, bg, mod4,
                 tile_w(gdn_norm[0], A_HEADS), tile_w(gla_norm[0], B_HEADS),
                 w_out[0].astype(BF16), ffn_norm, w_gate[0].astype(BF16), w_up[0].astype(BF16),
                 w_down[0].astype(BF16), final_norm.reshape(1, -1), ctx_len)
```

```python
import functools

import numpy as np
import jax
import jax.numpy as jnp
from jax import lax
from jax.experimental import pallas as pl
from jax.experimental.pallas import tpu as pltpu

F32 = jnp.float32
BF16 = jnp.bfloat16

GRID_W = 64
A_HEADS, A_DK, A_DV = 4, 128, 128
B_HEADS, B_DK, B_DV = 4, 64, 128
NDIR = 2
GLA_RANK = 16
GLA_NORMALIZER = 16.0
EPS = 1e-6

CHUNK = 64
TOK_TILE = 256
SMALL_W = 128
LEVELS = int(np.log2(CHUNK))

A_W = A_HEADS * A_DK
BQ_W = B_HEADS * B_DK
BV_W = B_HEADS * B_DV
P_COLS = 4 * A_W + 2 * BQ_W + 2 * BV_W + SMALL_W

VMEM_LIMIT = 56 * 1024 * 1024


def _bf(x):
    return x.astype(BF16)


def _dot(a, b):
    return jnp.dot(_bf(a), _bf(b), preferred_element_type=F32)


def _dot_nt(a, b):
    return lax.dot_general(_bf(a), _bf(b), (((1,), (1,)), ((), ())), preferred_element_type=F32)


def _dot_tn(a, b):
    return lax.dot_general(_bf(a), _bf(b), (((0,), (0,)), ((), ())), preferred_element_type=F32)


def _split3(x):
    hi = _bf(x)
    r1 = x - hi.astype(F32)
    mid = _bf(r1)
    lo = _bf(r1 - mid.astype(F32))
    return hi, mid, lo


def _dot_exact_lhs(m, x):
    hi, mid, lo = _split3(x)
    d = lambda p: jnp.dot(m, p, preferred_element_type=F32)
    return d(hi) + d(mid) + d(lo)


def _dot_exact_rhs_nt(x, m):
    hi, mid, lo = _split3(x)
    d = lambda p: lax.dot_general(p, m, (((1,), (1,)), ((), ())), preferred_element_type=F32)
    return d(hi) + d(mid) + d(lo)


def _sigmoid(x):
    return 1.0 / (1.0 + jnp.exp(-x))


def _silu(x):
    return x * _sigmoid(x)


def _softplus(x):
    return jnp.maximum(x, 0.0) + jnp.log(1.0 + jnp.exp(-jnp.abs(x)))


def _mod_kernel(cc_ref, w_ref, b_ref, o_ref):
    o_ref[...] = _dot(_silu(cc_ref[...]), w_ref[...]) + b_ref[...]


def _modulation(cc, w_mod, b_mod):
    rows, d = cc.shape
    n = w_mod.shape[1]
    tn = 1536
    return pl.pallas_call(
        _mod_kernel,
        grid=(n // tn,),
        in_specs=[pl.BlockSpec((rows, d), lambda j: (0, 0)),
                  pl.BlockSpec((d, tn), lambda j: (0, j)),
                  pl.BlockSpec((1, tn), lambda j: (0, j))],
        out_specs=pl.BlockSpec((rows, tn), lambda j: (0, j)),
        out_shape=jax.ShapeDtypeStruct((rows, n), F32),
        compiler_params=pltpu.CompilerParams(dimension_semantics=("arbitrary",),
                                             vmem_limit_bytes=VMEM_LIMIT),
        name="mod",
    )(cc, w_mod, b_mod)


def _proj_kernel(n_ctx_tiles, ctx_len,
                 x_ref, ctx_ref, sh_ref, sc_ref, nw_ref, win_ref, cw_ref, gp_ref, w2h_ref, w2m_ref, gb_ref,
                 aq_ref, ak_ref, av_ref, az_ref, bq_ref, bk_ref, bv_ref, bg_ref, gk_ref, sm_ref):
    j = pl.program_id(1)
    is_ctx = j < n_ctx_tiles
    xin = jnp.where(is_ctx, ctx_ref[0], x_ref[0])
    ms = jnp.mean(xin * xin, axis=-1, keepdims=True)
    h = xin * lax.rsqrt(ms + EPS) * nw_ref[...]
    h = h * (1.0 + sc_ref[...]) + sh_ref[...]
    p = _dot(h, win_ref[...])

    tm = xin.shape[0]
    u = p[:, :3 * A_W]
    row_len = jnp.where(is_ctx, ctx_len, GRID_W)
    pos = lax.broadcasted_iota(jnp.int32, (tm, 1), 0) % row_len
    prev = jnp.where(pos != 0, pltpu.roll(u, 1, 0), 0.0)
    nxt = jnp.where(pos != row_len - 1, pltpu.roll(u, tm - 1, 0), 0.0)
    cw = cw_ref[...]
    qkv = _silu(cw[0:1] * prev + cw[1:2] * u + cw[2:3] * nxt)

    def l2n(t, scale):
        parts = []
        for hd in range(A_HEADS):
            th = t[:, hd * A_DK:(hd + 1) * A_DK]
            parts.append(th * (lax.rsqrt(jnp.sum(th * th, axis=-1, keepdims=True) + EPS) * scale))
        return jnp.concatenate(parts, axis=-1)

    aq_ref[0] = l2n(qkv[:, :A_W], A_DK ** -0.5)
    ak_ref[0] = l2n(qkv[:, A_W:2 * A_W], 1.0)
    av_ref[0] = qkv[:, 2 * A_W:]
    az_ref[0] = p[:, 3 * A_W:4 * A_W]
    o = 4 * A_W
    bq_ref[0] = p[:, o:o + BQ_W] * (B_DK ** -0.5)
    bk_ref[0] = p[:, o + BQ_W:o + 2 * BQ_W]
    bv_ref[0] = p[:, o + 2 * BQ_W:o + 2 * BQ_W + BV_W]
    bg_ref[0] = p[:, o + 2 * BQ_W + BV_W:o + 2 * BQ_W + 2 * BV_W]

    ps = p[:, P_COLS - SMALL_W:]
    gp = gp_ref[...]
    lane = lax.broadcasted_iota(jnp.int32, (1, SMALL_W), 1)
    beta = _sigmoid(ps)
    g = -jnp.exp(gp[1:2]) * _softplus(ps + gp[0:1])
    nb = NDIR * A_HEADS
    sm_ref[0] = jnp.where(lane < nb, beta, jnp.where(lane < 2 * nb, g, 0.0))

    ph = _bf(ps)
    pm = _bf(ps - ph.astype(F32))
    d = lambda a, b: jnp.dot(a, b, preferred_element_type=F32)
    pre = d(ph, w2h_ref[...]) + d(ph, w2m_ref[...]) + d(pm, w2h_ref[...]) + gb_ref[...]
    gk_ref[0] = -_softplus(-pre) * (1.0 / GLA_NORMALIZER)


def _projection(x, ctx, mod4, attn_norm, win, conv_w, gparams, w2h, w2m, gla_b):
    bsz, seq, d = x.shape
    ctx_len = ctx.shape[1]
    tm = TOK_TILE
    nct = ctx_len // tm
    nxt = seq // tm
    ttot = ctx_len + seq
    mod_rows = mod4.shape[0]

    def tok(w):
        return pl.BlockSpec((1, tm, w), lambda b, j: (b, j, 0))

    const = lambda shape: pl.BlockSpec(shape, lambda b, j: tuple(0 for _ in shape))
    widths = [A_W, A_W, A_W, A_W, BQ_W, BQ_W, BV_W, BV_W, NDIR * BQ_W, SMALL_W]
    return pl.pallas_call(
        functools.partial(_proj_kernel, nct, ctx_len),
        grid=(bsz, nct + nxt),
        in_specs=[
            pl.BlockSpec((1, tm, d), lambda b, j: (b, jnp.maximum(j - nct, 0), 0)),
            pl.BlockSpec((1, tm, d), lambda b, j: (b, jnp.minimum(j, nct - 1), 0)),
            pl.BlockSpec((None, None, 1, d), lambda b, j: (jnp.where(j < nct, mod_rows - 1, b), 0, 0, 0)),
            pl.BlockSpec((None, None, 1, d), lambda b, j: (jnp.where(j < nct, mod_rows - 1, b), 1, 0, 0)),
            const((1, d)), const((d, P_COLS)), const(conv_w.shape), const(gparams.shape),
            const(w2h.shape), const(w2m.shape), const(gla_b.shape),
        ],
        out_specs=[tok(w) for w in widths],
        out_shape=[jax.ShapeDtypeStruct((bsz, ttot, w), F32) for w in widths],
        compiler_params=pltpu.CompilerParams(dimension_semantics=("arbitrary", "arbitrary"),
                                             vmem_limit_bytes=VMEM_LIMIT),
        name="proj",
    )(x, ctx, mod4, mod4, attn_norm, win, conv_w, gparams, w2h, w2m, gla_b)


def _scan_consts():
    c = CHUNK
    i = np.arange(c)[:, None]
    t = np.arange(c)[None, :]
    out = []
    for d in range(NDIR):
        incl = (t <= i) if d == 0 else (t >= i)
        strict = (t < i) if d == 0 else (t > i)
        tri = incl.astype(np.float32)
        mats = [tri]
        masks = []
        for lv in range(LEVELS):
            m = c >> (lv + 1)
            start = (np.arange(c) // (2 * m)) * (2 * m)
            ref = start + (m - 1 if d == 0 else m)
            mats.append(tri - tri[ref, :])
            later = (np.arange(c) % (2 * m) >= m) if d == 0 else (np.arange(c) % (2 * m) < m)
            same = (start[:, None] == start[None, :])
            masks.append((same & later[:, None] & ~later[None, :]).astype(np.float32))
        masks.append(np.eye(c, dtype=np.float32))
        out.append(dict(
            tri=jnp.asarray(tri, BF16),
            mst=jnp.asarray(np.concatenate(mats, 0), BF16),
            lmask=jnp.asarray(np.stack([np.tile(mk, (1, B_HEADS)) for mk in masks]), F32),
            incl=jnp.asarray(incl.astype(np.float32)),
            strict=jnp.asarray(strict.astype(np.float32)),
        ))
    e = np.arange(BV_W)[:, None] // B_DV
    dd = np.arange(BQ_W)[None, :] // B_DK
    bdmask = jnp.asarray((e == dd).astype(np.float32))
    return out, bdmask


def _gdn_stages(dirs, s_ref):
    c = CHUNK
    units = []

    def prep():
        for dd in dirs:
            d = dd["d"]
            last = c - 1 if d == 0 else 0
            gc_all = _dot_exact_lhs(dd["tri"], dd["sm"])
            gcr_all = _dot_exact_rhs_nt(dd["smT"], dd["tri"])
            inc = dd["incl"] > 0.5
            for h in range(A_HEADS):
                col = NDIR * A_HEADS + A_HEADS * d + h
                bcol = A_HEADS * d + h
                u = dict(d=d, h=h, dd=dd, inc=inc,
                         gcc=gc_all[:, col:col + 1], gcr=gcr_all[col:col + 1, :],
                         gtot=gc_all[last:last + 1, col:col + 1], beta=dd["sm"][:, bcol:bcol + 1],
                         q=dd["aq"][:, h * A_DK:(h + 1) * A_DK], k=dd["ak"][:, h * A_DK:(h + 1) * A_DK],
                         v=dd["av"][:, h * A_DV:(h + 1) * A_DV])
                u["kb"] = u["k"] * u["beta"]
                units.append(u)
        for u in units:
            u["m"] = _dot_nt(jnp.concatenate([u["kb"], u["q"]], axis=0), u["k"])
        for u in units:
            inc, lm = u["inc"], u["dd"]["lmask"]
            decay = jnp.where(inc, jnp.exp(jnp.where(inc, u["gcc"] - u["gcr"], 0.0)), 0.0)
            u["a"] = u["m"][:c] * decay * u["dd"]["strict"]
            u["qk"] = u["m"][c:] * decay
            u["egc"] = jnp.exp(u["gcc"])
            u["rhs"] = jnp.concatenate([u["v"] * u["beta"], u["kb"] * u["egc"]], axis=1)
            u["t"] = lm[LEVELS, :, :c] - u["a"] * lm[LEVELS - 1, :, :c]

    def level(lv):
        def run():
            for u in units:
                u["tl"] = _dot(u["t"], u["a"] * u["dd"]["lmask"][lv, :, :c])
            for u in units:
                u["t"] = u["t"] - _dot(u["tl"], u["t"])
        return run

    def recur():
        for u in units:
            u["r"] = _dot(u["t"], u["rhs"])
        for u in units:
            u["s"] = s_ref[u["d"], u["h"]]
            lhs = jnp.concatenate([u["r"][:, A_DV:], u["q"] * u["egc"]], axis=0)
            u["ws"] = _dot(lhs, u["s"])
        for u in units:
            u["vn"] = u["r"][:, :A_DV] - u["ws"][:c]
        for u in units:
            h = u["h"]
            u["dd"]["og"][0, :, h * A_DV:(h + 1) * A_DV] = u["ws"][c:] + _dot(u["qk"], u["vn"])
        for u in units:
            k_dec = u["k"] * jnp.exp(u["gtot"] - u["gcc"])
            s_ref[u["d"], u["h"]] = u["s"] * jnp.exp(u["gtot"]) + _dot_tn(k_dec, u["vn"])

    return [prep] + [level(lv) for lv in range(LEVELS - 2, -1, -1)] + [recur]


def _gla_stages(dirs, bdmask, st_ref):
    c = CHUNK
    lane_q = lax.broadcasted_iota(jnp.int32, (1, BQ_W), 1) // B_DK
    lane_v = lax.broadcasted_iota(jnp.int32, (1, BV_W), 1) // B_DV

    def prep():
        for dd in dirs:
            dd["e_all"] = _dot_exact_lhs(dd["mst"], dd["gk"])
            dd["sacc"] = jnp.zeros((c, B_HEADS * c), F32)

    def level(lv):
        def run():
            for dd in dirs:
                if lv < LEVELS:
                    wgt = jnp.exp(-jnp.abs(dd["e_all"][(lv + 1) * c:(lv + 2) * c]))
                    ql, kl = dd["bq"] * wgt, dd["bk"] * wgt
                else:
                    ql, kl = dd["bq"], dd["bk"]
                kst = jnp.concatenate([jnp.where(lane_q == h, kl, 0.0) for h in range(B_HEADS)], axis=0)
                dd["sacc"] = dd["sacc"] + _dot_nt(ql, kst) * dd["lmask"][lv]
        return run

    def recur():
        for dd in dirs:
            last = c - 1 if dd["d"] == 0 else 0
            bc = dd["e_all"][:c]
            bv = dd["bv"]
            vbd = jnp.concatenate([jnp.where(lane_v == h, bv, 0.0) for h in range(B_HEADS)], axis=0)
            st = st_ref[dd["d"]]
            bct = bc[last:last + 1]
            dd["ol"][0] = _dot(dd["sacc"], vbd) + _dot_nt(dd["bq"] * jnp.exp(bc), st)
            kd = dd["bk"] * jnp.exp(bct - bc)
            st_ref[dd["d"]] = st * jnp.exp(bct) + _dot_tn(bv, kd) * bdmask

    return [prep] + [level(lv) for lv in range(LEVELS + 1)] + [recur]


def _scan_kernel(*refs):
    n_in = 9
    n_c = 5
    og_f, og_b, ol_f, ol_b = refs[2 * n_in + 2 * n_c + 1:2 * n_in + 2 * n_c + 5]
    s_ref, st_ref = refs[2 * n_in + 2 * n_c + 5:]
    bdmask_ref = refs[2 * n_in + 2 * n_c]

    @pl.when(pl.program_id(1) == 0)
    def _():
        s_ref[...] = jnp.zeros_like(s_ref)
        st_ref[...] = jnp.zeros_like(st_ref)

    dirs = []
    for d, (og, ol) in enumerate(((og_f, ol_f), (og_b, ol_b))):
        aq, ak, av, sm, smT, bq, bk, bv, gk = refs[d * n_in:(d + 1) * n_in]
        tri, mst, lmask, incl, strict = refs[2 * n_in + d * n_c:2 * n_in + (d + 1) * n_c]
        dirs.append(dict(d=d, aq=aq[0], ak=ak[0], av=av[0], sm=sm[0], smT=smT[...], bq=bq[0], bk=bk[0], bv=bv[0],
                         gk=gk[0], tri=tri[...], mst=mst[...], lmask=lmask, incl=incl[...], strict=strict[...],
                         og=og, ol=ol))
    gdn = _gdn_stages(dirs, s_ref)
    gla = _gla_stages(dirs, bdmask_ref[...], st_ref)
    for i in range(max(len(gdn), len(gla))):
        if i < len(gdn):
            gdn[i]()
        if i < len(gla):
            gla[i]()


def _scans(aq, ak, av, sm, smT, bq, bk, bv, gk, ctx_len):
    bsz, ttot, _ = aq.shape
    c = CHUNK
    nch = ttot // c
    ncc = ctx_len // c
    consts, bdmask = _scan_consts()

    def fidx(b, s):
        return s

    def bidx(b, s):
        return jnp.where(s < ncc, ncc - 1 - s, nch - 1 - s + ncc)

    def tok(w, idx, lane_blk=0):
        return pl.BlockSpec((1, c, w), lambda b, s: (b, idx(b, s), lane_blk))

    def dir_specs(idx, d):
        return [tok(A_W, idx), tok(A_W, idx), tok(A_W, idx), tok(SMALL_W, idx),
                pl.BlockSpec((None, None, 16, c), lambda b, s: (b, idx(b, s), 0, 0)),
                tok(BQ_W, idx), tok(BQ_W, idx), tok(BV_W, idx), tok(BQ_W, idx, d)]

    const = lambda a: pl.BlockSpec(a.shape, lambda b, s: tuple(0 for _ in a.shape))
    cargs = []
    for d in range(NDIR):
        cd = consts[d]
        cargs += [cd["tri"], cd["mst"], cd["lmask"], cd["incl"], cd["strict"]]
    cargs.append(bdmask)
    ins = [aq, ak, av, sm, smT, bq, bk, bv, gk]
    out_w = [A_HEADS * A_DV, A_HEADS * A_DV, BV_W, BV_W]
    return pl.pallas_call(
        _scan_kernel,
        grid=(bsz, nch),
        in_specs=dir_specs(fidx, 0) + dir_specs(bidx, 1) + [const(a) for a in cargs],
        out_specs=[tok(out_w[0], fidx), tok(out_w[1], bidx), tok(out_w[2], fidx), tok(out_w[3], bidx)],
        out_shape=[jax.ShapeDtypeStruct((bsz, ttot, w), F32) for w in out_w],
        scratch_shapes=[pltpu.VMEM((NDIR, A_HEADS, A_DK, A_DV), F32),
                        pltpu.VMEM((NDIR, BV_W, BQ_W), F32)],
        compiler_params=pltpu.CompilerParams(dimension_semantics=("arbitrary", "arbitrary"),
                                             vmem_limit_bytes=VMEM_LIMIT),
        name="scan",
    )(*ins, *ins, *cargs)


def _head_norm(o, w, z, n_heads, dv):
    parts = []
    for h in range(n_heads):
        oh = o[:, h * dv:(h + 1) * dv]
        parts.append(oh * lax.rsqrt(jnp.mean(oh * oh, axis=-1, keepdims=True) + EPS))
    return jnp.concatenate(parts, axis=-1) * w * _silu(z)


def _rms(x, w):
    return x * lax.rsqrt(jnp.mean(x * x, axis=-1, keepdims=True) + EPS) * w


def _post_kernel(x_ref, ogf_ref, ogb_ref, olf_ref, olb_ref, az_ref, bg_ref,
                 g1_ref, sh2_ref, sc2_ref, g2_ref, gn_ref, ln_ref, wo_ref, fn_ref,
                 wg_ref, wu_ref, wd_ref, fin_ref, o_ref):
    gdn = _head_norm(ogf_ref[0] + ogb_ref[0], gn_ref[...], az_ref[0], A_HEADS, A_DV)
    gla = _head_norm(olf_ref[0] + olb_ref[0], ln_ref[...], bg_ref[0], B_HEADS, B_DV)
    mix = jnp.concatenate([gdn, gla], axis=-1)
    x1 = x_ref[0] + g1_ref[...] * _dot(mix, wo_ref[...])
    h2 = _rms(x1, fn_ref[...]) * (1.0 + sc2_ref[...]) + sh2_ref[...]
    h2b = _bf(h2)
    gate = jnp.dot(h2b, wg_ref[...], preferred_element_type=F32)
    up = jnp.dot(h2b, wu_ref[...], preferred_element_type=F32)
    y = _dot(_silu(gate) * up, wd_ref[...])
    x2 = x1 + g2_ref[...] * y
    o_ref[0] = _rms(x2, fin_ref[...])


def _post(x, ogf, ogb, olf, olb, az, bg, mod4, gn, ln, wo, fn, wg, wu, wd, fin, ctx_len):
    bsz, seq, d = x.shape
    tm = TOK_TILE
    off = ctx_len // tm
    dff = wg.shape[1]

    def tok(w, o):
        return pl.BlockSpec((1, tm, w), lambda b, j: (b, j + o, 0))

    def modspec(k):
        return pl.BlockSpec((None, None, 1, d), lambda b, j: (b, k, 0, 0))

    def const(shape, single=True):
        idx = lambda b, j: tuple(0 for _ in shape)
        if single:
            return pl.BlockSpec(shape, idx, pipeline_mode=pl.Buffered(1))
        return pl.BlockSpec(shape, idx)

    return pl.pallas_call(
        _post_kernel,
        grid=(bsz, seq // tm),
        in_specs=[tok(d, 0), tok(A_W, off), tok(A_W, off), tok(BV_W, off), tok(BV_W, off),
                  tok(A_W, off), tok(BV_W, off),
                  modspec(2), modspec(3), modspec(4), modspec(5),
                  const((1, A_W), False), const((1, BV_W), False), const((d, d)), const((1, d), False),
                  const((d, dff)), const((d, dff)), const((dff, d)), const((1, d), False)],
        out_specs=tok(d, 0),
        out_shape=jax.ShapeDtypeStruct((bsz, seq, d), F32),
        compiler_params=pltpu.CompilerParams(dimension_semantics=("arbitrary", "arbitrary"),
                                             vmem_limit_bytes=VMEM_LIMIT),
        name="post",
    )(x, ogf, ogb, olf, olb, az, bg, mod4, mod4, mod4, mod4, gn, ln, wo, fn, wg, wu, wd, fin)


def kernel(x, c, ctx, c_ctx, w_mod, b_mod, attn_norm, w_in, conv_w, a_log, dt_bias, gdn_norm, gla_w2, gla_b,
           gla_norm, w_out, ffn_norm, w_gate, w_up, w_down, final_norm):
    bsz, seq, d = x.shape
    ctx_len = ctx.shape[1]
    assert w_mod.shape[0] == 1, "single-layer block"
    assert ctx_len == TOK_TILE and seq % TOK_TILE == 0 and TOK_TILE % GRID_W == 0 and TOK_TILE % CHUNK == 0

    rows = -(-(bsz + 1) // 8) * 8
    cc = jnp.zeros((rows, d), F32).at[:bsz].set(c).at[rows - 1].set(c_ctx)
    mod = _modulation(cc, w_mod[0], b_mod[0][None, :])
    mod4 = mod.reshape(rows, 6, 1, d)

    w = w_in[0]
    o1 = 4 * A_W
    nb = NDIR * A_HEADS
    o2 = o1 + 2 * nb
    o3 = o2 + 2 * BQ_W + 2 * BV_W
    nlr = NDIR * GLA_RANK
    win = jnp.concatenate([w[:, :o1], w[:, o2:o3], w[:, o1:o2], w[:, o3:o3 + nlr],
                           jnp.zeros((d, SMALL_W - 2 * nb - nlr), F32)], axis=1).astype(BF16)
    gparams = jnp.zeros((2, SMALL_W), F32)
    gparams = gparams.at[0, nb:2 * nb].set(dt_bias[0].reshape(-1)).at[1, nb:2 * nb].set(a_log[0].reshape(-1))
    w2 = jnp.zeros((SMALL_W, NDIR * BQ_W), F32)
    for n in range(NDIR):
        w2 = w2.at[2 * nb + n * GLA_RANK:2 * nb + (n + 1) * GLA_RANK, n * BQ_W:(n + 1) * BQ_W].set(gla_w2[0, n])
    w2h = w2.astype(BF16)
    w2m = (w2 - w2h.astype(F32)).astype(BF16)

    aq, ak, av, az, bq, bk, bv, bg, gk, sm = _projection(
        x, ctx, mod4, attn_norm, win, conv_w[0], gparams, w2h, w2m, gla_b[0].reshape(1, -1))

    ttot = ctx_len + seq
    smT = sm[:, :, :16].reshape(bsz, ttot // CHUNK, CHUNK, 16).transpose(0, 1, 3, 2)
    ogf, ogb, olf, olb = _scans(aq, ak, av, sm, smT, bq, bk, bv, gk, ctx_len)

    tile_w = lambda v, n: jnp.tile(v.reshape(1, -1), (1, n))
    return _post(x, ogf, ogb, olf, olb, az, bg, mod4,
                 tile_w(gdn_norm[0], A_HEADS), tile_w(gla_norm[0], B_HEADS),
                 w_out[0].astype(BF16), ffn_norm, w_gate[0].astype(BF16), w_up[0].astype(BF16),
                 w_down[0].astype(BF16), final_norm.reshape(1, -1), ctx_len)
```

```python
import functools

import numpy as np
import jax
import jax.numpy as jnp
from jax import lax
from jax.experimental import pallas as pl
from jax.experimental.pallas import tpu as pltpu

F32 = jnp.float32
BF16 = jnp.bfloat16

GRID_W = 64
A_HEADS, A_DK, A_DV = 4, 128, 128
B_HEADS, B_DK, B_DV = 4, 64, 128
NDIR = 2
GLA_RANK = 16
GLA_NORMALIZER = 16.0
EPS = 1e-6

CHUNK = 64
STEP_CHUNKS = 2
TOK_TILE = 256
SMALL_W = 128
LEVELS = int(np.log2(CHUNK))

A_W = A_HEADS * A_DK
BQ_W = B_HEADS * B_DK
BV_W = B_HEADS * B_DV
P_COLS = 4 * A_W + 2 * BQ_W + 2 * BV_W + SMALL_W

VMEM_LIMIT = 56 * 1024 * 1024


def _bf(x):
    return x.astype(BF16)


def _dot(a, b):
    return jnp.dot(_bf(a), _bf(b), preferred_element_type=F32)


def _dot_nt(a, b):
    return lax.dot_general(_bf(a), _bf(b), (((1,), (1,)), ((), ())), preferred_element_type=F32)


def _dot_tn(a, b):
    return lax.dot_general(_bf(a), _bf(b), (((0,), (0,)), ((), ())), preferred_element_type=F32)


def _split3(x):
    hi = _bf(x)
    r1 = x - hi.astype(F32)
    mid = _bf(r1)
    lo = _bf(r1 - mid.astype(F32))
    return hi, mid, lo


def _dot_exact_lhs(m, x, terms=3):
    d = lambda p: jnp.dot(m, p, preferred_element_type=F32)
    parts = _split3(x)[:terms]
    acc = d(parts[0])
    for p in parts[1:]:
        acc = acc + d(p)
    return acc


def _dot_exact_rhs_nt(x, m):
    hi, mid, lo = _split3(x)
    d = lambda p: lax.dot_general(p, m, (((1,), (1,)), ((), ())), preferred_element_type=F32)
    return d(hi) + d(mid) + d(lo)


def _sigmoid(x):
    return 1.0 / (1.0 + jnp.exp(-x))


def _silu(x):
    return x * _sigmoid(x)


def _softplus(x):
    return jnp.maximum(x, 0.0) + jnp.log(1.0 + jnp.exp(-jnp.abs(x)))


def _mod_kernel(cc_ref, w_ref, b_ref, o_ref):
    o_ref[...] = _dot(_silu(cc_ref[...]), w_ref[...]) + b_ref[...]


def _modulation(cc, w_mod, b_mod):
    rows, d = cc.shape
    n = w_mod.shape[1]
    tn = 1536
    return pl.pallas_call(
        _mod_kernel,
        grid=(n // tn,),
        in_specs=[pl.BlockSpec((rows, d), lambda j: (0, 0)),
                  pl.BlockSpec((d, tn), lambda j: (0, j)),
                  pl.BlockSpec((1, tn), lambda j: (0, j))],
        out_specs=pl.BlockSpec((rows, tn), lambda j: (0, j)),
        out_shape=jax.ShapeDtypeStruct((rows, n), F32),
        compiler_params=pltpu.CompilerParams(dimension_semantics=("arbitrary",),
                                             vmem_limit_bytes=VMEM_LIMIT),
        name="mod",
    )(cc, w_mod, b_mod)


def _proj_kernel(n_ctx_tiles, ctx_len,
                 x_ref, ctx_ref, sh_ref, sc_ref, nw_ref, win_ref, cw_ref, gp_ref, w2h_ref, w2m_ref, gb_ref,
                 aq_ref, ak_ref, av_ref, az_ref, bq_ref, bk_ref, bv_ref, bg_ref, gk_ref, sm_ref):
    j = pl.program_id(1)
    is_ctx = j < n_ctx_tiles
    xin = jnp.where(is_ctx, ctx_ref[0], x_ref[0])
    ms = jnp.mean(xin * xin, axis=-1, keepdims=True)
    h = xin * lax.rsqrt(ms + EPS) * (nw_ref[...] * (1.0 + sc_ref[...])) + sh_ref[...]
    p = _dot(h, win_ref[...])

    tm = xin.shape[0]
    u = p[:, :3 * A_W]
    tok = lax.broadcasted_iota(jnp.int32, (tm, 1), 0)
    pos = jnp.where(is_ctx, tok, tok % GRID_W)
    last = jnp.where(is_ctx, ctx_len - 1, GRID_W - 1)
    prev = jnp.where(pos != 0, pltpu.roll(u, 1, 0), 0.0)
    nxt = jnp.where(pos != last, pltpu.roll(u, tm - 1, 0), 0.0)
    cw = cw_ref[...]
    qkv = _silu(cw[0:1] * prev + cw[1:2] * u + cw[2:3] * nxt)

    def l2n(t, scale):
        parts = []
        for hd in range(A_HEADS):
            th = t[:, hd * A_DK:(hd + 1) * A_DK]
            parts.append(th * (lax.rsqrt(jnp.sum(th * th, axis=-1, keepdims=True) + EPS) * scale))
        return jnp.concatenate(parts, axis=-1)

    aq_ref[0] = l2n(qkv[:, :A_W], A_DK ** -0.5)
    ak_ref[0] = l2n(qkv[:, A_W:2 * A_W], 1.0)
    av_ref[0] = qkv[:, 2 * A_W:]
    az_ref[0] = p[:, 3 * A_W:4 * A_W]
    o = 4 * A_W
    bq_ref[0] = p[:, o:o + BQ_W] * (B_DK ** -0.5)
    bk_ref[0] = p[:, o + BQ_W:o + 2 * BQ_W]
    bv_ref[0] = p[:, o + 2 * BQ_W:o + 2 * BQ_W + BV_W]
    bg_ref[0] = p[:, o + 2 * BQ_W + BV_W:o + 2 * BQ_W + 2 * BV_W]

    ps = p[:, P_COLS - SMALL_W:]
    gp = gp_ref[...]
    lane = lax.broadcasted_iota(jnp.int32, (1, SMALL_W), 1)
    beta = _sigmoid(ps)
    g = -jnp.exp(gp[1:2]) * _softplus(ps + gp[0:1])
    nb = NDIR * A_HEADS
    sm_ref[0] = jnp.where(lane < nb, beta, jnp.where(lane < 2 * nb, g, 0.0))

    ph = _bf(ps)
    pm = _bf(ps - ph.astype(F32))
    d = lambda a, b: jnp.dot(a, b, preferred_element_type=F32)
    pre = d(ph, w2h_ref[...]) + d(ph, w2m_ref[...]) + d(pm, w2h_ref[...]) + gb_ref[...]
    gk_ref[0] = -_softplus(-pre) * (1.0 / GLA_NORMALIZER)


def _projection(x, ctx, mod4, attn_norm, win, conv_w, gparams, w2h, w2m, gla_b):
    bsz, seq, d = x.shape
    ctx_len = ctx.shape[1]
    tm = TOK_TILE
    nct = ctx_len // tm
    nxt = seq // tm
    ttot = ctx_len + seq
    mod_rows = mod4.shape[0]

    def tok(w):
        return pl.BlockSpec((1, tm, w), lambda b, j: (b, j, 0))

    const = lambda shape: pl.BlockSpec(shape, lambda b, j: tuple(0 for _ in shape))
    widths = [A_W, A_W, A_W, A_W, BQ_W, BQ_W, BV_W, BV_W, NDIR * BQ_W, SMALL_W]
    return pl.pallas_call(
        functools.partial(_proj_kernel, nct, ctx_len),
        grid=(bsz, nct + nxt),
        in_specs=[
            pl.BlockSpec((1, tm, d), lambda b, j: (b, jnp.maximum(j - nct, 0), 0)),
            pl.BlockSpec((1, tm, d), lambda b, j: (b, jnp.minimum(j, nct - 1), 0)),
            pl.BlockSpec((None, None, 1, d), lambda b, j: (jnp.where(j < nct, mod_rows - 1, b), 0, 0, 0)),
            pl.BlockSpec((None, None, 1, d), lambda b, j: (jnp.where(j < nct, mod_rows - 1, b), 1, 0, 0)),
            const((1, d)), const((d, P_COLS)), const(conv_w.shape), const(gparams.shape),
            const(w2h.shape), const(w2m.shape), const(gla_b.shape),
        ],
        out_specs=[tok(w) for w in widths],
        out_shape=[jax.ShapeDtypeStruct((bsz, ttot, w), F32) for w in widths],
        compiler_params=pltpu.CompilerParams(dimension_semantics=("arbitrary", "arbitrary"),
                                             vmem_limit_bytes=VMEM_LIMIT),
        name="proj",
    )(x, ctx, mod4, mod4, attn_norm, win, conv_w, gparams, w2h, w2m, gla_b)


def _scan_consts():
    c = CHUNK
    i = np.arange(c)[:, None]
    t = np.arange(c)[None, :]
    out = []
    for d in range(NDIR):
        incl = (t <= i) if d == 0 else (t >= i)
        strict = (t < i) if d == 0 else (t > i)
        tri = incl.astype(np.float32)
        mats = [tri]
        masks = []
        for lv in range(LEVELS):
            m = c >> (lv + 1)
            start = (np.arange(c) // (2 * m)) * (2 * m)
            ref = start + (m - 1 if d == 0 else m)
            mats.append(tri - tri[ref, :])
            later = (np.arange(c) % (2 * m) >= m) if d == 0 else (np.arange(c) % (2 * m) < m)
            same = (start[:, None] == start[None, :])
            masks.append((same & later[:, None] & ~later[None, :]).astype(np.float32))
        masks.append(np.eye(c, dtype=np.float32))
        out.append(dict(
            tri=jnp.asarray(tri, BF16),
            mst=jnp.asarray(np.concatenate(mats, 0), BF16),
            lmask=jnp.asarray(np.stack([np.tile(mk, (1, B_HEADS)) for mk in masks]), F32),
            incl=jnp.asarray(incl.astype(np.float32)),
            strict=jnp.asarray(strict.astype(np.float32)),
        ))
    e = np.arange(BV_W)[:, None] // B_DV
    dd = np.arange(BQ_W)[None, :] // B_DK
    bdmask = jnp.asarray((e == dd).astype(np.float32))
    return out, bdmask


def _gdn_stages(views, s_ref):
    c = CHUNK
    units = []
    state = {}

    def prep():
        for vw in views:
            d = vw["d"]
            last = c - 1 if d == 0 else 0
            gc_all = _dot_exact_lhs(vw["tri"], vw["sm"])
            gcr_all = _dot_exact_rhs_nt(vw["smT"], vw["tri"])
            inc = vw["incl"] > 0.5
            for h in range(A_HEADS):
                col = NDIR * A_HEADS + A_HEADS * d + h
                bcol = A_HEADS * d + h
                u = dict(d=d, h=h, vw=vw, inc=inc,
                         gcc=gc_all[:, col:col + 1], gcr=gcr_all[col:col + 1, :],
                         gtot=gc_all[last:last + 1, col:col + 1], beta=vw["sm"][:, bcol:bcol + 1],
                         q=vw["aq"][:, h * A_DK:(h + 1) * A_DK], k=vw["ak"][:, h * A_DK:(h + 1) * A_DK],
                         v=vw["av"][:, h * A_DV:(h + 1) * A_DV])
                u["kb"] = u["k"] * u["beta"]
                units.append(u)
        for u in units:
            u["m"] = _dot_nt(jnp.concatenate([u["kb"], u["q"]], axis=0), u["k"])
        for u in units:
            inc, lm = u["inc"], u["vw"]["lmask"]
            decay = jnp.where(inc, jnp.exp(jnp.where(inc, u["gcc"] - u["gcr"], 0.0)), 0.0)
            u["a"] = u["m"][:c] * decay * u["vw"]["strict"]
            u["qk"] = u["m"][c:] * decay
            u["egc"] = jnp.exp(u["gcc"])
            u["rhs"] = jnp.concatenate([u["v"] * u["beta"], u["kb"] * u["egc"]], axis=1)
            u["t"] = lm[LEVELS, :, :c] - u["a"] * lm[LEVELS - 1, :, :c]

    def level(lv):
        def run():
            for u in units:
                u["tl"] = _dot(u["t"], u["a"] * u["vw"]["lmask"][lv, :, :c])
            for u in units:
                u["t"] = u["t"] - _dot(u["tl"], u["t"])
        return run

    def solve():
        for u in units:
            u["r"] = _dot(u["t"], u["rhs"])
        for u in units:
            u["lhs"] = _bf(jnp.concatenate([u["r"][:, A_DV:], u["q"] * u["egc"]], axis=0))
            u["kdec"] = _bf(u["k"] * jnp.exp(u["gtot"] - u["gcc"]))
            u["gl"] = jnp.exp(u["gtot"])

    def recur(pos, is_last):
        def run():
            cur = [u for u in units if u["vw"]["pos"] == pos]
            for u in cur:
                key = (u["d"], u["h"])
                if key not in state:
                    state[key] = s_ref[u["d"], u["h"]]
                u["ws"] = _dot(u["lhs"], state[key])
            for u in cur:
                u["vn"] = u["r"][:, :A_DV] - u["ws"][:c]
            for u in cur:
                h, rows = u["h"], u["vw"]["rows"]
                u["vw"]["og"][0, rows, h * A_DV:(h + 1) * A_DV] = u["ws"][c:] + _dot(u["qk"], u["vn"])
            for u in cur:
                key = (u["d"], u["h"])
                state[key] = state[key] * u["gl"] + _dot_tn(u["kdec"], u["vn"])
                if is_last:
                    s_ref[u["d"], u["h"]] = state[key]
        return run

    return [prep] + [level(lv) for lv in range(LEVELS - 2, -1, -1)] + [solve], recur


def _gla_stages(views, bdmask, st_ref):
    c = CHUNK
    lane_q = lax.broadcasted_iota(jnp.int32, (1, BQ_W), 1) // B_DK
    lane_v = lax.broadcasted_iota(jnp.int32, (1, BV_W), 1) // B_DV
    state = {}

    def prep():
        for vw in views:
            vw["e_all"] = _dot_exact_lhs(vw["mst"], vw["gk"], terms=2)
            vw["sacc"] = jnp.zeros((c, B_HEADS * c), F32)

    def level(lv):
        def run():
            for vw in views:
                if lv < LEVELS:
                    wgt = jnp.exp(-jnp.abs(vw["e_all"][(lv + 1) * c:(lv + 2) * c]))
                    ql, kl = vw["bq"] * wgt, vw["bk"] * wgt
                else:
                    ql, kl = vw["bq"], vw["bk"]
                kst = jnp.concatenate([jnp.where(lane_q == h, kl, 0.0) for h in range(B_HEADS)], axis=0)
                vw["sacc"] = vw["sacc"] + _dot_nt(ql, kst) * vw["lmask"][lv]
        return run

    def intra():
        for vw in views:
            last = c - 1 if vw["d"] == 0 else 0
            bc = vw["e_all"][:c]
            bv = vw["bv"]
            vbd = jnp.concatenate([jnp.where(lane_v == h, bv, 0.0) for h in range(B_HEADS)], axis=0)
            bct = bc[last:last + 1]
            vw["oi"] = _dot(vw["sacc"], vbd)
            vw["qd"] = _bf(vw["bq"] * jnp.exp(bc))
            vw["kd"] = _bf(vw["bk"] * jnp.exp(bct - bc))
            vw["gl"] = jnp.exp(bct)

    def recur(pos, is_last):
        def run():
            for vw in views:
                if vw["pos"] != pos:
                    continue
                d = vw["d"]
                if d not in state:
                    state[d] = st_ref[d]
                vw["ol"][0, vw["rows"], :] = vw["oi"] + _dot_nt(vw["qd"], state[d])
                state[d] = state[d] * vw["gl"] + _dot_tn(vw["bv"], vw["kd"]) * bdmask
                if is_last:
                    st_ref[d] = state[d]
        return run

    return [prep] + [level(lv) for lv in range(LEVELS + 1)] + [intra], recur


def _scan_kernel(*refs):
    n_in = 9
    n_c = 5
    og_f, og_b, ol_f, ol_b = refs[2 * n_in + 2 * n_c + 1:2 * n_in + 2 * n_c + 5]
    s_ref, st_ref = refs[2 * n_in + 2 * n_c + 5:]
    bdmask_ref = refs[2 * n_in + 2 * n_c]
    c = CHUNK

    @pl.when(pl.program_id(1) == 0)
    def _():
        s_ref[...] = jnp.zeros_like(s_ref)
        st_ref[...] = jnp.zeros_like(st_ref)

    views = []
    for d, (og, ol) in enumerate(((og_f, ol_f), (og_b, ol_b))):
        aq, ak, av, sm, smT, bq, bk, bv, gk = refs[d * n_in:(d + 1) * n_in]
        tri, mst, lmask, incl, strict = refs[2 * n_in + d * n_c:2 * n_in + (d + 1) * n_c]
        order = range(STEP_CHUNKS) if d == 0 else range(STEP_CHUNKS - 1, -1, -1)
        for pos, j in enumerate(order):
            rows = pl.ds(j * c, c)
            views.append(dict(d=d, pos=pos, rows=rows, aq=aq[0, rows, :], ak=ak[0, rows, :], av=av[0, rows, :],
                              sm=sm[0, rows, :], smT=smT[j], bq=bq[0, rows, :], bk=bk[0, rows, :],
                              bv=bv[0, rows, :], gk=gk[0, rows, :], tri=tri[...], mst=mst[...], lmask=lmask,
                              incl=incl[...], strict=strict[...], og=og, ol=ol))
    gdn, gdn_recur = _gdn_stages(views, s_ref)
    gla, gla_recur = _gla_stages(views, bdmask_ref[...], st_ref)
    for i in range(max(len(gdn), len(gla))):
        if i < len(gdn):
            gdn[i]()
        if i < len(gla):
            gla[i]()
    for pos in range(STEP_CHUNKS):
        gdn_recur(pos, pos == STEP_CHUNKS - 1)()
        gla_recur(pos, pos == STEP_CHUNKS - 1)()


def _scans(aq, ak, av, sm, smT, bq, bk, bv, gk, ctx_len):
    bsz, ttot, _ = aq.shape
    c = CHUNK
    tb = STEP_CHUNKS * c
    nst = ttot // tb
    ncs = ctx_len // tb
    consts, bdmask = _scan_consts()

    def fidx(b, s):
        return s

    def bidx(b, s):
        return jnp.where(s < ncs, ncs - 1 - s, nst - 1 - s + ncs)

    def tok(w, idx, lane_blk=0):
        return pl.BlockSpec((1, tb, w), lambda b, s: (b, idx(b, s), lane_blk))

    def dir_specs(idx, d):
        return [tok(A_W, idx), tok(A_W, idx), tok(A_W, idx), tok(SMALL_W, idx),
                pl.BlockSpec((None, STEP_CHUNKS, 16, c), lambda b, s: (b, idx(b, s), 0, 0)),
                tok(BQ_W, idx), tok(BQ_W, idx), tok(BV_W, idx), tok(BQ_W, idx, d)]

    const = lambda a: pl.BlockSpec(a.shape, lambda b, s: tuple(0 for _ in a.shape))
    cargs = []
    for d in range(NDIR):
        cd = consts[d]
        cargs += [cd["tri"], cd["mst"], cd["lmask"], cd["incl"], cd["strict"]]
    cargs.append(bdmask)
    ins = [aq, ak, av, sm, smT, bq, bk, bv, gk]
    out_w = [A_HEADS * A_DV, A_HEADS * A_DV, BV_W, BV_W]
    return pl.pallas_call(
        _scan_kernel,
        grid=(bsz, nst),
        in_specs=dir_specs(fidx, 0) + dir_specs(bidx, 1) + [const(a) for a in cargs],
        out_specs=[tok(out_w[0], fidx), tok(out_w[1], bidx), tok(out_w[2], fidx), tok(out_w[3], bidx)],
        out_shape=[jax.ShapeDtypeStruct((bsz, ttot, w), F32) for w in out_w],
        scratch_shapes=[pltpu.VMEM((NDIR, A_HEADS, A_DK, A_DV), F32),
                        pltpu.VMEM((NDIR, BV_W, BQ_W), F32)],
        compiler_params=pltpu.CompilerParams(dimension_semantics=("arbitrary", "arbitrary"),
                                             vmem_limit_bytes=VMEM_LIMIT),
        name="scan",
    )(*ins, *ins, *cargs)


def _head_norm(o, w, z, n_heads, dv):
    parts = []
    for h in range(n_heads):
        oh = o[:, h * dv:(h + 1) * dv]
        parts.append(oh * lax.rsqrt(jnp.mean(oh * oh, axis=-1, keepdims=True) + EPS))
    return jnp.concatenate(parts, axis=-1) * w * _silu(z)


def _rms(x, w):
    return x * lax.rsqrt(jnp.mean(x * x, axis=-1, keepdims=True) + EPS) * w


def _post_kernel(x_ref, ogf_ref, ogb_ref, olf_ref, olb_ref, az_ref, bg_ref,
                 g1_ref, sh2_ref, sc2_ref, g2_ref, gn_ref, ln_ref, wo_ref, fn_ref,
                 wg_ref, wu_ref, wd_ref, fin_ref, o_ref):
    gdn = _head_norm(ogf_ref[0] + ogb_ref[0], gn_ref[...], az_ref[0], A_HEADS, A_DV)
    gla = _head_norm(olf_ref[0] + olb_ref[0], ln_ref[...], bg_ref[0], B_HEADS, B_DV)
    mix = jnp.concatenate([gdn, gla], axis=-1)
    x1 = x_ref[0] + g1_ref[...] * _dot(mix, wo_ref[...])
    h2 = _rms(x1, fn_ref[...]) * (1.0 + sc2_ref[...]) + sh2_ref[...]
    h2b = _bf(h2)
    gate = jnp.dot(h2b, wg_ref[...], preferred_element_type=F32)
    up = jnp.dot(h2b, wu_ref[...], preferred_element_type=F32)
    y = _dot(_silu(gate) * up, wd_ref[...])
    x2 = x1 + g2_ref[...] * y
    o_ref[0] = _rms(x2, fin_ref[...])


def _post(x, ogf, ogb, olf, olb, az, bg, mod4, gn, ln, wo, fn, wg, wu, wd, fin, ctx_len):
    bsz, seq, d = x.shape
    tm = TOK_TILE
    off = ctx_len // tm
    dff = wg.shape[1]

    def tok(w, o):
        return pl.BlockSpec((1, tm, w), lambda b, j: (b, j + o, 0))

    def modspec(k):
        return pl.BlockSpec((None, None, 1, d), lambda b, j: (b, k, 0, 0))

    def const(shape, single=True):
        idx = lambda b, j: tuple(0 for _ in shape)
        if single:
            return pl.BlockSpec(shape, idx, pipeline_mode=pl.Buffered(1))
        return pl.BlockSpec(shape, idx)

    return pl.pallas_call(
        _post_kernel,
        grid=(bsz, seq // tm),
        in_specs=[tok(d, 0), tok(A_W, off), tok(A_W, off), tok(BV_W, off), tok(BV_W, off),
                  tok(A_W, off), tok(BV_W, off),
                  modspec(2), modspec(3), modspec(4), modspec(5),
                  const((1, A_W), False), const((1, BV_W), False), const((d, d)), const((1, d), False),
                  const((d, dff)), const((d, dff)), const((dff, d)), const((1, d), False)],
        out_specs=tok(d, 0),
        out_shape=jax.ShapeDtypeStruct((bsz, seq, d), F32),
        compiler_params=pltpu.CompilerParams(dimension_semantics=("arbitrary", "arbitrary"),
                                             vmem_limit_bytes=VMEM_LIMIT),
        name="post",
    )(x, ogf, ogb, olf, olb, az, bg, mod4, mod4, mod4, mod4, gn, ln, wo, fn, wg, wu, wd, fin)


def kernel(x, c, ctx, c_ctx, w_mod, b_mod, attn_norm, w_in, conv_w, a_log, dt_bias, gdn_norm, gla_w2, gla_b,
           gla_norm, w_out, ffn_norm, w_gate, w_up, w_down, final_norm):
    bsz, seq, d = x.shape
    ctx_len = ctx.shape[1]
    assert w_mod.shape[0] == 1, "single-layer block"
    assert ctx_len == TOK_TILE and seq % TOK_TILE == 0 and TOK_TILE % GRID_W == 0 and TOK_TILE % (CHUNK * STEP_CHUNKS) == 0

    rows = -(-(bsz + 1) // 8) * 8
    cc = jnp.zeros((rows, d), F32).at[:bsz].set(c).at[rows - 1].set(c_ctx)
    mod = _modulation(cc, w_mod[0], b_mod[0][None, :])
    mod4 = mod.reshape(rows, 6, 1, d)

    w = w_in[0]
    o1 = 4 * A_W
    nb = NDIR * A_HEADS
    o2 = o1 + 2 * nb
    o3 = o2 + 2 * BQ_W + 2 * BV_W
    nlr = NDIR * GLA_RANK
    win = jnp.concatenate([w[:, :o1], w[:, o2:o3], w[:, o1:o2], w[:, o3:o3 + nlr],
                           jnp.zeros((d, SMALL_W - 2 * nb - nlr), F32)], axis=1).astype(BF16)
    gparams = jnp.zeros((2, SMALL_W), F32)
    gparams = gparams.at[0, nb:2 * nb].set(dt_bias[0].reshape(-1)).at[1, nb:2 * nb].set(a_log[0].reshape(-1))
    w2 = jnp.zeros((SMALL_W, NDIR * BQ_W), F32)
    for n in range(NDIR):
        w2 = w2.at[2 * nb + n * GLA_RANK:2 * nb + (n + 1) * GLA_RANK, n * BQ_W:(n + 1) * BQ_W].set(gla_w2[0, n])
    w2h = w2.astype(BF16)
    w2m = (w2 - w2h.astype(F32)).astype(BF16)

    aq, ak, av, az, bq, bk, bv, bg, gk, sm = _projection(
        x, ctx, mod4, attn_norm, win, conv_w[0], gparams, w2h, w2m, gla_b[0].reshape(1, -1))

    ttot = ctx_len + seq
    smT = sm[:, :, :16].reshape(bsz, ttot // CHUNK, CHUNK, 16).transpose(0, 1, 3, 2)
    ogf, ogb, olf, olb = _scans(aq, ak, av, sm, smT, bq, bk, bv, gk, ctx_len)

    tile_w = lambda v, n: jnp.tile(v.reshape(1, -1), (1, n))
    return _post(x, ogf, ogb, olf, olb, az, bg, mod4,
                 tile_w(gdn_norm[0], A_HEADS), tile_w(gla_norm[0], B_HEADS),
                 w_out[0].astype(BF16), ffn_norm, w_gate[0].astype(BF16), w_up[0].astype(BF16),
                 w_down[0].astype(BF16), final_norm.reshape(1, -1), ctx_len)
```

```python
import functools

import numpy as np
import jax
import jax.numpy as jnp
from jax import lax
from jax.experimental import pallas as pl
from jax.experimental.pallas import tpu as pltpu

F32 = jnp.float32
BF16 = jnp.bfloat16

GRID_W = 64
A_HEADS, A_DK, A_DV = 4, 128, 128
B_HEADS, B_DK, B_DV = 4, 64, 128
NDIR = 2
GLA_RANK = 16
GLA_NORMALIZER = 16.0
EPS = 1e-6

CHUNK = 64
STEP_CHUNKS = 4
TOK_TILE = 256
SMALL_W = 128
LEVELS = int(np.log2(CHUNK))

A_W = A_HEADS * A_DK
BQ_W = B_HEADS * B_DK
BV_W = B_HEADS * B_DV
P_COLS = 4 * A_W + 2 * BQ_W + 2 * BV_W + SMALL_W

VMEM_LIMIT = 56 * 1024 * 1024


def _bf(x):
    return x.astype(BF16)


def _dot(a, b):
    return jnp.dot(_bf(a), _bf(b), preferred_element_type=F32)


def _dot_nt(a, b):
    return lax.dot_general(_bf(a), _bf(b), (((1,), (1,)), ((), ())), preferred_element_type=F32)


def _dot_tn(a, b):
    return lax.dot_general(_bf(a), _bf(b), (((0,), (0,)), ((), ())), preferred_element_type=F32)


def _split3(x):
    hi = _bf(x)
    r1 = x - hi.astype(F32)
    mid = _bf(r1)
    lo = _bf(r1 - mid.astype(F32))
    return hi, mid, lo


def _dot_exact_lhs(m, x, terms=3):
    d = lambda p: jnp.dot(m, p, preferred_element_type=F32)
    parts = _split3(x)[:terms]
    acc = d(parts[0])
    for p in parts[1:]:
        acc = acc + d(p)
    return acc


def _dot_exact_rhs_nt(x, m):
    hi, mid, lo = _split3(x)
    d = lambda p: lax.dot_general(p, m, (((1,), (1,)), ((), ())), preferred_element_type=F32)
    return d(hi) + d(mid) + d(lo)


def _sigmoid(x):
    return 1.0 / (1.0 + jnp.exp(-x))


def _silu(x):
    return x * _sigmoid(x)


def _softplus(x):
    return jnp.maximum(x, 0.0) + jnp.log(1.0 + jnp.exp(-jnp.abs(x)))


def _mod_kernel(cc_ref, w_ref, b_ref, o_ref):
    o_ref[...] = _dot(_silu(cc_ref[...]), w_ref[...]) + b_ref[...]


def _modulation(cc, w_mod, b_mod):
    rows, d = cc.shape
    n = w_mod.shape[1]
    tn = 1536
    return pl.pallas_call(
        _mod_kernel,
        grid=(n // tn,),
        in_specs=[pl.BlockSpec((rows, d), lambda j: (0, 0)),
                  pl.BlockSpec((d, tn), lambda j: (0, j)),
                  pl.BlockSpec((1, tn), lambda j: (0, j))],
        out_specs=pl.BlockSpec((rows, tn), lambda j: (0, j)),
        out_shape=jax.ShapeDtypeStruct((rows, n), F32),
        compiler_params=pltpu.CompilerParams(dimension_semantics=("arbitrary",),
                                             vmem_limit_bytes=VMEM_LIMIT),
        name="mod",
    )(cc, w_mod, b_mod)


def _proj_kernel(n_ctx_tiles, ctx_len,
                 x_ref, ctx_ref, sh_ref, sc_ref, nw_ref, win_ref, cw_ref, gp_ref, w2h_ref, w2m_ref, gb_ref,
                 aq_ref, ak_ref, av_ref, az_ref, bq_ref, bk_ref, bv_ref, bg_ref, gk_ref, sm_ref):
    j = pl.program_id(1)
    is_ctx = j < n_ctx_tiles
    xin = jnp.where(is_ctx, ctx_ref[0], x_ref[0])
    ms = jnp.mean(xin * xin, axis=-1, keepdims=True)
    h = xin * lax.rsqrt(ms + EPS) * (nw_ref[...] * (1.0 + sc_ref[...])) + sh_ref[...]
    p = _dot(h, win_ref[...])

    tm = xin.shape[0]
    u = p[:, :3 * A_W]
    tok = lax.broadcasted_iota(jnp.int32, (tm, 1), 0)
    pos = jnp.where(is_ctx, tok, tok % GRID_W)
    last = jnp.where(is_ctx, ctx_len - 1, GRID_W - 1)
    prev = jnp.where(pos != 0, pltpu.roll(u, 1, 0), 0.0)
    nxt = jnp.where(pos != last, pltpu.roll(u, tm - 1, 0), 0.0)
    cw = cw_ref[...]
    qkv = _silu(cw[0:1] * prev + cw[1:2] * u + cw[2:3] * nxt)

    def l2n(t, scale):
        parts = []
        for hd in range(A_HEADS):
            th = t[:, hd * A_DK:(hd + 1) * A_DK]
            parts.append(th * (lax.rsqrt(jnp.sum(th * th, axis=-1, keepdims=True) + EPS) * scale))
        return jnp.concatenate(parts, axis=-1)

    aq_ref[0] = l2n(qkv[:, :A_W], A_DK ** -0.5)
    ak_ref[0] = l2n(qkv[:, A_W:2 * A_W], 1.0)
    av_ref[0] = qkv[:, 2 * A_W:]
    az_ref[0] = p[:, 3 * A_W:4 * A_W]
    o = 4 * A_W
    bq_ref[0] = p[:, o:o + BQ_W] * (B_DK ** -0.5)
    bk_ref[0] = p[:, o + BQ_W:o + 2 * BQ_W]
    bv_ref[0] = p[:, o + 2 * BQ_W:o + 2 * BQ_W + BV_W]
    bg_ref[0] = p[:, o + 2 * BQ_W + BV_W:o + 2 * BQ_W + 2 * BV_W]

    ps = p[:, P_COLS - SMALL_W:]
    gp = gp_ref[...]
    lane = lax.broadcasted_iota(jnp.int32, (1, SMALL_W), 1)
    beta = _sigmoid(ps)
    g = -jnp.exp(gp[1:2]) * _softplus(ps + gp[0:1])
    nb = NDIR * A_HEADS
    sm_ref[0] = jnp.where(lane < nb, beta, jnp.where(lane < 2 * nb, g, 0.0))

    ph = _bf(ps)
    pm = _bf(ps - ph.astype(F32))
    d = lambda a, b: jnp.dot(a, b, preferred_element_type=F32)
    pre = d(ph, w2h_ref[...]) + d(ph, w2m_ref[...]) + d(pm, w2h_ref[...]) + gb_ref[...]
    gk_ref[0] = -_softplus(-pre) * (1.0 / GLA_NORMALIZER)


def _projection(x, ctx, mod4, attn_norm, win, conv_w, gparams, w2h, w2m, gla_b):
    bsz, seq, d = x.shape
    ctx_len = ctx.shape[1]
    tm = TOK_TILE
    nct = ctx_len // tm
    nxt = seq // tm
    ttot = ctx_len + seq
    mod_rows = mod4.shape[0]

    def tok(w):
        return pl.BlockSpec((1, tm, w), lambda b, j: (b, j, 0))

    const = lambda shape: pl.BlockSpec(shape, lambda b, j: tuple(0 for _ in shape))
    widths = [A_W, A_W, A_W, A_W, BQ_W, BQ_W, BV_W, BV_W, NDIR * BQ_W, SMALL_W]
    return pl.pallas_call(
        functools.partial(_proj_kernel, nct, ctx_len),
        grid=(bsz, nct + nxt),
        in_specs=[
            pl.BlockSpec((1, tm, d), lambda b, j: (b, jnp.maximum(j - nct, 0), 0)),
            pl.BlockSpec((1, tm, d), lambda b, j: (b, jnp.minimum(j, nct - 1), 0)),
            pl.BlockSpec((None, None, 1, d), lambda b, j: (jnp.where(j < nct, mod_rows - 1, b), 0, 0, 0)),
            pl.BlockSpec((None, None, 1, d), lambda b, j: (jnp.where(j < nct, mod_rows - 1, b), 1, 0, 0)),
            const((1, d)), const((d, P_COLS)), const(conv_w.shape), const(gparams.shape),
            const(w2h.shape), const(w2m.shape), const(gla_b.shape),
        ],
        out_specs=[tok(w) for w in widths],
        out_shape=[jax.ShapeDtypeStruct((bsz, ttot, w), F32) for w in widths],
        compiler_params=pltpu.CompilerParams(dimension_semantics=("arbitrary", "arbitrary"),
                                             vmem_limit_bytes=VMEM_LIMIT),
        name="proj",
    )(x, ctx, mod4, mod4, attn_norm, win, conv_w, gparams, w2h, w2m, gla_b)


def _scan_consts():
    c = CHUNK
    i = np.arange(c)[:, None]
    t = np.arange(c)[None, :]
    out = []
    for d in range(NDIR):
        incl = (t <= i) if d == 0 else (t >= i)
        strict = (t < i) if d == 0 else (t > i)
        tri = incl.astype(np.float32)
        masks = []
        for lv in range(LEVELS):
            m = c >> (lv + 1)
            start = (np.arange(c) // (2 * m)) * (2 * m)
            later = (np.arange(c) % (2 * m) >= m) if d == 0 else (np.arange(c) % (2 * m) < m)
            same = (start[:, None] == start[None, :])
            masks.append((same & later[:, None] & ~later[None, :]).astype(np.float32))
        masks.append(np.eye(c, dtype=np.float32))
        tile4 = lambda mk: np.tile(mk, (1, A_HEADS))
        out.append(dict(
            tri=jnp.asarray(tri, BF16),
            tri4=jnp.asarray(np.tile(tri, (A_HEADS, 1)), BF16),
            lmask=jnp.asarray(np.stack([tile4(mk) for mk in masks]), F32),
            incl4=jnp.asarray(tile4(incl.astype(np.float32))),
            strict4=jnp.asarray(tile4(strict.astype(np.float32))),
        ))
    e = np.arange(BV_W)[:, None] // B_DV
    dd = np.arange(BQ_W)[None, :] // B_DK
    bdmask = jnp.asarray((e == dd).astype(np.float32))
    return out, bdmask


def _gdn_stages(views, s_ref):
    c = CHUNK
    nh = A_HEADS
    lane_p = lax.broadcasted_iota(jnp.int32, (1, nh * c), 1) // c
    lane_k = lax.broadcasted_iota(jnp.int32, (1, A_W), 1) // A_DK
    pm = [lane_p == h for h in range(nh)]
    km = [lane_k == h for h in range(nh)]
    state = {}

    def expand(cols, masks):
        out = cols[nh - 1]
        for h in range(nh - 2, -1, -1):
            out = jnp.where(masks[h], cols[h], out)
        return out

    def blockdiag(x):
        xb = _bf(x)
        return jnp.concatenate([jnp.where(pm[h], xb, 0) for h in range(nh)], axis=0)

    def heads(x, w):
        return [x[:, h * w:(h + 1) * w] for h in range(nh)]

    def prep():
        for vw in views:
            d = vw["d"]
            last = c - 1 if d == 0 else 0
            base = NDIR * nh + nh * d
            sm = vw["sm"]
            gc_all = _dot_exact_lhs(vw["tri"], sm)
            gcr_all = _dot_exact_rhs_nt(vw["smT"], vw["tri4"])
            gcc = [gc_all[:, base + h:base + h + 1] for h in range(nh)]
            gtot = [gc_all[last:last + 1, base + h:base + h + 1] for h in range(nh)]
            beta = [sm[:, nh * d + h:nh * d + h + 1] for h in range(nh)]
            egc = [jnp.exp(g) for g in gcc]
            vw["gl"] = [jnp.exp(g) for g in gtot]
            inc = vw["incl4"] > 0.5
            diff = expand(gcc, pm) - expand([gcr_all[base + h:base + h + 1, :] for h in range(nh)], pm)
            vw["decay"] = jnp.where(inc, jnp.exp(jnp.where(inc, diff, 0.0)), 0.0)
            ak, aq = vw["ak"], vw["aq"]
            kb = ak * expand(beta, km)
            egc_k = expand(egc, km)
            vw["kst"] = jnp.concatenate([jnp.where(km[h], _bf(ak), 0) for h in range(nh)], axis=0)
            vw["kbq"] = _bf(jnp.concatenate([kb, aq], axis=0))
            vb = heads(vw["av"] * expand(beta, km), A_DV)
            kbe = heads(kb * egc_k, A_DK)
            vw["rhs"] = _bf(jnp.concatenate([jnp.concatenate([vb[h], kbe[h]], axis=1) for h in range(nh)], axis=0))
            vw["qe"] = heads(aq * egc_k, A_DK)
            vw["kdec"] = heads(_bf(ak * expand([jnp.exp(gtot[h] - gcc[h]) for h in range(nh)], km)), A_DK)
        for vw in views:
            vw["m"] = _dot_nt(vw["kbq"], vw["kst"])
        for vw in views:
            lm = vw["lmask"]
            vw["a"] = vw["m"][:c] * vw["decay"] * vw["strict4"]
            vw["qk"] = _bf(vw["m"][c:] * vw["decay"])
            vw["t"] = lm[LEVELS] - vw["a"] * lm[LEVELS - 1]

    def level(lv):
        def run():
            for vw in views:
                vw["tl"] = _dot(vw["t"], blockdiag(vw["a"] * vw["lmask"][lv]))
            for vw in views:
                vw["t"] = vw["t"] - _dot(vw["tl"], blockdiag(vw["t"]))
        return run

    def solve():
        for vw in views:
            tb = _bf(vw["t"])
            vw["r"] = [_dot(jnp.where(pm[h], tb, 0), vw["rhs"]) for h in range(nh)]
        for vw in views:
            vw["lhs"] = [_bf(jnp.concatenate([vw["r"][h][:, A_DV:], vw["qe"][h]], axis=0)) for h in range(nh)]

    def recur(pos, is_last):
        def run():
            cur = [vw for vw in views if vw["pos"] == pos]
            for vw in cur:
                d = vw["d"]
                for h in range(nh):
                    if (d, h) not in state:
                        state[(d, h)] = s_ref[d, h]
                vw["ws"] = [_dot(vw["lhs"][h], state[(d, h)]) for h in range(nh)]
            for vw in cur:
                vw["vn"] = [vw["r"][h][:, :A_DV] - vw["ws"][h][:c] for h in range(nh)]
                vw["vn_all"] = _bf(jnp.concatenate(vw["vn"], axis=0))
            for vw in cur:
                for h in range(nh):
                    vw["og"][0, vw["rows"], h * A_DV:(h + 1) * A_DV] = (
                        vw["ws"][h][c:] + _dot(jnp.where(pm[h], vw["qk"], 0), vw["vn_all"]))
            for vw in cur:
                d = vw["d"]
                for h in range(nh):
                    state[(d, h)] = state[(d, h)] * vw["gl"][h] + _dot_tn(vw["kdec"][h], vw["vn"][h])
                    if is_last:
                        s_ref[d, h] = state[(d, h)]
        return run

    return [prep] + [level(lv) for lv in range(LEVELS - 2, -1, -1)] + [solve], recur


def _gla_stages(views, bdmask, st_ref):
    c = CHUNK
    lane_q = lax.broadcasted_iota(jnp.int32, (1, BQ_W), 1) // B_DK
    lane_v = lax.broadcasted_iota(jnp.int32, (1, BV_W), 1) // B_DV
    row = lax.broadcasted_iota(jnp.int32, (c, 1), 0)
    state = {}

    def level_abs(vw, lv):
        m = c >> (lv + 1)
        fwd = vw["d"] == 0
        bc, na = vw["bc"], vw["na"]
        if m >= 4:
            refs = [s + (m - 1 if fwd else m) for s in range(0, c, 2 * m)]
            ref_rows = jnp.concatenate([jnp.broadcast_to(bc[r:r + 1, :], (2 * m, BQ_W)) for r in refs], axis=0)
            return jnp.abs(bc - ref_rows)
        r = row % (2 * m)
        up = pltpu.roll(na, c - 1, 0)
        dn = pltpu.roll(na, 1, 0)
        if m == 2:
            if fwd:
                return jnp.where(r == 0, up, jnp.where(r == 1, 0.0, jnp.where(r == 2, na, na + dn)))
            return jnp.where(r == 0, na + up, jnp.where(r == 1, na, jnp.where(r == 2, 0.0, dn)))
        return jnp.where(r == (1 if fwd else 0), na, 0.0)

    def prep():
        for vw in views:
            vw["bc"] = _dot_exact_lhs(vw["tri"], vw["gk"])
            vw["na"] = -vw["gk"]
            vw["sacc"] = jnp.zeros((c, B_HEADS * c), F32)

    def level(lv):
        def run():
            for vw in views:
                if lv < LEVELS:
                    wgt = jnp.exp(-level_abs(vw, lv))
                    ql, kl = vw["bq"] * wgt, _bf(vw["bk"] * wgt)
                else:
                    ql, kl = vw["bq"], _bf(vw["bk"])
                kst = jnp.concatenate([jnp.where(lane_q == h, kl, 0) for h in range(B_HEADS)], axis=0)
                vw["sacc"] = vw["sacc"] + _dot_nt(ql, kst) * vw["lmask"][lv]
        return run

    def intra():
        for vw in views:
            last = c - 1 if vw["d"] == 0 else 0
            bc = vw["bc"]
            bvb = _bf(vw["bv"])
            vbd = jnp.concatenate([jnp.where(lane_v == h, bvb, 0) for h in range(B_HEADS)], axis=0)
            bct = bc[last:last + 1]
            vw["oi"] = _dot(vw["sacc"], vbd)
            vw["qd"] = _bf(vw["bq"] * jnp.exp(bc))
            vw["kd"] = _bf(vw["bk"] * jnp.exp(bct - bc))
            vw["lgl"] = jnp.exp(bct)
            vw["bvb"] = bvb

    def recur(pos, is_last):
        def run():
            for vw in views:
                if vw["pos"] != pos:
                    continue
                d = vw["d"]
                if d not in state:
                    state[d] = st_ref[d]
                vw["ol"][0, vw["rows"], :] = vw["oi"] + _dot_nt(vw["qd"], state[d])
                state[d] = state[d] * vw["lgl"] + _dot_tn(vw["bvb"], vw["kd"]) * bdmask
                if is_last:
                    st_ref[d] = state[d]
        return run

    return [prep] + [level(lv) for lv in range(LEVELS + 1)] + [intra], recur


def _scan_kernel(*refs):
    n_in = 9
    n_c = 5
    og_f, og_b, ol_f, ol_b = refs[2 * n_in + 2 * n_c + 1:2 * n_in + 2 * n_c + 5]
    s_ref, st_ref = refs[2 * n_in + 2 * n_c + 5:]
    bdmask_ref = refs[2 * n_in + 2 * n_c]
    c = CHUNK

    @pl.when(pl.program_id(1) == 0)
    def _():
        s_ref[...] = jnp.zeros_like(s_ref)
        st_ref[...] = jnp.zeros_like(st_ref)

    views = []
    for d, (og, ol) in enumerate(((og_f, ol_f), (og_b, ol_b))):
        aq, ak, av, sm, smT, bq, bk, bv, gk = refs[d * n_in:(d + 1) * n_in]
        tri, tri4, lmask, incl4, strict4 = refs[2 * n_in + d * n_c:2 * n_in + (d + 1) * n_c]
        order = range(STEP_CHUNKS) if d == 0 else range(STEP_CHUNKS - 1, -1, -1)
        for pos, j in enumerate(order):
            rows = pl.ds(j * c, c)
            views.append(dict(d=d, pos=pos, rows=rows, aq=aq[0, rows, :], ak=ak[0, rows, :], av=av[0, rows, :],
                              sm=sm[0, rows, :], smT=smT[j], bq=bq[0, rows, :], bk=bk[0, rows, :],
                              bv=bv[0, rows, :], gk=gk[0, rows, :], tri=tri[...], tri4=tri4[...], lmask=lmask,
                              incl4=incl4[...], strict4=strict4[...], og=og, ol=ol))
    gdn, gdn_recur = _gdn_stages(views, s_ref)
    gla, gla_recur = _gla_stages(views, bdmask_ref[...], st_ref)
    for i in range(max(len(gdn), len(gla))):
        if i < len(gdn):
            gdn[i]()
        if i < len(gla):
            gla[i]()
    for pos in range(STEP_CHUNKS):
        gdn_recur(pos, pos == STEP_CHUNKS - 1)()
        gla_recur(pos, pos == STEP_CHUNKS - 1)()


def _scans(aq, ak, av, sm, smT, bq, bk, bv, gk, ctx_len):
    bsz, ttot, _ = aq.shape
    c = CHUNK
    tb = STEP_CHUNKS * c
    nst = ttot // tb
    ncs = ctx_len // tb
    consts, bdmask = _scan_consts()

    def fidx(b, s):
        return s

    def bidx(b, s):
        return jnp.where(s < ncs, ncs - 1 - s, nst - 1 - s + ncs)

    def tok(w, idx, lane_blk=0):
        return pl.BlockSpec((1, tb, w), lambda b, s: (b, idx(b, s), lane_blk))

    def dir_specs(idx, d):
        return [tok(A_W, idx), tok(A_W, idx), tok(A_W, idx), tok(SMALL_W, idx),
                pl.BlockSpec((None, STEP_CHUNKS, 16, c), lambda b, s: (b, idx(b, s), 0, 0)),
                tok(BQ_W, idx), tok(BQ_W, idx), tok(BV_W, idx), tok(BQ_W, idx, d)]

    const = lambda a: pl.BlockSpec(a.shape, lambda b, s: tuple(0 for _ in a.shape))
    cargs = []
    for d in range(NDIR):
        cd = consts[d]
        cargs += [cd["tri"], cd["tri4"], cd["lmask"], cd["incl4"], cd["strict4"]]
    cargs.append(bdmask)
    ins = [aq, ak, av, sm, smT, bq, bk, bv, gk]
    out_w = [A_HEADS * A_DV, A_HEADS * A_DV, BV_W, BV_W]
    return pl.pallas_call(
        _scan_kernel,
        grid=(bsz, nst),
        in_specs=dir_specs(fidx, 0) + dir_specs(bidx, 1) + [const(a) for a in cargs],
        out_specs=[tok(out_w[0], fidx), tok(out_w[1], bidx), tok(out_w[2], fidx), tok(out_w[3], bidx)],
        out_shape=[jax.ShapeDtypeStruct((bsz, ttot, w), F32) for w in out_w],
        scratch_shapes=[pltpu.VMEM((NDIR, A_HEADS, A_DK, A_DV), F32),
                        pltpu.VMEM((NDIR, BV_W, BQ_W), F32)],
        compiler_params=pltpu.CompilerParams(dimension_semantics=("arbitrary", "arbitrary"),
                                             vmem_limit_bytes=VMEM_LIMIT),
        name="scan",
    )(*ins, *ins, *cargs)


def _head_norm(o, w, z, n_heads, dv):
    parts = []
    for h in range(n_heads):
        oh = o[:, h * dv:(h + 1) * dv]
        parts.append(oh * lax.rsqrt(jnp.mean(oh * oh, axis=-1, keepdims=True) + EPS))
    return jnp.concatenate(parts, axis=-1) * w * _silu(z)


def _rms(x, w):
    return x * lax.rsqrt(jnp.mean(x * x, axis=-1, keepdims=True) + EPS) * w


def _post_kernel(x_ref, ogf_ref, ogb_ref, olf_ref, olb_ref, az_ref, bg_ref,
                 g1_ref, sh2_ref, sc2_ref, g2_ref, gn_ref, ln_ref, wo_ref, fn_ref,
                 wg_ref, wu_ref, wd_ref, fin_ref, o_ref):
    gdn = _head_norm(ogf_ref[0] + ogb_ref[0], gn_ref[...], az_ref[0], A_HEADS, A_DV)
    gla = _head_norm(olf_ref[0] + olb_ref[0], ln_ref[...], bg_ref[0], B_HEADS, B_DV)
    mix = jnp.concatenate([gdn, gla], axis=-1)
    x1 = x_ref[0] + g1_ref[...] * _dot(mix, wo_ref[...])
    h2 = _rms(x1, fn_ref[...]) * (1.0 + sc2_ref[...]) + sh2_ref[...]
    h2b = _bf(h2)
    gate = jnp.dot(h2b, wg_ref[...], preferred_element_type=F32)
    up = jnp.dot(h2b, wu_ref[...], preferred_element_type=F32)
    y = _dot(_silu(gate) * up, wd_ref[...])
    x2 = x1 + g2_ref[...] * y
    o_ref[0] = _rms(x2, fin_ref[...])


def _post(x, ogf, ogb, olf, olb, az, bg, mod4, gn, ln, wo, fn, wg, wu, wd, fin, ctx_len):
    bsz, seq, d = x.shape
    tm = TOK_TILE
    off = ctx_len // tm
    dff = wg.shape[1]

    def tok(w, o):
        return pl.BlockSpec((1, tm, w), lambda b, j: (b, j + o, 0))

    def modspec(k):
        return pl.BlockSpec((None, None, 1, d), lambda b, j: (b, k, 0, 0))

    def const(shape, single=True):
        idx = lambda b, j: tuple(0 for _ in shape)
        if single:
            return pl.BlockSpec(shape, idx, pipeline_mode=pl.Buffered(1))
        return pl.BlockSpec(shape, idx)

    return pl.pallas_call(
        _post_kernel,
        grid=(bsz, seq // tm),
        in_specs=[tok(d, 0), tok(A_W, off), tok(A_W, off), tok(BV_W, off), tok(BV_W, off),
                  tok(A_W, off), tok(BV_W, off),
                  modspec(2), modspec(3), modspec(4), modspec(5),
                  const((1, A_W), False), const((1, BV_W), False), const((d, d)), const((1, d), False),
                  const((d, dff)), const((d, dff)), const((dff, d)), const((1, d), False)],
        out_specs=tok(d, 0),
        out_shape=jax.ShapeDtypeStruct((bsz, seq, d), F32),
        compiler_params=pltpu.CompilerParams(dimension_semantics=("arbitrary", "arbitrary"),
                                             vmem_limit_bytes=VMEM_LIMIT),
        name="post",
    )(x, ogf, ogb, olf, olb, az, bg, mod4, mod4, mod4, mod4, gn, ln, wo, fn, wg, wu, wd, fin)


def kernel(x, c, ctx, c_ctx, w_mod, b_mod, attn_norm, w_in, conv_w, a_log, dt_bias, gdn_norm, gla_w2, gla_b,
           gla_norm, w_out, ffn_norm, w_gate, w_up, w_down, final_norm):
    bsz, seq, d = x.shape
    ctx_len = ctx.shape[1]
    assert w_mod.shape[0] == 1, "single-layer block"
    assert ctx_len == TOK_TILE and seq % TOK_TILE == 0 and TOK_TILE % GRID_W == 0 and TOK_TILE % (CHUNK * STEP_CHUNKS) == 0

    rows = -(-(bsz + 1) // 8) * 8
    cc = jnp.zeros((rows, d), F32).at[:bsz].set(c).at[rows - 1].set(c_ctx)
    mod = _modulation(cc, w_mod[0], b_mod[0][None, :])
    mod4 = mod.reshape(rows, 6, 1, d)

    w = w_in[0]
    o1 = 4 * A_W
    nb = NDIR * A_HEADS
    o2 = o1 + 2 * nb
    o3 = o2 + 2 * BQ_W + 2 * BV_W
    nlr = NDIR * GLA_RANK
    win = jnp.concatenate([w[:, :o1], w[:, o2:o3], w[:, o1:o2], w[:, o3:o3 + nlr],
                           jnp.zeros((d, SMALL_W - 2 * nb - nlr), F32)], axis=1).astype(BF16)
    gparams = jnp.zeros((2, SMALL_W), F32)
    gparams = gparams.at[0, nb:2 * nb].set(dt_bias[0].reshape(-1)).at[1, nb:2 * nb].set(a_log[0].reshape(-1))
    w2 = jnp.zeros((SMALL_W, NDIR * BQ_W), F32)
    for n in range(NDIR):
        w2 = w2.at[2 * nb + n * GLA_RANK:2 * nb + (n + 1) * GLA_RANK, n * BQ_W:(n + 1) * BQ_W].set(gla_w2[0, n])
    w2h = w2.astype(BF16)
    w2m = (w2 - w2h.astype(F32)).astype(BF16)

    aq, ak, av, az, bq, bk, bv, bg, gk, sm = _projection(
        x, ctx, mod4, attn_norm, win, conv_w[0], gparams, w2h, w2m, gla_b[0].reshape(1, -1))

    ttot = ctx_len + seq
    smT = sm[:, :, :16].reshape(bsz, ttot // CHUNK, CHUNK, 16).transpose(0, 1, 3, 2)
    ogf, ogb, olf, olb = _scans(aq, ak, av, sm, smT, bq, bk, bv, gk, ctx_len)

    tile_w = lambda v, n: jnp.tile(v.reshape(1, -1), (1, n))
    return _post(x, ogf, ogb, olf, olb, az, bg, mod4,
                 tile_w(gdn_norm[0], A_HEADS), tile_w(gla_norm[0], B_HEADS),
                 w_out[0].astype(BF16), ffn_norm, w_gate[0].astype(BF16), w_up[0].astype(BF16),
                 w_down[0].astype(BF16), final_norm.reshape(1, -1), ctx_len)
```

```python
import functools

import numpy as np
import jax
import jax.numpy as jnp
from jax import lax
from jax.experimental import pallas as pl
from jax.experimental.pallas import tpu as pltpu

F32 = jnp.float32
BF16 = jnp.bfloat16

GRID_W = 64
A_HEADS, A_DK, A_DV = 4, 128, 128
B_HEADS, B_DK, B_DV = 4, 64, 128
NDIR = 2
GLA_RANK = 16
GLA_NORMALIZER = 16.0
EPS = 1e-6

CHUNK = 64
STEP_CHUNKS = 4
TOK_TILE = 256
SMALL_W = 128
LEVELS = int(np.log2(CHUNK))

A_W = A_HEADS * A_DK
BQ_W = B_HEADS * B_DK
BV_W = B_HEADS * B_DV
P_COLS = 4 * A_W + 2 * BQ_W + 2 * BV_W + SMALL_W

VMEM_LIMIT = 56 * 1024 * 1024


def _bf(x):
    return x.astype(BF16)


def _dot(a, b):
    return jnp.dot(_bf(a), _bf(b), preferred_element_type=F32)


def _dot_nt(a, b):
    return lax.dot_general(_bf(a), _bf(b), (((1,), (1,)), ((), ())), preferred_element_type=F32)


def _dot_tn(a, b):
    return lax.dot_general(_bf(a), _bf(b), (((0,), (0,)), ((), ())), preferred_element_type=F32)


def _split3(x):
    hi = _bf(x)
    r1 = x - hi.astype(F32)
    mid = _bf(r1)
    lo = _bf(r1 - mid.astype(F32))
    return hi, mid, lo


def _dot_exact_lhs(m, x, terms=3):
    d = lambda p: jnp.dot(m, p, preferred_element_type=F32)
    parts = _split3(x)[:terms]
    acc = d(parts[0])
    for p in parts[1:]:
        acc = acc + d(p)
    return acc


def _dot_exact_rhs_nt(x, m):
    hi, mid, lo = _split3(x)
    d = lambda p: lax.dot_general(p, m, (((1,), (1,)), ((), ())), preferred_element_type=F32)
    return d(hi) + d(mid) + d(lo)


def _sigmoid(x):
    return 1.0 / (1.0 + jnp.exp(-x))


def _silu(x):
    return x * _sigmoid(x)


def _softplus(x):
    return jnp.maximum(x, 0.0) + jnp.log(1.0 + jnp.exp(-jnp.abs(x)))


def _mod_kernel(cc_ref, w_ref, b_ref, o_ref):
    o_ref[...] = _dot(_silu(cc_ref[...]), w_ref[...]) + b_ref[...]


def _modulation(cc, w_mod, b_mod):
    rows, d = cc.shape
    n = w_mod.shape[1]
    tn = 1536
    return pl.pallas_call(
        _mod_kernel,
        grid=(n // tn,),
        in_specs=[pl.BlockSpec((rows, d), lambda j: (0, 0)),
                  pl.BlockSpec((d, tn), lambda j: (0, j)),
                  pl.BlockSpec((1, tn), lambda j: (0, j))],
        out_specs=pl.BlockSpec((rows, tn), lambda j: (0, j)),
        out_shape=jax.ShapeDtypeStruct((rows, n), F32),
        compiler_params=pltpu.CompilerParams(dimension_semantics=("arbitrary",),
                                             vmem_limit_bytes=VMEM_LIMIT),
        name="mod",
    )(cc, w_mod, b_mod)


def _proj_kernel(n_ctx_tiles, ctx_len,
                 x_ref, ctx_ref, sh_ref, sc_ref, nw_ref, win_ref, cw_ref, gp_ref, w2h_ref, w2m_ref, gb_ref,
                 aq_ref, ak_ref, av_ref, az_ref, bq_ref, bk_ref, bv_ref, bg_ref, gk_ref, sm_ref):
    j = pl.program_id(1)
    is_ctx = j < n_ctx_tiles
    xin = jnp.where(is_ctx, ctx_ref[0], x_ref[0])
    ms = jnp.mean(xin * xin, axis=-1, keepdims=True)
    h = xin * lax.rsqrt(ms + EPS) * (nw_ref[...] * (1.0 + sc_ref[...])) + sh_ref[...]
    p = _dot(h, win_ref[...])

    tm = xin.shape[0]
    u = p[:, :3 * A_W]
    tok = lax.broadcasted_iota(jnp.int32, (tm, 1), 0)
    pos = jnp.where(is_ctx, tok, tok % GRID_W)
    last = jnp.where(is_ctx, ctx_len - 1, GRID_W - 1)
    prev = jnp.where(pos != 0, pltpu.roll(u, 1, 0), 0.0)
    nxt = jnp.where(pos != last, pltpu.roll(u, tm - 1, 0), 0.0)
    cw = cw_ref[...]
    qkv = _silu(cw[0:1] * prev + cw[1:2] * u + cw[2:3] * nxt)

    def l2n(t, scale):
        parts = []
        for hd in range(A_HEADS):
            th = t[:, hd * A_DK:(hd + 1) * A_DK]
            parts.append(th * (lax.rsqrt(jnp.sum(th * th, axis=-1, keepdims=True) + EPS) * scale))
        return jnp.concatenate(parts, axis=-1)

    aq_ref[0] = l2n(qkv[:, :A_W], A_DK ** -0.5)
    ak_ref[0] = l2n(qkv[:, A_W:2 * A_W], 1.0)
    av_ref[0] = qkv[:, 2 * A_W:]
    az_ref[0] = p[:, 3 * A_W:4 * A_W]
    o = 4 * A_W
    bq_ref[0] = p[:, o:o + BQ_W] * (B_DK ** -0.5)
    bk_ref[0] = p[:, o + BQ_W:o + 2 * BQ_W]
    bv_ref[0] = p[:, o + 2 * BQ_W:o + 2 * BQ_W + BV_W]
    bg_ref[0] = p[:, o + 2 * BQ_W + BV_W:o + 2 * BQ_W + 2 * BV_W]

    ps = p[:, P_COLS - SMALL_W:]
    gp = gp_ref[...]
    lane = lax.broadcasted_iota(jnp.int32, (1, SMALL_W), 1)
    beta = _sigmoid(ps)
    g = -jnp.exp(gp[1:2]) * _softplus(ps + gp[0:1])
    nb = NDIR * A_HEADS
    sm_ref[0] = jnp.where(lane < nb, beta, jnp.where(lane < 2 * nb, g, 0.0))

    ph = _bf(ps)
    pm = _bf(ps - ph.astype(F32))
    d = lambda a, b: jnp.dot(a, b, preferred_element_type=F32)
    pre = d(ph, w2h_ref[...]) + d(ph, w2m_ref[...]) + d(pm, w2h_ref[...]) + gb_ref[...]
    gk_ref[0] = -_softplus(-pre) * (1.0 / GLA_NORMALIZER)


def _projection(x, ctx, mod4, attn_norm, win, conv_w, gparams, w2h, w2m, gla_b):
    bsz, seq, d = x.shape
    ctx_len = ctx.shape[1]
    tm = TOK_TILE
    nct = ctx_len // tm
    nxt = seq // tm
    ttot = ctx_len + seq
    mod_rows = mod4.shape[0]

    def tok(w):
        return pl.BlockSpec((1, tm, w), lambda b, j: (b, j, 0))

    const = lambda shape: pl.BlockSpec(shape, lambda b, j: tuple(0 for _ in shape))
    widths = [A_W, A_W, A_W, A_W, BQ_W, BQ_W, BV_W, BV_W, NDIR * BQ_W, SMALL_W]
    return pl.pallas_call(
        functools.partial(_proj_kernel, nct, ctx_len),
        grid=(bsz, nct + nxt),
        in_specs=[
            pl.BlockSpec((1, tm, d), lambda b, j: (b, jnp.maximum(j - nct, 0), 0)),
            pl.BlockSpec((1, tm, d), lambda b, j: (b, jnp.minimum(j, nct - 1), 0)),
            pl.BlockSpec((None, None, 1, d), lambda b, j: (jnp.where(j < nct, mod_rows - 1, b), 0, 0, 0)),
            pl.BlockSpec((None, None, 1, d), lambda b, j: (jnp.where(j < nct, mod_rows - 1, b), 1, 0, 0)),
            const((1, d)), const((d, P_COLS)), const(conv_w.shape), const(gparams.shape),
            const(w2h.shape), const(w2m.shape), const(gla_b.shape),
        ],
        out_specs=[tok(w) for w in widths],
        out_shape=[jax.ShapeDtypeStruct((bsz, ttot, w), F32) for w in widths],
        compiler_params=pltpu.CompilerParams(dimension_semantics=("arbitrary", "arbitrary"),
                                             vmem_limit_bytes=VMEM_LIMIT),
        name="proj",
    )(x, ctx, mod4, mod4, attn_norm, win, conv_w, gparams, w2h, w2m, gla_b)


def _scan_consts():
    c = CHUNK
    i = np.arange(c)[:, None]
    t = np.arange(c)[None, :]
    out = []
    for d in range(NDIR):
        incl = (t <= i) if d == 0 else (t >= i)
        strict = (t < i) if d == 0 else (t > i)
        tri = incl.astype(np.float32)
        masks = []
        for lv in range(LEVELS):
            m = c >> (lv + 1)
            start = (np.arange(c) // (2 * m)) * (2 * m)
            later = (np.arange(c) % (2 * m) >= m) if d == 0 else (np.arange(c) % (2 * m) < m)
            same = (start[:, None] == start[None, :])
            masks.append((same & later[:, None] & ~later[None, :]).astype(np.float32))
        masks.append(np.eye(c, dtype=np.float32))
        tile2 = lambda mk: np.tile(mk, (1, 2))
        out.append(dict(
            tri=jnp.asarray(tri, BF16),
            tri2=jnp.asarray(np.tile(tri, (2, 1)), BF16),
            lmask2=jnp.asarray(np.stack([tile2(mk) for mk in masks]), F32),
            lmaskv=jnp.asarray(np.stack([np.tile(mk, (B_HEADS, 1)) for mk in masks]), F32),
            incl2=jnp.asarray(tile2(incl.astype(np.float32))),
            strict2=jnp.asarray(tile2(strict.astype(np.float32))),
        ))
    return out


def _gdn_stages(views, s_ref):
    c = CHUNK
    nh = A_HEADS
    npair = nh // 2
    w2 = 2 * c
    first = lax.broadcasted_iota(jnp.int32, (1, w2), 1) < c
    first_k = lax.broadcasted_iota(jnp.int32, (1, 2 * A_DK), 1) < A_DK
    lane_k = lax.broadcasted_iota(jnp.int32, (1, A_W), 1) // A_DK
    km = [lane_k == h for h in range(nh)]
    state = {}

    def expand(cols):
        out = cols[nh - 1]
        for h in range(nh - 2, -1, -1):
            out = jnp.where(km[h], cols[h], out)
        return out

    def blockdiag(x):
        xb = _bf(x)
        return jnp.concatenate([jnp.where(first, xb, 0), jnp.where(first, 0, xb)], axis=0)

    def heads(x, w):
        return [x[:, h * w:(h + 1) * w] for h in range(nh)]

    def prep():
        for vw in views:
            d = vw["d"]
            last = c - 1 if d == 0 else 0
            base = NDIR * nh + nh * d
            sm = vw["sm"]
            gc_all = _dot_exact_lhs(vw["tri"], sm)
            gcr_all = _dot_exact_rhs_nt(vw["smT"], vw["tri2"])
            gcc = [gc_all[:, base + h:base + h + 1] for h in range(nh)]
            gtot = [gc_all[last:last + 1, base + h:base + h + 1] for h in range(nh)]
            beta = [sm[:, nh * d + h:nh * d + h + 1] for h in range(nh)]
            egc = [jnp.exp(g) for g in gcc]
            vw["gl"] = [jnp.exp(g) for g in gtot]
            inc = vw["incl2"] > 0.5
            ak, aq = vw["ak"], vw["aq"]
            beta_k = expand(beta)
            kb = ak * beta_k
            egc_k = expand(egc)
            kbq = _bf(jnp.concatenate([kb, aq], axis=0))
            akb = _bf(ak)
            vw["decay"], vw["kbq"], vw["kst"] = [], [], []
            for p in range(npair):
                h0, h1 = 2 * p, 2 * p + 1
                diff = (jnp.where(first, gcc[h0], gcc[h1])
                        - jnp.where(first, gcr_all[base + h0:base + h0 + 1, :], gcr_all[base + h1:base + h1 + 1, :]))
                vw["decay"].append(jnp.where(inc, jnp.exp(jnp.where(inc, diff, 0.0)), 0.0))
                kp = akb[:, 2 * A_DK * p:2 * A_DK * (p + 1)]
                vw["kst"].append(jnp.concatenate([jnp.where(first_k, kp, 0), jnp.where(first_k, 0, kp)], axis=0))
                vw["kbq"].append(kbq[:, 2 * A_DK * p:2 * A_DK * (p + 1)])
            vb = heads(vw["av"] * beta_k, A_DV)
            kbe = heads(kb * egc_k, A_DK)
            rhs = [jnp.concatenate([vb[h], kbe[h]], axis=1) for h in range(nh)]
            vw["rhs"] = [_bf(jnp.concatenate([rhs[2 * p], rhs[2 * p + 1]], axis=0)) for p in range(npair)]
            vw["qe"] = heads(aq * egc_k, A_DK)
            vw["kdec"] = heads(_bf(ak * expand([jnp.exp(gtot[h] - gcc[h]) for h in range(nh)])), A_DK)
        for vw in views:
            vw["m"] = [_dot_nt(vw["kbq"][p], vw["kst"][p]) for p in range(npair)]
        for vw in views:
            lm = vw["lmask2"]
            vw["a"] = [vw["m"][p][:c] * vw["decay"][p] * vw["strict2"] for p in range(npair)]
            vw["qk"] = [blockdiag(vw["m"][p][c:] * vw["decay"][p]) for p in range(npair)]
            vw["t"] = [lm[LEVELS] - vw["a"][p] * lm[LEVELS - 1] for p in range(npair)]

    def level(lv):
        def run():
            for vw in views:
                vw["tl"] = [_dot(vw["t"][p], blockdiag(vw["a"][p] * vw["lmask2"][lv])) for p in range(npair)]
            for vw in views:
                vw["t"] = [vw["t"][p] - _dot(vw["tl"][p], blockdiag(vw["t"][p])) for p in range(npair)]
        return run

    def solve():
        for vw in views:
            rp = [_dot(blockdiag(vw["t"][p]), vw["rhs"][p]) for p in range(npair)]
            vw["r"] = [rp[h // 2][(h % 2) * c:(h % 2 + 1) * c] for h in range(nh)]
        for vw in views:
            vw["lhs"] = [_bf(jnp.concatenate([vw["r"][h][:, A_DV:], vw["qe"][h]], axis=0)) for h in range(nh)]

    def recur(pos, is_last):
        def run():
            cur = [vw for vw in views if vw["pos"] == pos]
            for vw in cur:
                d = vw["d"]
                for h in range(nh):
                    if (d, h) not in state:
                        state[(d, h)] = s_ref[d, h]
                vw["ws"] = [_dot(vw["lhs"][h], state[(d, h)]) for h in range(nh)]
            for vw in cur:
                vw["vn"] = [vw["r"][h][:, :A_DV] - vw["ws"][h][:c] for h in range(nh)]
            for vw in cur:
                for p in range(npair):
                    vn2 = _bf(jnp.concatenate([vw["vn"][2 * p], vw["vn"][2 * p + 1]], axis=0))
                    oq = _dot(vw["qk"][p], vn2)
                    for e in range(2):
                        h = 2 * p + e
                        vw["og"][0, vw["rows"], h * A_DV:(h + 1) * A_DV] = vw["ws"][h][c:] + oq[e * c:(e + 1) * c]
            for vw in cur:
                d = vw["d"]
                for h in range(nh):
                    state[(d, h)] = state[(d, h)] * vw["gl"][h] + _dot_tn(vw["kdec"][h], vw["vn"][h])
                    if is_last:
                        s_ref[d, h] = state[(d, h)]
        return run

    return [prep] + [level(lv) for lv in range(LEVELS - 2, -1, -1)] + [solve], recur


def _gla_stages(views, st_ref):
    c = CHUNK
    nh = B_HEADS
    lane_q = lax.broadcasted_iota(jnp.int32, (1, BQ_W), 1) // B_DK
    row = lax.broadcasted_iota(jnp.int32, (c, 1), 0)
    state = {}

    def stack_heads(x):
        return jnp.concatenate([jnp.where(lane_q == h, x, 0) for h in range(nh)], axis=0)

    def level_abs(vw, lv):
        m = c >> (lv + 1)
        fwd = vw["d"] == 0
        bc, na = vw["bc"], vw["na"]
        if m >= 4:
            refs = [s + (m - 1 if fwd else m) for s in range(0, c, 2 * m)]
            ref_rows = jnp.concatenate([jnp.broadcast_to(bc[r:r + 1, :], (2 * m, BQ_W)) for r in refs], axis=0)
            return jnp.abs(bc - ref_rows)
        r = row % (2 * m)
        up = pltpu.roll(na, c - 1, 0)
        dn = pltpu.roll(na, 1, 0)
        if m == 2:
            if fwd:
                return jnp.where(r == 0, up, jnp.where(r == 1, 0.0, jnp.where(r == 2, na, na + dn)))
            return jnp.where(r == 0, na + up, jnp.where(r == 1, na, jnp.where(r == 2, 0.0, dn)))
        return jnp.where(r == (1 if fwd else 0), na, 0.0)

    def prep():
        for vw in views:
            vw["bc"] = _dot_exact_lhs(vw["tri"], vw["gk"])
            vw["na"] = -vw["gk"]
            vw["sacc"] = jnp.zeros((nh * c, c), F32)

    def level(lv):
        def run():
            for vw in views:
                if lv < LEVELS:
                    wgt = jnp.exp(-level_abs(vw, lv))
                    ql, kl = _bf(vw["bq"] * wgt), _bf(vw["bk"] * wgt)
                else:
                    ql, kl = _bf(vw["bq"]), _bf(vw["bk"])
                vw["sacc"] = vw["sacc"] + _dot_nt(stack_heads(ql), kl) * vw["lmaskv"][lv]
        return run

    def intra():
        for vw in views:
            last = c - 1 if vw["d"] == 0 else 0
            bc = vw["bc"]
            bct = bc[last:last + 1]
            sb = _bf(vw["sacc"])
            vw["bvh"] = [_bf(vw["bv"][:, h * B_DV:(h + 1) * B_DV]) for h in range(nh)]
            vw["oi"] = [_dot(sb[h * c:(h + 1) * c, :], vw["bvh"][h]) for h in range(nh)]
            vw["qd"] = stack_heads(_bf(vw["bq"] * jnp.exp(bc)))
            kd = vw["bk"] * jnp.exp(bct - bc)
            kdg = jnp.concatenate([kd, jnp.broadcast_to(jnp.exp(bct), (8, BQ_W))], axis=0).T
            vw["kdT"] = _bf(kdg[:, :c])
            vw["lgl"] = kdg[:, c:c + 1]

    def recur(pos, is_last):
        def run():
            for vw in views:
                if vw["pos"] != pos:
                    continue
                d = vw["d"]
                if d not in state:
                    state[d] = st_ref[d]
                oint = _dot(vw["qd"], state[d])
                for h in range(nh):
                    vw["ol"][0, vw["rows"], h * B_DV:(h + 1) * B_DV] = vw["oi"][h] + oint[h * c:(h + 1) * c]
                upd = jnp.concatenate([_dot(vw["kdT"][h * B_DK:(h + 1) * B_DK], vw["bvh"][h]) for h in range(nh)],
                                      axis=0)
                state[d] = state[d] * vw["lgl"] + upd
                if is_last:
                    st_ref[d] = state[d]
        return run

    return [prep] + [level(lv) for lv in range(LEVELS + 1)] + [intra], recur


def _scan_kernel(*refs):
    n_in = 9
    n_c = 6
    og_f, og_b, ol_f, ol_b = refs[2 * n_in + 2 * n_c:2 * n_in + 2 * n_c + 4]
    s_ref, st_ref = refs[2 * n_in + 2 * n_c + 4:]
    c = CHUNK

    @pl.when(pl.program_id(1) == 0)
    def _():
        s_ref[...] = jnp.zeros_like(s_ref)
        st_ref[...] = jnp.zeros_like(st_ref)

    views = []
    for d, (og, ol) in enumerate(((og_f, ol_f), (og_b, ol_b))):
        aq, ak, av, sm, smT, bq, bk, bv, gk = refs[d * n_in:(d + 1) * n_in]
        tri, tri2, lmask2, lmaskv, incl2, strict2 = refs[2 * n_in + d * n_c:2 * n_in + (d + 1) * n_c]
        order = range(STEP_CHUNKS) if d == 0 else range(STEP_CHUNKS - 1, -1, -1)
        for pos, j in enumerate(order):
            rows = pl.ds(j * c, c)
            views.append(dict(d=d, pos=pos, rows=rows, aq=aq[0, rows, :], ak=ak[0, rows, :], av=av[0, rows, :],
                              sm=sm[0, rows, :], smT=smT[j], bq=bq[0, rows, :], bk=bk[0, rows, :],
                              bv=bv[0, rows, :], gk=gk[0, rows, :], tri=tri[...], tri2=tri2[...], lmask2=lmask2, lmaskv=lmaskv,
                              incl2=incl2[...], strict2=strict2[...], og=og, ol=ol))
    gdn, gdn_recur = _gdn_stages(views, s_ref)
    gla, gla_recur = _gla_stages(views, st_ref)
    for i in range(max(len(gdn), len(gla))):
        if i < len(gdn):
            gdn[i]()
        if i < len(gla):
            gla[i]()
    for pos in range(STEP_CHUNKS):
        gdn_recur(pos, pos == STEP_CHUNKS - 1)()
        gla_recur(pos, pos == STEP_CHUNKS - 1)()


def _scans(aq, ak, av, sm, smT, bq, bk, bv, gk, ctx_len):
    bsz, ttot, _ = aq.shape
    c = CHUNK
    tb = STEP_CHUNKS * c
    nst = ttot // tb
    ncs = ctx_len // tb
    consts = _scan_consts()

    def fidx(b, s):
        return s

    def bidx(b, s):
        return jnp.where(s < ncs, ncs - 1 - s, nst - 1 - s + ncs)

    def tok(w, idx, lane_blk=0):
        return pl.BlockSpec((1, tb, w), lambda b, s: (b, idx(b, s), lane_blk))

    def dir_specs(idx, d):
        return [tok(A_W, idx), tok(A_W, idx), tok(A_W, idx), tok(SMALL_W, idx),
                pl.BlockSpec((None, STEP_CHUNKS, 16, c), lambda b, s: (b, idx(b, s), 0, 0)),
                tok(BQ_W, idx), tok(BQ_W, idx), tok(BV_W, idx), tok(BQ_W, idx, d)]

    const = lambda a: pl.BlockSpec(a.shape, lambda b, s: tuple(0 for _ in a.shape))
    cargs = []
    for d in range(NDIR):
        cd = consts[d]
        cargs += [cd["tri"], cd["tri2"], cd["lmask2"], cd["lmaskv"], cd["incl2"], cd["strict2"]]
    ins = [aq, ak, av, sm, smT, bq, bk, bv, gk]
    out_w = [A_HEADS * A_DV, A_HEADS * A_DV, BV_W, BV_W]
    return pl.pallas_call(
        _scan_kernel,
        grid=(bsz, nst),
        in_specs=dir_specs(fidx, 0) + dir_specs(bidx, 1) + [const(a) for a in cargs],
        out_specs=[tok(out_w[0], fidx), tok(out_w[1], bidx), tok(out_w[2], fidx), tok(out_w[3], bidx)],
        out_shape=[jax.ShapeDtypeStruct((bsz, ttot, w), F32) for w in out_w],
        scratch_shapes=[pltpu.VMEM((NDIR, A_HEADS, A_DK, A_DV), F32),
                        pltpu.VMEM((NDIR, BQ_W, B_DV), F32)],
        compiler_params=pltpu.CompilerParams(dimension_semantics=("arbitrary", "arbitrary"),
                                             vmem_limit_bytes=VMEM_LIMIT),
        name="scan",
    )(*ins, *ins, *cargs)


def _head_norm(o, w, z, n_heads, dv):
    parts = []
    for h in range(n_heads):
        oh = o[:, h * dv:(h + 1) * dv]
        parts.append(oh * lax.rsqrt(jnp.mean(oh * oh, axis=-1, keepdims=True) + EPS))
    return jnp.concatenate(parts, axis=-1) * w * _silu(z)


def _rms(x, w):
    return x * lax.rsqrt(jnp.mean(x * x, axis=-1, keepdims=True) + EPS) * w


def _post_kernel(x_ref, ogf_ref, ogb_ref, olf_ref, olb_ref, az_ref, bg_ref,
                 g1_ref, sh2_ref, sc2_ref, g2_ref, gn_ref, ln_ref, wo_ref, fn_ref,
                 wg_ref, wu_ref, wd_ref, fin_ref, o_ref):
    gdn = _head_norm(ogf_ref[0] + ogb_ref[0], gn_ref[...], az_ref[0], A_HEADS, A_DV)
    gla = _head_norm(olf_ref[0] + olb_ref[0], ln_ref[...], bg_ref[0], B_HEADS, B_DV)
    mix = jnp.concatenate([gdn, gla], axis=-1)
    x1 = x_ref[0] + g1_ref[...] * _dot(mix, wo_ref[...])
    h2 = _rms(x1, fn_ref[...]) * (1.0 + sc2_ref[...]) + sh2_ref[...]
    h2b = _bf(h2)
    gate = jnp.dot(h2b, wg_ref[...], preferred_element_type=F32)
    up = jnp.dot(h2b, wu_ref[...], preferred_element_type=F32)
    y = _dot(_silu(gate) * up, wd_ref[...])
    x2 = x1 + g2_ref[...] * y
    o_ref[0] = _rms(x2, fin_ref[...])


def _post(x, ogf, ogb, olf, olb, az, bg, mod4, gn, ln, wo, fn, wg, wu, wd, fin, ctx_len):
    bsz, seq, d = x.shape
    tm = TOK_TILE
    off = ctx_len // tm
    dff = wg.shape[1]

    def tok(w, o):
        return pl.BlockSpec((1, tm, w), lambda b, j: (b, j + o, 0))

    def modspec(k):
        return pl.BlockSpec((None, None, 1, d), lambda b, j: (b, k, 0, 0))

    def const(shape, single=True):
        idx = lambda b, j: tuple(0 for _ in shape)
        if single:
            return pl.BlockSpec(shape, idx, pipeline_mode=pl.Buffered(1))
        return pl.BlockSpec(shape, idx)

    return pl.pallas_call(
        _post_kernel,
        grid=(bsz, seq // tm),
        in_specs=[tok(d, 0), tok(A_W, off), tok(A_W, off), tok(BV_W, off), tok(BV_W, off),
                  tok(A_W, off), tok(BV_W, off),
                  modspec(2), modspec(3), modspec(4), modspec(5),
                  const((1, A_W), False), const((1, BV_W), False), const((d, d)), const((1, d), False),
                  const((d, dff)), const((d, dff)), const((dff, d)), const((1, d), False)],
        out_specs=tok(d, 0),
        out_shape=jax.ShapeDtypeStruct((bsz, seq, d), F32),
        compiler_params=pltpu.CompilerParams(dimension_semantics=("arbitrary", "arbitrary"),
                                             vmem_limit_bytes=VMEM_LIMIT),
        name="post",
    )(x, ogf, ogb, olf, olb, az, bg, mod4, mod4, mod4, mod4, gn, ln, wo, fn, wg, wu, wd, fin)


def kernel(x, c, ctx, c_ctx, w_mod, b_mod, attn_norm, w_in, conv_w, a_log, dt_bias, gdn_norm, gla_w2, gla_b,
           gla_norm, w_out, ffn_norm, w_gate, w_up, w_down, final_norm):
    bsz, seq, d = x.shape
    ctx_len = ctx.shape[1]
    assert w_mod.shape[0] == 1, "single-layer block"
    assert ctx_len == TOK_TILE and seq % TOK_TILE == 0 and TOK_TILE % GRID_W == 0 and TOK_TILE % (CHUNK * STEP_CHUNKS) == 0

    rows = -(-(bsz + 1) // 8) * 8
    cc = jnp.zeros((rows, d), F32).at[:bsz].set(c).at[rows - 1].set(c_ctx)
    mod = _modulation(cc, w_mod[0], b_mod[0][None, :])
    mod4 = mod.reshape(rows, 6, 1, d)

    w = w_in[0]
    o1 = 4 * A_W
    nb = NDIR * A_HEADS
    o2 = o1 + 2 * nb
    o3 = o2 + 2 * BQ_W + 2 * BV_W
    nlr = NDIR * GLA_RANK
    win = jnp.concatenate([w[:, :o1], w[:, o2:o3], w[:, o1:o2], w[:, o3:o3 + nlr],
                           jnp.zeros((d, SMALL_W - 2 * nb - nlr), F32)], axis=1).astype(BF16)
    gparams = jnp.zeros((2, SMALL_W), F32)
    gparams = gparams.at[0, nb:2 * nb].set(dt_bias[0].reshape(-1)).at[1, nb:2 * nb].set(a_log[0].reshape(-1))
    w2 = jnp.zeros((SMALL_W, NDIR * BQ_W), F32)
    for n in range(NDIR):
        w2 = w2.at[2 * nb + n * GLA_RANK:2 * nb + (n + 1) * GLA_RANK, n * BQ_W:(n + 1) * BQ_W].set(gla_w2[0, n])
    w2h = w2.astype(BF16)
    w2m = (w2 - w2h.astype(F32)).astype(BF16)

    aq, ak, av, az, bq, bk, bv, bg, gk, sm = _projection(
        x, ctx, mod4, attn_norm, win, conv_w[0], gparams, w2h, w2m, gla_b[0].reshape(1, -1))

    ttot = ctx_len + seq
    smT = sm[:, :, :16].reshape(bsz, ttot // CHUNK, CHUNK, 16).transpose(0, 1, 3, 2)
    ogf, ogb, olf, olb = _scans(aq, ak, av, sm, smT, bq, bk, bv, gk, ctx_len)

    tile_w = lambda v, n: jnp.tile(v.reshape(1, -1), (1, n))
    return _post(x, ogf, ogb, olf, olb, az, bg, mod4,
                 tile_w(gdn_norm[0], A_HEADS), tile_w(gla_norm[0], B_HEADS),
                 w_out[0].astype(BF16), ffn_norm, w_gate[0].astype(BF16), w_up[0].astype(BF16),
                 w_down[0].astype(BF16), final_norm.reshape(1, -1), ctx_len)
```

```python
import functools

import numpy as np
import jax
import jax.numpy as jnp
from jax import lax
from jax.experimental import pallas as pl
from jax.experimental.pallas import tpu as pltpu

F32 = jnp.float32
BF16 = jnp.bfloat16

GRID_W = 64
A_HEADS, A_DK, A_DV = 4, 128, 128
B_HEADS, B_DK, B_DV = 4, 64, 128
NDIR = 2
GLA_RANK = 16
GLA_NORMALIZER = 16.0
EPS = 1e-6

CHUNK = 64
STEP_CHUNKS = 4
TOK_TILE = 256
POST_SUBTILES = 2
SMALL_W = 128
LEVELS = int(np.log2(CHUNK))

A_W = A_HEADS * A_DK
BQ_W = B_HEADS * B_DK
BV_W = B_HEADS * B_DV
P_COLS = 4 * A_W + 2 * BQ_W + 2 * BV_W + SMALL_W

VMEM_LIMIT = 56 * 1024 * 1024


def _bf(x):
    return x.astype(BF16)


def _dot(a, b):
    return jnp.dot(_bf(a), _bf(b), preferred_element_type=F32)


def _dot_nt(a, b):
    return lax.dot_general(_bf(a), _bf(b), (((1,), (1,)), ((), ())), preferred_element_type=F32)


def _dot_tn(a, b):
    return lax.dot_general(_bf(a), _bf(b), (((0,), (0,)), ((), ())), preferred_element_type=F32)


def _split3(x):
    hi = _bf(x)
    r1 = x - hi.astype(F32)
    mid = _bf(r1)
    lo = _bf(r1 - mid.astype(F32))
    return hi, mid, lo


def _dot_exact_lhs(m, x, terms=3):
    d = lambda p: jnp.dot(m, p, preferred_element_type=F32)
    parts = _split3(x)[:terms]
    acc = d(parts[0])
    for p in parts[1:]:
        acc = acc + d(p)
    return acc


def _dot_exact_rhs_nt(x, m):
    hi, mid, lo = _split3(x)
    d = lambda p: lax.dot_general(p, m, (((1,), (1,)), ((), ())), preferred_element_type=F32)
    return d(hi) + d(mid) + d(lo)


def _sigmoid(x):
    return 0.5 * jnp.tanh(0.5 * x) + 0.5


def _silu(x):
    return x * _sigmoid(x)


def _softplus(x):
    return jnp.maximum(x, 0.0) + jnp.log(1.0 + jnp.exp(-jnp.abs(x)))


def _mod_kernel(cc_ref, w_ref, b_ref, o_ref):
    o_ref[...] = _dot(_silu(cc_ref[...]), w_ref[...]) + b_ref[...]


def _modulation(cc, w_mod, b_mod):
    rows, d = cc.shape
    n = w_mod.shape[1]
    tn = 1536
    return pl.pallas_call(
        _mod_kernel,
        grid=(n // tn,),
        in_specs=[pl.BlockSpec((rows, d), lambda j: (0, 0)),
                  pl.BlockSpec((d, tn), lambda j: (0, j)),
                  pl.BlockSpec((1, tn), lambda j: (0, j))],
        out_specs=pl.BlockSpec((rows, tn), lambda j: (0, j)),
        out_shape=jax.ShapeDtypeStruct((rows, n), F32),
        compiler_params=pltpu.CompilerParams(dimension_semantics=("arbitrary",),
                                             vmem_limit_bytes=VMEM_LIMIT),
        name="mod",
    )(cc, w_mod, b_mod)


def _proj_kernel(n_ctx_tiles, ctx_len,
                 x_ref, ctx_ref, sh_ref, sc_ref, nw_ref, win_ref, cw_ref, gp_ref, w2h_ref, w2m_ref, gb_ref,
                 aq_ref, ak_ref, av_ref, az_ref, bq_ref, bk_ref, bv_ref, bg_ref, gk_ref, sm_ref):
    j = pl.program_id(1)
    is_ctx = j < n_ctx_tiles
    xin = jnp.where(is_ctx, ctx_ref[0], x_ref[0])
    ms = jnp.mean(xin * xin, axis=-1, keepdims=True)
    h = xin * lax.rsqrt(ms + EPS) * (nw_ref[...] * (1.0 + sc_ref[...])) + sh_ref[...]
    hb = _bf(h)
    tm = xin.shape[0]
    sub = 8
    sublane = lax.broadcasted_iota(jnp.int32, (1, sub, 1), 1)
    cw = cw_ref[...]

    def mm(c0, c1):
        return jnp.dot(hb, win_ref[:, c0:c1], preferred_element_type=F32)

    def conv_silu(u, c0):
        wu = u.shape[1]
        u3 = u.reshape(tm // sub, sub, wu)

        def shifted(down):
            rot = pltpu.roll(u3, 1 if down else sub - 1, 1)
            nt = tm // sub
            tiles_per_row = GRID_W // sub
            zero = jnp.zeros((1, sub, wu), F32)
            nbr = []
            for r in range(nt):
                src = r - 1 if down else r + 1
                row_start = (r if down else src) % tiles_per_row == 0
                if src < 0 or src >= nt:
                    nbr.append(zero)
                elif row_start:
                    nbr.append(jnp.where(is_ctx, rot[src:src + 1], 0.0))
                else:
                    nbr.append(rot[src:src + 1])
            edge = sublane == (0 if down else sub - 1)
            return jnp.where(edge, jnp.concatenate(nbr, axis=0), rot).reshape(tm, wu)

        w = cw[:, c0:c0 + wu]
        return _silu(w[0:1] * shifted(True) + w[1:2] * u + w[2:3] * shifted(False))

    def l2n(t, scale):
        parts = []
        for hd in range(A_HEADS):
            th = t[:, hd * A_DK:(hd + 1) * A_DK]
            parts.append(th * (lax.rsqrt(jnp.sum(th * th, axis=-1, keepdims=True) + EPS) * scale))
        return jnp.concatenate(parts, axis=-1)

    uq, uk, uv = mm(0, A_W), mm(A_W, 2 * A_W), mm(2 * A_W, 3 * A_W)
    o = 3 * A_W
    aq_ref[0] = l2n(conv_silu(uq, 0), A_DK ** -0.5)
    p1 = mm(o, o + A_W + 2 * BQ_W)
    az_ref[0] = p1[:, :A_W]
    bq_ref[0] = p1[:, A_W:A_W + BQ_W] * (B_DK ** -0.5)
    bk_ref[0] = p1[:, A_W + BQ_W:]
    o += A_W + 2 * BQ_W
    ak_ref[0] = l2n(conv_silu(uk, A_W), 1.0)
    bv_ref[0] = mm(o, o + BV_W)
    o += BV_W
    av_ref[0] = conv_silu(uv, 2 * A_W)
    p3 = mm(o, P_COLS)
    bg_ref[0] = p3[:, :BV_W]

    ps = p3[:, BV_W:]
    gp = gp_ref[...]
    lane = lax.broadcasted_iota(jnp.int32, (1, SMALL_W), 1)
    beta = _sigmoid(ps)
    g = -jnp.exp(gp[1:2]) * _softplus(ps + gp[0:1])
    nb = NDIR * A_HEADS
    sm_ref[0] = jnp.where(lane < nb, beta, jnp.where(lane < 2 * nb, g, 0.0))

    ph = _bf(ps)
    pm = _bf(ps - ph.astype(F32))
    d = lambda a, b: jnp.dot(a, b, preferred_element_type=F32)
    pre = d(ph, w2h_ref[...]) + d(ph, w2m_ref[...]) + d(pm, w2h_ref[...]) + gb_ref[...]
    gk_ref[0] = -_softplus(-pre) * (1.0 / GLA_NORMALIZER)


def _projection(x, ctx, mod4, attn_norm, win, conv_w, gparams, w2h, w2m, gla_b):
    bsz, seq, d = x.shape
    ctx_len = ctx.shape[1]
    tm = TOK_TILE
    nct = ctx_len // tm
    nxt = seq // tm
    ttot = ctx_len + seq
    mod_rows = mod4.shape[0]

    def tok(w):
        return pl.BlockSpec((1, tm, w), lambda b, j: (b, j, 0))

    const = lambda shape: pl.BlockSpec(shape, lambda b, j: tuple(0 for _ in shape))
    widths = [A_W, A_W, A_W, A_W, BQ_W, BQ_W, BV_W, BV_W, NDIR * BQ_W, SMALL_W]
    return pl.pallas_call(
        functools.partial(_proj_kernel, nct, ctx_len),
        grid=(bsz, nct + nxt),
        in_specs=[
            pl.BlockSpec((1, tm, d), lambda b, j: (b, jnp.maximum(j - nct, 0), 0)),
            pl.BlockSpec((1, tm, d), lambda b, j: (b, jnp.minimum(j, nct - 1), 0)),
            pl.BlockSpec((None, None, 1, d), lambda b, j: (jnp.where(j < nct, mod_rows - 1, b), 0, 0, 0)),
            pl.BlockSpec((None, None, 1, d), lambda b, j: (jnp.where(j < nct, mod_rows - 1, b), 1, 0, 0)),
            const((1, d)), const((d, P_COLS)), const(conv_w.shape), const(gparams.shape),
            const(w2h.shape), const(w2m.shape), const(gla_b.shape),
        ],
        out_specs=[tok(w) for w in widths],
        out_shape=[jax.ShapeDtypeStruct((bsz, ttot, w), F32) for w in widths],
        compiler_params=pltpu.CompilerParams(dimension_semantics=("arbitrary", "arbitrary"),
                                             vmem_limit_bytes=VMEM_LIMIT),
        name="proj",
    )(x, ctx, mod4, mod4, attn_norm, win, conv_w, gparams, w2h, w2m, gla_b)


def _scan_consts():
    c = CHUNK
    i = np.arange(c)[:, None]
    t = np.arange(c)[None, :]
    out = []
    for d in range(NDIR):
        incl = (t <= i) if d == 0 else (t >= i)
        strict = (t < i) if d == 0 else (t > i)
        tri = incl.astype(np.float32)
        masks = []
        for lv in range(LEVELS):
            m = c >> (lv + 1)
            start = (np.arange(c) // (2 * m)) * (2 * m)
            later = (np.arange(c) % (2 * m) >= m) if d == 0 else (np.arange(c) % (2 * m) < m)
            same = (start[:, None] == start[None, :])
            masks.append((same & later[:, None] & ~later[None, :]).astype(np.float32))
        masks.append(np.eye(c, dtype=np.float32))
        tile2 = lambda mk: np.tile(mk, (1, 2))
        out.append(dict(
            tri=jnp.asarray(tri, BF16),
            tri2=jnp.asarray(np.tile(tri, (2, 1)), BF16),
            lmask2=jnp.asarray(np.stack([tile2(mk) for mk in masks]), F32),
            lmaskv=jnp.asarray(np.stack([np.tile(mk, (B_HEADS, 1)) for mk in masks]), F32),
            incl2=jnp.asarray(tile2(incl.astype(np.float32))),
            strict2=jnp.asarray(tile2(strict.astype(np.float32))),
        ))
    return out


def _gdn_stages(views, s_ref):
    c = CHUNK
    nh = A_HEADS
    npair = nh // 2
    w2 = 2 * c
    first = lax.broadcasted_iota(jnp.int32, (1, w2), 1) < c
    first_k = lax.broadcasted_iota(jnp.int32, (1, 2 * A_DK), 1) < A_DK
    lane_k = lax.broadcasted_iota(jnp.int32, (1, A_W), 1) // A_DK
    km = [lane_k == h for h in range(nh)]
    state = {}

    def expand(cols):
        out = cols[nh - 1]
        for h in range(nh - 2, -1, -1):
            out = jnp.where(km[h], cols[h], out)
        return out

    def blockdiag(x):
        xb = _bf(x)
        return jnp.concatenate([jnp.where(first, xb, 0), jnp.where(first, 0, xb)], axis=0)

    def heads(x, w):
        return [x[:, h * w:(h + 1) * w] for h in range(nh)]

    def prep():
        for vw in views:
            d = vw["d"]
            last = c - 1 if d == 0 else 0
            base = NDIR * nh + nh * d
            sm = vw["sm"]
            gc_all = _dot_exact_lhs(vw["tri"], sm)
            gcr_all = _dot_exact_rhs_nt(vw["smT"], vw["tri2"])
            gcc = [gc_all[:, base + h:base + h + 1] for h in range(nh)]
            gtot = [gc_all[last:last + 1, base + h:base + h + 1] for h in range(nh)]
            beta = [sm[:, nh * d + h:nh * d + h + 1] for h in range(nh)]
            egc = [jnp.exp(g) for g in gcc]
            vw["gl"] = [jnp.exp(g) for g in gtot]
            inc = vw["incl2"] > 0.5
            ak, aq = vw["ak"], vw["aq"]
            beta_k = expand(beta)
            kb = ak * beta_k
            egc_k = expand(egc)
            kbq = _bf(jnp.concatenate([kb, aq], axis=0))
            akb = _bf(ak)
            vw["decay"], vw["kbq"], vw["kst"] = [], [], []
            for p in range(npair):
                h0, h1 = 2 * p, 2 * p + 1
                diff = (jnp.where(first, gcc[h0], gcc[h1])
                        - jnp.where(first, gcr_all[base + h0:base + h0 + 1, :], gcr_all[base + h1:base + h1 + 1, :]))
                vw["decay"].append(jnp.where(inc, jnp.exp(jnp.where(inc, diff, 0.0)), 0.0))
                kp = akb[:, 2 * A_DK * p:2 * A_DK * (p + 1)]
                vw["kst"].append(jnp.concatenate([jnp.where(first_k, kp, 0), jnp.where(first_k, 0, kp)], axis=0))
                vw["kbq"].append(kbq[:, 2 * A_DK * p:2 * A_DK * (p + 1)])
            vb = heads(vw["av"] * beta_k, A_DV)
            kbe = heads(kb * egc_k, A_DK)
            rhs = [jnp.concatenate([vb[h], kbe[h]], axis=1) for h in range(nh)]
            vw["rhs"] = [_bf(jnp.concatenate([rhs[2 * p], rhs[2 * p + 1]], axis=0)) for p in range(npair)]
            vw["qe"] = heads(aq * egc_k, A_DK)
            vw["kdec"] = heads(_bf(ak * expand([jnp.exp(gtot[h] - gcc[h]) for h in range(nh)])), A_DK)
        for vw in views:
            vw["m"] = [_dot_nt(vw["kbq"][p], vw["kst"][p]) for p in range(npair)]
        for vw in views:
            lm = vw["lmask2"]
            vw["a"] = [vw["m"][p][:c] * vw["decay"][p] * vw["strict2"] for p in range(npair)]
            vw["qk"] = [blockdiag(vw["m"][p][c:] * vw["decay"][p]) for p in range(npair)]
            vw["t"] = [lm[LEVELS] - vw["a"][p] * lm[LEVELS - 1] for p in range(npair)]

    def level(lv):
        def run():
            for vw in views:
                vw["tl"] = [_dot(vw["t"][p], blockdiag(vw["a"][p] * vw["lmask2"][lv])) for p in range(npair)]
            for vw in views:
                vw["t"] = [vw["t"][p] - _dot(vw["tl"][p], blockdiag(vw["t"][p])) for p in range(npair)]
        return run

    def solve():
        for vw in views:
            rp = [_dot(blockdiag(vw["t"][p]), vw["rhs"][p]) for p in range(npair)]
            vw["r"] = [rp[h // 2][(h % 2) * c:(h % 2 + 1) * c] for h in range(nh)]
        for vw in views:
            vw["lhs"] = [_bf(jnp.concatenate([vw["r"][h][:, A_DV:], vw["qe"][h]], axis=0)) for h in range(nh)]

    def recur(pos, is_last):
        def run():
            cur = [vw for vw in views if vw["pos"] == pos]
            for vw in cur:
                d = vw["d"]
                for h in range(nh):
                    if (d, h) not in state:
                        state[(d, h)] = s_ref[d, h]
                vw["ws"] = [_dot(vw["lhs"][h], state[(d, h)]) for h in range(nh)]
            for vw in cur:
                vw["vn"] = [vw["r"][h][:, :A_DV] - vw["ws"][h][:c] for h in range(nh)]
            for vw in cur:
                for p in range(npair):
                    vn2 = _bf(jnp.concatenate([vw["vn"][2 * p], vw["vn"][2 * p + 1]], axis=0))
                    oq = _dot(vw["qk"][p], vn2)
                    for e in range(2):
                        h = 2 * p + e
                        vw["og"][0, vw["rows"], h * A_DV:(h + 1) * A_DV] = vw["ws"][h][c:] + oq[e * c:(e + 1) * c]
            for vw in cur:
                d = vw["d"]
                for h in range(nh):
                    state[(d, h)] = state[(d, h)] * vw["gl"][h] + _dot_tn(vw["kdec"][h], vw["vn"][h])
                    if is_last:
                        s_ref[d, h] = state[(d, h)]
        return run

    return [prep] + [level(lv) for lv in range(LEVELS - 2, -1, -1)] + [solve], recur


def _gla_stages(views, st_ref):
    c = CHUNK
    nh = B_HEADS
    lane_q = lax.broadcasted_iota(jnp.int32, (1, BQ_W), 1) // B_DK
    row = lax.broadcasted_iota(jnp.int32, (c, 1), 0)
    state = {}

    def stack_heads(x):
        return jnp.concatenate([jnp.where(lane_q == h, x, 0) for h in range(nh)], axis=0)

    def level_abs(vw, lv):
        m = c >> (lv + 1)
        fwd = vw["d"] == 0
        bc, na = vw["bc"], vw["na"]
        if m >= 4:
            refs = [s + (m - 1 if fwd else m) for s in range(0, c, 2 * m)]
            ref_rows = jnp.concatenate([jnp.broadcast_to(bc[r:r + 1, :], (2 * m, BQ_W)) for r in refs], axis=0)
            return jnp.abs(bc - ref_rows)
        r = row % (2 * m)
        up = pltpu.roll(na, c - 1, 0)
        dn = pltpu.roll(na, 1, 0)
        if m == 2:
            if fwd:
                return jnp.where(r == 0, up, jnp.where(r == 1, 0.0, jnp.where(r == 2, na, na + dn)))
            return jnp.where(r == 0, na + up, jnp.where(r == 1, na, jnp.where(r == 2, 0.0, dn)))
        return jnp.where(r == (1 if fwd else 0), na, 0.0)

    def prep():
        for vw in views:
            vw["bc"] = _dot_exact_lhs(vw["tri"], vw["gk"])
            vw["na"] = -vw["gk"]
            vw["sacc"] = jnp.zeros((nh * c, c), F32)

    def level(lv):
        def run():
            for vw in views:
                if lv < LEVELS:
                    wgt = jnp.exp(-level_abs(vw, lv))
                    ql, kl = _bf(vw["bq"] * wgt), _bf(vw["bk"] * wgt)
                else:
                    ql, kl = _bf(vw["bq"]), _bf(vw["bk"])
                vw["sacc"] = vw["sacc"] + _dot_nt(stack_heads(ql), kl) * vw["lmaskv"][lv]
        return run

    def intra():
        for vw in views:
            last = c - 1 if vw["d"] == 0 else 0
            bc = vw["bc"]
            bct = bc[last:last + 1]
            sb = _bf(vw["sacc"])
            vw["bvh"] = [_bf(vw["bv"][:, h * B_DV:(h + 1) * B_DV]) for h in range(nh)]
            vw["oi"] = [_dot(sb[h * c:(h + 1) * c, :], vw["bvh"][h]) for h in range(nh)]
            vw["qd"] = stack_heads(_bf(vw["bq"] * jnp.exp(bc)))
            kd = vw["bk"] * jnp.exp(bct - bc)
            kdg = jnp.concatenate([kd, jnp.broadcast_to(jnp.exp(bct), (8, BQ_W))], axis=0).T
            vw["kdT"] = _bf(kdg[:, :c])
            vw["lgl"] = kdg[:, c:c + 1]

    def recur(pos, is_last):
        def run():
            for vw in views:
                if vw["pos"] != pos:
                    continue
                d = vw["d"]
                if d not in state:
                    state[d] = st_ref[d]
                oint = _dot(vw["qd"], state[d])
                for h in range(nh):
                    vw["ol"][0, vw["rows"], h * B_DV:(h + 1) * B_DV] = vw["oi"][h] + oint[h * c:(h + 1) * c]
                upd = jnp.concatenate([_dot(vw["kdT"][h * B_DK:(h + 1) * B_DK], vw["bvh"][h]) for h in range(nh)],
                                      axis=0)
                state[d] = state[d] * vw["lgl"] + upd
                if is_last:
                    st_ref[d] = state[d]
        return run

    return [prep] + [level(lv) for lv in range(LEVELS + 1)] + [intra], recur


def _scan_kernel(*refs):
    n_in = 9
    n_c = 6
    og_f, og_b, ol_f, ol_b = refs[2 * n_in + 2 * n_c:2 * n_in + 2 * n_c + 4]
    s_ref, st_ref = refs[2 * n_in + 2 * n_c + 4:]
    c = CHUNK

    @pl.when(pl.program_id(1) == 0)
    def _():
        s_ref[...] = jnp.zeros_like(s_ref)
        st_ref[...] = jnp.zeros_like(st_ref)

    views = []
    for d, (og, ol) in enumerate(((og_f, ol_f), (og_b, ol_b))):
        aq, ak, av, sm, smT, bq, bk, bv, gk = refs[d * n_in:(d + 1) * n_in]
        tri, tri2, lmask2, lmaskv, incl2, strict2 = refs[2 * n_in + d * n_c:2 * n_in + (d + 1) * n_c]
        order = range(STEP_CHUNKS) if d == 0 else range(STEP_CHUNKS - 1, -1, -1)
        for pos, j in enumerate(order):
            rows = pl.ds(j * c, c)
            views.append(dict(d=d, pos=pos, rows=rows, aq=aq[0, rows, :], ak=ak[0, rows, :], av=av[0, rows, :],
                              sm=sm[0, rows, :], smT=smT[j], bq=bq[0, rows, :], bk=bk[0, rows, :],
                              bv=bv[0, rows, :], gk=gk[0, rows, :], tri=tri[...], tri2=tri2[...], lmask2=lmask2, lmaskv=lmaskv,
                              incl2=incl2[...], strict2=strict2[...], og=og, ol=ol))
    gdn, gdn_recur = _gdn_stages(views, s_ref)
    gla, gla_recur = _gla_stages(views, st_ref)
    for i in range(max(len(gdn), len(gla))):
        if i < len(gdn):
            gdn[i]()
        if i < len(gla):
            gla[i]()
    for pos in range(STEP_CHUNKS):
        gdn_recur(pos, pos == STEP_CHUNKS - 1)()
        gla_recur(pos, pos == STEP_CHUNKS - 1)()


def _scans(aq, ak, av, sm, smT, bq, bk, bv, gk, ctx_len):
    bsz, ttot, _ = aq.shape
    c = CHUNK
    tb = STEP_CHUNKS * c
    nst = ttot // tb
    ncs = ctx_len // tb
    consts = _scan_consts()

    def fidx(b, s):
        return s

    def bidx(b, s):
        return jnp.where(s < ncs, ncs - 1 - s, nst - 1 - s + ncs)

    def tok(w, idx, lane_blk=0):
        return pl.BlockSpec((1, tb, w), lambda b, s: (b, idx(b, s), lane_blk))

    def dir_specs(idx, d):
        return [tok(A_W, idx), tok(A_W, idx), tok(A_W, idx), tok(SMALL_W, idx),
                pl.BlockSpec((None, STEP_CHUNKS, 16, c), lambda b, s: (b, idx(b, s), 0, 0)),
                tok(BQ_W, idx), tok(BQ_W, idx), tok(BV_W, idx), tok(BQ_W, idx, d)]

    const = lambda a: pl.BlockSpec(a.shape, lambda b, s: tuple(0 for _ in a.shape))
    cargs = []
    for d in range(NDIR):
        cd = consts[d]
        cargs += [cd["tri"], cd["tri2"], cd["lmask2"], cd["lmaskv"], cd["incl2"], cd["strict2"]]
    ins = [aq, ak, av, sm, smT, bq, bk, bv, gk]
    out_w = [A_HEADS * A_DV, A_HEADS * A_DV, BV_W, BV_W]
    return pl.pallas_call(
        _scan_kernel,
        grid=(bsz, nst),
        in_specs=dir_specs(fidx, 0) + dir_specs(bidx, 1) + [const(a) for a in cargs],
        out_specs=[tok(out_w[0], fidx), tok(out_w[1], bidx), tok(out_w[2], fidx), tok(out_w[3], bidx)],
        out_shape=[jax.ShapeDtypeStruct((bsz, ttot, w), F32) for w in out_w],
        scratch_shapes=[pltpu.VMEM((NDIR, A_HEADS, A_DK, A_DV), F32),
                        pltpu.VMEM((NDIR, BQ_W, B_DV), F32)],
        compiler_params=pltpu.CompilerParams(dimension_semantics=("arbitrary", "arbitrary"),
                                             vmem_limit_bytes=VMEM_LIMIT),
        name="scan",
    )(*ins, *ins, *cargs)


def _head_norm(o, w, z, n_heads, dv):
    parts = []
    for h in range(n_heads):
        oh = o[:, h * dv:(h + 1) * dv]
        parts.append(oh * lax.rsqrt(jnp.mean(oh * oh, axis=-1, keepdims=True) + EPS))
    return jnp.concatenate(parts, axis=-1) * w * _silu(z)


def _rms(x, w):
    return x * lax.rsqrt(jnp.mean(x * x, axis=-1, keepdims=True) + EPS) * w


def _post_kernel(*refs):
    ns = POST_SUBTILES
    x_ref = refs[0]
    sub = [refs[1 + 6 * s:7 + 6 * s] for s in range(ns)]
    (g1_ref, sh2_ref, sc2_ref, g2_ref, gn_ref, ln_ref, wo_ref, fn_ref,
     wg_ref, wu_ref, wd_ref, fin_ref, o_ref) = refs[1 + 6 * ns:]
    tm = TOK_TILE
    for s in range(ns):
        ogf, ogb, olf, olb, az, bg = sub[s]
        rows = pl.ds(s * tm, tm)
        gdn = _head_norm(ogf[0] + ogb[0], gn_ref[...], az[0], A_HEADS, A_DV)
        gla = _head_norm(olf[0] + olb[0], ln_ref[...], bg[0], B_HEADS, B_DV)
        mix = jnp.concatenate([gdn, gla], axis=-1)
        x1 = x_ref[0, rows, :] + g1_ref[...] * _dot(mix, wo_ref[...])
        h2b = _bf(_rms(x1, fn_ref[...]) * (1.0 + sc2_ref[...]) + sh2_ref[...])
        gate = jnp.dot(h2b, wg_ref[...], preferred_element_type=F32)
        up = jnp.dot(h2b, wu_ref[...], preferred_element_type=F32)
        y = _dot(_silu(gate) * up, wd_ref[...])
        x2 = x1 + g2_ref[...] * y
        o_ref[0, rows, :] = _rms(x2, fin_ref[...])


def _post(x, ogf, ogb, olf, olb, az, bg, mod4, gn, ln, wo, fn, wg, wu, wd, fin, ctx_len):
    bsz, seq, d = x.shape
    tm = TOK_TILE
    ns = POST_SUBTILES
    off = ctx_len // tm
    dff = wg.shape[1]

    def tok(w, s):
        return pl.BlockSpec((1, tm, w), lambda b, j: (b, ns * j + s + off, 0))

    def xtok():
        return pl.BlockSpec((1, ns * tm, d), lambda b, j: (b, j, 0))

    def modspec(k):
        return pl.BlockSpec((None, None, 1, d), lambda b, j: (b, k, 0, 0))

    def const(shape, single=True):
        idx = lambda b, j: tuple(0 for _ in shape)
        if single:
            return pl.BlockSpec(shape, idx, pipeline_mode=pl.Buffered(1))
        return pl.BlockSpec(shape, idx)

    widths = [A_W, A_W, BV_W, BV_W, A_W, BV_W]
    return pl.pallas_call(
        _post_kernel,
        grid=(bsz, seq // (ns * tm)),
        in_specs=[xtok()] + [tok(w, s) for s in range(ns) for w in widths] + [
                  modspec(2), modspec(3), modspec(4), modspec(5),
                  const((1, A_W), False), const((1, BV_W), False), const((d, d)), const((1, d), False),
                  const((d, dff)), const((d, dff)), const((dff, d)), const((1, d), False)],
        out_specs=xtok(),
        out_shape=jax.ShapeDtypeStruct((bsz, seq, d), F32),
        compiler_params=pltpu.CompilerParams(dimension_semantics=("arbitrary", "arbitrary"),
                                             vmem_limit_bytes=VMEM_LIMIT),
        name="post",
    )(x, *([ogf, ogb, olf, olb, az, bg] * ns), mod4, mod4, mod4, mod4, gn, ln, wo, fn, wg, wu, wd, fin)


def kernel(x, c, ctx, c_ctx, w_mod, b_mod, attn_norm, w_in, conv_w, a_log, dt_bias, gdn_norm, gla_w2, gla_b,
           gla_norm, w_out, ffn_norm, w_gate, w_up, w_down, final_norm):
    bsz, seq, d = x.shape
    ctx_len = ctx.shape[1]
    assert w_mod.shape[0] == 1, "single-layer block"
    assert ctx_len == TOK_TILE and seq % (TOK_TILE * POST_SUBTILES) == 0
    assert TOK_TILE % GRID_W == 0 and TOK_TILE % (CHUNK * STEP_CHUNKS) == 0

    rows = -(-(bsz + 1) // 8) * 8
    cc = jnp.zeros((rows, d), F32).at[:bsz].set(c).at[rows - 1].set(c_ctx)
    mod = _modulation(cc, w_mod[0], b_mod[0][None, :])
    mod4 = mod.reshape(rows, 6, 1, d)

    w = w_in[0]
    o1 = 4 * A_W
    nb = NDIR * A_HEADS
    o2 = o1 + 2 * nb
    o3 = o2 + 2 * BQ_W + 2 * BV_W
    nlr = NDIR * GLA_RANK
    win = jnp.concatenate([w[:, :o1], w[:, o2:o3], w[:, o1:o2], w[:, o3:o3 + nlr],
                           jnp.zeros((d, SMALL_W - 2 * nb - nlr), F32)], axis=1).astype(BF16)
    gparams = jnp.zeros((2, SMALL_W), F32)
    gparams = gparams.at[0, nb:2 * nb].set(dt_bias[0].reshape(-1)).at[1, nb:2 * nb].set(a_log[0].reshape(-1))
    w2 = jnp.zeros((SMALL_W, NDIR * BQ_W), F32)
    for n in range(NDIR):
        w2 = w2.at[2 * nb + n * GLA_RANK:2 * nb + (n + 1) * GLA_RANK, n * BQ_W:(n + 1) * BQ_W].set(gla_w2[0, n])
    w2h = w2.astype(BF16)
    w2m = (w2 - w2h.astype(F32)).astype(BF16)

    aq, ak, av, az, bq, bk, bv, bg, gk, sm = _projection(
        x, ctx, mod4, attn_norm, win, conv_w[0], gparams, w2h, w2m, gla_b[0].reshape(1, -1))

    ttot = ctx_len + seq
    smT = sm[:, :, :16].reshape(bsz, ttot // CHUNK, CHUNK, 16).transpose(0, 1, 3, 2)
    ogf, ogb, olf, olb = _scans(aq, ak, av, sm, smT, bq, bk, bv, gk, ctx_len)

    tile_w = lambda v, n: jnp.tile(v.reshape(1, -1), (1, n))
    return _post(x, ogf, ogb, olf, olb, az, bg, mod4,
                 tile_w(gdn_norm[0], A_HEADS), tile_w(gla_norm[0], B_HEADS),
                 w_out[0].astype(BF16), ffn_norm, w_gate[0].astype(BF16), w_up[0].astype(BF16),
                 w_down[0].astype(BF16), final_norm.reshape(1, -1), ctx_len)
```

```python
import functools

import numpy as np
import jax
import jax.numpy as jnp
from jax import lax
from jax.experimental import pallas as pl
from jax.experimental.pallas import tpu as pltpu

F32 = jnp.float32
BF16 = jnp.bfloat16

GRID_W = 64
A_HEADS, A_DK, A_DV = 4, 128, 128
B_HEADS, B_DK, B_DV = 4, 64, 128
NDIR = 2
GLA_RANK = 16
GLA_NORMALIZER = 16.0
EPS = 1e-6

CHUNK = 64
STEP_CHUNKS = 4
SUBLANES = 8
TOK_TILE = 256
POST_SUBTILES = 2
SMALL_W = 128
LEVELS = int(np.log2(CHUNK))

A_W = A_HEADS * A_DK
BQ_W = B_HEADS * B_DK
BV_W = B_HEADS * B_DV
P_COLS = 4 * A_W + 2 * BQ_W + 2 * BV_W + SMALL_W

VMEM_LIMIT = 56 * 1024 * 1024


def _bf(x):
    return x.astype(BF16)


def _dot(a, b):
    return jnp.dot(_bf(a), _bf(b), preferred_element_type=F32)


def _dot_nt(a, b):
    return lax.dot_general(_bf(a), _bf(b), (((1,), (1,)), ((), ())), preferred_element_type=F32)


def _dot_tn(a, b):
    return lax.dot_general(_bf(a), _bf(b), (((0,), (0,)), ((), ())), preferred_element_type=F32)


def _split3(x):
    hi = _bf(x)
    r1 = x - hi.astype(F32)
    mid = _bf(r1)
    lo = _bf(r1 - mid.astype(F32))
    return hi, mid, lo


def _dot_exact_lhs(m, x, terms=3):
    d = lambda p: jnp.dot(m, p, preferred_element_type=F32)
    parts = _split3(x)[:terms]
    acc = d(parts[0])
    for p in parts[1:]:
        acc = acc + d(p)
    return acc


def _dot_exact_rhs_nt(x, m):
    hi, mid, lo = _split3(x)
    d = lambda p: lax.dot_general(p, m, (((1,), (1,)), ((), ())), preferred_element_type=F32)
    return d(hi) + d(mid) + d(lo)


def _sigmoid(x):
    return 0.5 * jnp.tanh(0.5 * x) + 0.5


def _silu(x):
    return x * _sigmoid(x)


def _softplus(x):
    return jnp.maximum(x, 0.0) + jnp.log(1.0 + jnp.exp(-jnp.abs(x)))


def _later_rows(x, m, fwd):
    n = x.shape[0]
    return jnp.concatenate([x[s + m:s + 2 * m] if fwd else x[s:s + m] for s in range(0, n, 2 * m)], axis=0)


def _merge_later_rows(full, later, m, fwd):
    n = full.shape[0]
    pieces = []
    for i, s in enumerate(range(0, n, 2 * m)):
        lat = later[i * m:(i + 1) * m]
        pieces += [full[s:s + m], lat] if fwd else [lat, full[s + m:s + 2 * m]]
    return jnp.concatenate(pieces, axis=0)


def _mod_kernel(cc_ref, w_ref, b_ref, o_ref):
    o_ref[...] = _dot(_silu(cc_ref[...]), w_ref[...]) + b_ref[...]


def _modulation(cc, w_mod, b_mod):
    rows, d = cc.shape
    n = w_mod.shape[1]
    tn = 1536
    return pl.pallas_call(
        _mod_kernel,
        grid=(n // tn,),
        in_specs=[pl.BlockSpec((rows, d), lambda j: (0, 0)),
                  pl.BlockSpec((d, tn), lambda j: (0, j)),
                  pl.BlockSpec((1, tn), lambda j: (0, j))],
        out_specs=pl.BlockSpec((rows, tn), lambda j: (0, j)),
        out_shape=jax.ShapeDtypeStruct((rows, n), F32),
        compiler_params=pltpu.CompilerParams(dimension_semantics=("arbitrary",),
                                             vmem_limit_bytes=VMEM_LIMIT),
        name="mod",
    )(cc, w_mod, b_mod)


def _proj_kernel(n_ctx_tiles, ctx_len,
                 x_ref, ctx_ref, sh_ref, sc_ref, nw_ref, win_ref, cw_ref, gp_ref, w2h_ref, w2m_ref, gb_ref,
                 aq_ref, ak_ref, av_ref, az_ref, bq_ref, bk_ref, bv_ref, bg_ref, gk_ref, sm_ref):
    j = pl.program_id(1)
    is_ctx = j < n_ctx_tiles
    xin = jnp.where(is_ctx, ctx_ref[0], x_ref[0])
    ms = jnp.mean(xin * xin, axis=-1, keepdims=True)
    h = xin * lax.rsqrt(ms + EPS) * (nw_ref[...] * (1.0 + sc_ref[...])) + sh_ref[...]
    hb = _bf(h)
    tm = xin.shape[0]
    sub = 8
    sublane = lax.broadcasted_iota(jnp.int32, (1, sub, 1), 1)
    cw = cw_ref[...]

    def mm(c0, c1):
        return jnp.dot(hb, win_ref[:, c0:c1], preferred_element_type=F32)

    def conv_silu(u, c0):
        wu = u.shape[1]
        u3 = u.reshape(tm // sub, sub, wu)

        def shifted(down):
            rot = pltpu.roll(u3, 1 if down else sub - 1, 1)
            nt = tm // sub
            tiles_per_row = GRID_W // sub
            zero = jnp.zeros((1, sub, wu), F32)
            nbr = []
            for r in range(nt):
                src = r - 1 if down else r + 1
                row_start = (r if down else src) % tiles_per_row == 0
                if src < 0 or src >= nt:
                    nbr.append(zero)
                elif row_start:
                    nbr.append(jnp.where(is_ctx, rot[src:src + 1], 0.0))
                else:
                    nbr.append(rot[src:src + 1])
            edge = sublane == (0 if down else sub - 1)
            return jnp.where(edge, jnp.concatenate(nbr, axis=0), rot).reshape(tm, wu)

        w = cw[:, c0:c0 + wu]
        return _silu(w[0:1] * shifted(True) + w[1:2] * u + w[2:3] * shifted(False))

    def l2n(t, scale):
        parts = []
        for hd in range(A_HEADS):
            th = t[:, hd * A_DK:(hd + 1) * A_DK]
            parts.append(th * (lax.rsqrt(jnp.sum(th * th, axis=-1, keepdims=True) + EPS) * scale))
        return jnp.concatenate(parts, axis=-1)

    uq, uk, uv = mm(0, A_W), mm(A_W, 2 * A_W), mm(2 * A_W, 3 * A_W)
    o = 3 * A_W
    aq_ref[0] = l2n(conv_silu(uq, 0), A_DK ** -0.5)
    p1 = mm(o, o + A_W + 2 * BQ_W)
    az_ref[0] = p1[:, :A_W]
    bq_ref[0] = p1[:, A_W:A_W + BQ_W] * (B_DK ** -0.5)
    bk_ref[0] = p1[:, A_W + BQ_W:]
    o += A_W + 2 * BQ_W
    ak_ref[0] = l2n(conv_silu(uk, A_W), 1.0)
    bv_ref[0] = mm(o, o + BV_W)
    o += BV_W
    av_ref[0] = conv_silu(uv, 2 * A_W)
    p3 = mm(o, P_COLS)
    bg_ref[0] = p3[:, :BV_W]

    ps = p3[:, BV_W:]
    gp = gp_ref[...]
    lane = lax.broadcasted_iota(jnp.int32, (1, SMALL_W), 1)
    beta = _sigmoid(ps)
    g = -jnp.exp(gp[1:2]) * _softplus(ps + gp[0:1])
    nb = NDIR * A_HEADS
    sm_ref[0] = jnp.where(lane < nb, beta, jnp.where(lane < 2 * nb, g, 0.0))

    ph = _bf(ps)
    pm = _bf(ps - ph.astype(F32))
    d = lambda a, b: jnp.dot(a, b, preferred_element_type=F32)
    pre = d(ph, w2h_ref[...]) + d(ph, w2m_ref[...]) + d(pm, w2h_ref[...]) + gb_ref[...]
    gk_ref[0] = -_softplus(-pre) * (1.0 / GLA_NORMALIZER)


def _projection(x, ctx, mod4, attn_norm, win, conv_w, gparams, w2h, w2m, gla_b):
    bsz, seq, d = x.shape
    ctx_len = ctx.shape[1]
    tm = TOK_TILE
    nct = ctx_len // tm
    nxt = seq // tm
    ttot = ctx_len + seq
    mod_rows = mod4.shape[0]

    def tok(w):
        return pl.BlockSpec((1, tm, w), lambda b, j: (b, j, 0))

    const = lambda shape: pl.BlockSpec(shape, lambda b, j: tuple(0 for _ in shape))
    widths = [A_W, A_W, A_W, A_W, BQ_W, BQ_W, BV_W, BV_W, NDIR * BQ_W, SMALL_W]
    return pl.pallas_call(
        functools.partial(_proj_kernel, nct, ctx_len),
        grid=(bsz, nct + nxt),
        in_specs=[
            pl.BlockSpec((1, tm, d), lambda b, j: (b, jnp.maximum(j - nct, 0), 0)),
            pl.BlockSpec((1, tm, d), lambda b, j: (b, jnp.minimum(j, nct - 1), 0)),
            pl.BlockSpec((None, None, 1, d), lambda b, j: (jnp.where(j < nct, mod_rows - 1, b), 0, 0, 0)),
            pl.BlockSpec((None, None, 1, d), lambda b, j: (jnp.where(j < nct, mod_rows - 1, b), 1, 0, 0)),
            const((1, d)), const((d, P_COLS)), const(conv_w.shape), const(gparams.shape),
            const(w2h.shape), const(w2m.shape), const(gla_b.shape),
        ],
        out_specs=[tok(w) for w in widths],
        out_shape=[jax.ShapeDtypeStruct((bsz, ttot, w), F32) for w in widths],
        compiler_params=pltpu.CompilerParams(dimension_semantics=("arbitrary", "arbitrary"),
                                             vmem_limit_bytes=VMEM_LIMIT),
        name="proj",
    )(x, ctx, mod4, mod4, attn_norm, win, conv_w, gparams, w2h, w2m, gla_b)


def _scan_consts():
    c = CHUNK
    i = np.arange(c)[:, None]
    t = np.arange(c)[None, :]
    out = []
    for d in range(NDIR):
        incl = (t <= i) if d == 0 else (t >= i)
        strict = (t < i) if d == 0 else (t > i)
        tri = incl.astype(np.float32)
        masks = []
        for lv in range(LEVELS):
            m = c >> (lv + 1)
            start = (np.arange(c) // (2 * m)) * (2 * m)
            later = (np.arange(c) % (2 * m) >= m) if d == 0 else (np.arange(c) % (2 * m) < m)
            same = (start[:, None] == start[None, :])
            masks.append((same & later[:, None] & ~later[None, :]).astype(np.float32))
        masks.append(np.eye(c, dtype=np.float32))
        tile2 = lambda mk: np.tile(mk, (1, 2))
        out.append(dict(
            tri=jnp.asarray(tri, BF16),
            tri2=jnp.asarray(np.tile(tri, (2, 1)), BF16),
            lmask2=jnp.asarray(np.stack([tile2(mk) for mk in masks]), F32),
            lmaskv=jnp.asarray(np.stack([np.tile(mk, (B_HEADS, 1)) for mk in masks]), F32),
            incl2=jnp.asarray(tile2(incl.astype(np.float32))),
            strict2=jnp.asarray(tile2(strict.astype(np.float32))),
        ))
    return out


def _gdn_stages(views, s_ref, state):
    c = CHUNK
    nh = A_HEADS
    npair = nh // 2
    w2 = 2 * c
    first = lax.broadcasted_iota(jnp.int32, (1, w2), 1) < c
    first_k = lax.broadcasted_iota(jnp.int32, (1, 2 * A_DK), 1) < A_DK
    lane_k = lax.broadcasted_iota(jnp.int32, (1, A_W), 1) // A_DK
    km = [lane_k == h for h in range(nh)]

    def expand(cols):
        out = cols[nh - 1]
        for h in range(nh - 2, -1, -1):
            out = jnp.where(km[h], cols[h], out)
        return out

    def blockdiag(x):
        xb = _bf(x)
        return jnp.concatenate([jnp.where(first, xb, 0), jnp.where(first, 0, xb)], axis=0)

    def heads(x, w):
        return [x[:, h * w:(h + 1) * w] for h in range(nh)]

    def prep():
        for vw in views:
            d = vw["d"]
            last = c - 1 if d == 0 else 0
            base = NDIR * nh + nh * d
            sm = vw["sm"]
            gc_all = _dot_exact_lhs(vw["tri"], sm)
            gcr_all = _dot_exact_rhs_nt(vw["smT"], vw["tri2"])
            gcc = [gc_all[:, base + h:base + h + 1] for h in range(nh)]
            gtot = [gc_all[last:last + 1, base + h:base + h + 1] for h in range(nh)]
            beta = [sm[:, nh * d + h:nh * d + h + 1] for h in range(nh)]
            egc = [jnp.exp(g) for g in gcc]
            vw["gl"] = [jnp.exp(g) for g in gtot]
            inc = vw["incl2"] > 0.5
            ak, aq = vw["ak"], vw["aq"]
            beta_k = expand(beta)
            kb = ak * beta_k
            egc_k = expand(egc)
            kbq = _bf(jnp.concatenate([kb, aq], axis=0))
            akb = _bf(ak)
            vw["decay"], vw["kbq"], vw["kst"] = [], [], []
            for p in range(npair):
                h0, h1 = 2 * p, 2 * p + 1
                diff = (jnp.where(first, gcc[h0], gcc[h1])
                        - jnp.where(first, gcr_all[base + h0:base + h0 + 1, :], gcr_all[base + h1:base + h1 + 1, :]))
                vw["decay"].append(jnp.where(inc, jnp.exp(jnp.where(inc, diff, 0.0)), 0.0))
                kp = akb[:, 2 * A_DK * p:2 * A_DK * (p + 1)]
                vw["kst"].append(jnp.concatenate([jnp.where(first_k, kp, 0), jnp.where(first_k, 0, kp)], axis=0))
                vw["kbq"].append(kbq[:, 2 * A_DK * p:2 * A_DK * (p + 1)])
            vb = heads(vw["av"] * beta_k, A_DV)
            kbe = heads(kb * egc_k, A_DK)
            rhs = [jnp.concatenate([vb[h], kbe[h]], axis=1) for h in range(nh)]
            vw["rhs"] = [_bf(jnp.concatenate([rhs[2 * p], rhs[2 * p + 1]], axis=0)) for p in range(npair)]
            vw["qe"] = heads(aq * egc_k, A_DK)
            vw["kdec"] = heads(_bf(ak * expand([jnp.exp(gtot[h] - gcc[h]) for h in range(nh)])), A_DK)
        for vw in views:
            vw["m"] = [_dot_nt(vw["kbq"][p], vw["kst"][p]) for p in range(npair)]
        for vw in views:
            lm = vw["lmask2"]
            vw["a"] = [vw["m"][p][:c] * vw["decay"][p] * vw["strict2"] for p in range(npair)]
            vw["qk"] = [blockdiag(vw["m"][p][c:] * vw["decay"][p]) for p in range(npair)]
            vw["t"] = [lm[LEVELS] - vw["a"][p] * lm[LEVELS - 1] for p in range(npair)]

    def level(lv):
        m = c >> (lv + 1)
        cut = m % SUBLANES == 0

        def run():
            for vw in views:
                fwd = vw["d"] == 0
                rows = (lambda x: _later_rows(x, m, fwd)) if cut else (lambda x: x)
                vw["tl"] = [_dot(rows(vw["t"][p]), blockdiag(vw["a"][p] * vw["lmask2"][lv])) for p in range(npair)]
            for vw in views:
                fwd = vw["d"] == 0
                for p in range(npair):
                    t = vw["t"][p]
                    upd = _dot(vw["tl"][p], blockdiag(t))
                    vw["t"][p] = _merge_later_rows(t, _later_rows(t, m, fwd) - upd, m, fwd) if cut else t - upd
        return run

    def solve():
        for vw in views:
            rp = [_dot(blockdiag(vw["t"][p]), vw["rhs"][p]) for p in range(npair)]
            vw["r"] = [rp[h // 2][(h % 2) * c:(h % 2 + 1) * c] for h in range(nh)]
        for vw in views:
            vw["lhs"] = [_bf(jnp.concatenate([vw["r"][h][:, A_DV:], vw["qe"][h]], axis=0)) for h in range(nh)]

    def recur(pos, is_last):
        def run():
            cur = [vw for vw in views if vw["pos"] == pos]
            for vw in cur:
                d = vw["d"]
                for h in range(nh):
                    if (d, h) not in state:
                        state[(d, h)] = s_ref[d, h]
                vw["ws"] = [_dot(vw["lhs"][h], state[(d, h)]) for h in range(nh)]
            for vw in cur:
                vw["vn"] = [vw["r"][h][:, :A_DV] - vw["ws"][h][:c] for h in range(nh)]
            for vw in cur:
                for p in range(npair):
                    vn2 = _bf(jnp.concatenate([vw["vn"][2 * p], vw["vn"][2 * p + 1]], axis=0))
                    oq = _dot(vw["qk"][p], vn2)
                    for e in range(2):
                        h = 2 * p + e
                        vw["og"][0, vw["rows"], h * A_DV:(h + 1) * A_DV] = vw["ws"][h][c:] + oq[e * c:(e + 1) * c]
            for vw in cur:
                d = vw["d"]
                for h in range(nh):
                    state[(d, h)] = state[(d, h)] * vw["gl"][h] + _dot_tn(vw["kdec"][h], vw["vn"][h])
                    if is_last:
                        s_ref[d, h] = state[(d, h)]
        return run

    return [prep] + [level(lv) for lv in range(LEVELS - 2, -1, -1)] + [solve], recur


def _gla_stages(views, st_ref, state):
    c = CHUNK
    nh = B_HEADS
    lane_q = lax.broadcasted_iota(jnp.int32, (1, BQ_W), 1) // B_DK
    row = lax.broadcasted_iota(jnp.int32, (c, 1), 0)

    def stack_heads(x):
        return jnp.concatenate([jnp.where(lane_q == h, x, 0) for h in range(nh)], axis=0)

    def level_abs(vw, lv):
        m = c >> (lv + 1)
        fwd = vw["d"] == 0
        bc, na = vw["bc"], vw["na"]
        if m >= 4:
            refs = [s + (m - 1 if fwd else m) for s in range(0, c, 2 * m)]
            ref_rows = jnp.concatenate([jnp.broadcast_to(bc[r:r + 1, :], (2 * m, BQ_W)) for r in refs], axis=0)
            return jnp.abs(bc - ref_rows)
        r = row % (2 * m)
        up = pltpu.roll(na, c - 1, 0)
        dn = pltpu.roll(na, 1, 0)
        if m == 2:
            if fwd:
                return jnp.where(r == 0, up, jnp.where(r == 1, 0.0, jnp.where(r == 2, na, na + dn)))
            return jnp.where(r == 0, na + up, jnp.where(r == 1, na, jnp.where(r == 2, 0.0, dn)))
        return jnp.where(r == (1 if fwd else 0), na, 0.0)

    def prep():
        for vw in views:
            vw["bc"] = _dot_exact_lhs(vw["tri"], vw["gk"])
            vw["na"] = -vw["gk"]
            vw["sacc"] = [jnp.zeros((c, c), F32) for _ in range(nh)]

    def level(lv):
        def run():
            for vw in views:
                m = c >> (lv + 1)
                fwd = vw["d"] == 0
                if lv < LEVELS:
                    wgt = jnp.exp(-level_abs(vw, lv))
                    ql, kl = _bf(vw["bq"] * wgt), _bf(vw["bk"] * wgt)
                else:
                    ql, kl = _bf(vw["bq"]), _bf(vw["bk"])
                if lv < LEVELS and m % SUBLANES == 0:
                    sl = _dot_nt(stack_heads(_later_rows(ql, m, fwd)), kl)
                    mask = _later_rows(vw["lmaskv"][lv, :c, :], m, fwd)
                    hc = c // 2
                    for h in range(nh):
                        old = vw["sacc"][h]
                        new = _later_rows(old, m, fwd) + sl[h * hc:(h + 1) * hc] * mask
                        vw["sacc"][h] = _merge_later_rows(old, new, m, fwd)
                else:
                    sl = _dot_nt(stack_heads(ql), kl) * vw["lmaskv"][lv]
                    vw["sacc"] = [vw["sacc"][h] + sl[h * c:(h + 1) * c] for h in range(nh)]
        return run

    def intra():
        for vw in views:
            last = c - 1 if vw["d"] == 0 else 0
            bc = vw["bc"]
            bct = bc[last:last + 1]
            vw["bvh"] = [_bf(vw["bv"][:, h * B_DV:(h + 1) * B_DV]) for h in range(nh)]
            vw["oi"] = [_dot(vw["sacc"][h], vw["bvh"][h]) for h in range(nh)]
            vw["qd"] = stack_heads(_bf(vw["bq"] * jnp.exp(bc)))
            kd = vw["bk"] * jnp.exp(bct - bc)
            kdg = jnp.concatenate([kd, jnp.broadcast_to(jnp.exp(bct), (8, BQ_W))], axis=0).T
            vw["kdT"] = _bf(kdg[:, :c])
            vw["lgl"] = kdg[:, c:c + 1]

    def recur(pos, is_last):
        def run():
            for vw in views:
                if vw["pos"] != pos:
                    continue
                d = vw["d"]
                if d not in state:
                    state[d] = st_ref[d]
                oint = _dot(vw["qd"], state[d])
                for h in range(nh):
                    vw["ol"][0, vw["rows"], h * B_DV:(h + 1) * B_DV] = vw["oi"][h] + oint[h * c:(h + 1) * c]
                upd = jnp.concatenate([_dot(vw["kdT"][h * B_DK:(h + 1) * B_DK], vw["bvh"][h]) for h in range(nh)],
                                      axis=0)
                state[d] = state[d] * vw["lgl"] + upd
                if is_last:
                    st_ref[d] = state[d]
        return run

    return [prep] + [level(lv) for lv in range(LEVELS + 1)] + [intra], recur


def _scan_kernel(*refs):
    n_in = 9
    n_c = 6
    og_f, og_b, ol_f, ol_b = refs[2 * n_in + 2 * n_c:2 * n_in + 2 * n_c + 4]
    s_ref, st_ref = refs[2 * n_in + 2 * n_c + 4:]
    c = CHUNK

    @pl.when(pl.program_id(1) == 0)
    def _():
        s_ref[...] = jnp.zeros_like(s_ref)
        st_ref[...] = jnp.zeros_like(st_ref)

    views = []
    for d, (og, ol) in enumerate(((og_f, ol_f), (og_b, ol_b))):
        aq, ak, av, sm, smT, bq, bk, bv, gk = refs[d * n_in:(d + 1) * n_in]
        tri, tri2, lmask2, lmaskv, incl2, strict2 = refs[2 * n_in + d * n_c:2 * n_in + (d + 1) * n_c]
        order = range(STEP_CHUNKS) if d == 0 else range(STEP_CHUNKS - 1, -1, -1)
        for pos, j in enumerate(order):
            rows = pl.ds(j * c, c)
            views.append(dict(d=d, pos=pos, rows=rows, aq=aq[0, rows, :], ak=ak[0, rows, :], av=av[0, rows, :],
                              sm=sm[0, rows, :], smT=smT[j], bq=bq[0, rows, :], bk=bk[0, rows, :],
                              bv=bv[0, rows, :], gk=gk[0, rows, :], tri=tri[...], tri2=tri2[...], lmask2=lmask2, lmaskv=lmaskv,
                              incl2=incl2[...], strict2=strict2[...], og=og, ol=ol))
    gdn, gdn_recur = _gdn_stages(views, s_ref, {})
    gla, gla_recur = _gla_stages(views, st_ref, {})
    for i in range(max(len(gdn), len(gla))):
        if i < len(gdn):
            gdn[i]()
        if i < len(gla):
            gla[i]()
    for pos in range(STEP_CHUNKS):
        gdn_recur(pos, pos == STEP_CHUNKS - 1)()
        gla_recur(pos, pos == STEP_CHUNKS - 1)()


def _scans(aq, ak, av, sm, smT, bq, bk, bv, gk, ctx_len):
    bsz, ttot, _ = aq.shape
    c = CHUNK
    tb = STEP_CHUNKS * c
    nst = ttot // tb
    ncs = ctx_len // tb
    consts = _scan_consts()

    def fidx(b, s):
        return s

    def bidx(b, s):
        return jnp.where(s < ncs, ncs - 1 - s, nst - 1 - s + ncs)

    def tok(w, idx, lane_blk=0):
        return pl.BlockSpec((1, tb, w), lambda b, s: (b, idx(b, s), lane_blk))

    def dir_specs(idx, d):
        return [tok(A_W, idx), tok(A_W, idx), tok(A_W, idx), tok(SMALL_W, idx),
                pl.BlockSpec((None, STEP_CHUNKS, 16, c), lambda b, s: (b, idx(b, s), 0, 0)),
                tok(BQ_W, idx), tok(BQ_W, idx), tok(BV_W, idx), tok(BQ_W, idx, d)]

    const = lambda a: pl.BlockSpec(a.shape, lambda b, s: tuple(0 for _ in a.shape))
    cargs = []
    for d in range(NDIR):
        cd = consts[d]
        cargs += [cd["tri"], cd["tri2"], cd["lmask2"], cd["lmaskv"], cd["incl2"], cd["strict2"]]
    ins = [aq, ak, av, sm, smT, bq, bk, bv, gk]
    out_w = [A_HEADS * A_DV, A_HEADS * A_DV, BV_W, BV_W]
    return pl.pallas_call(
        _scan_kernel,
        grid=(bsz, nst),
        in_specs=dir_specs(fidx, 0) + dir_specs(bidx, 1) + [const(a) for a in cargs],
        out_specs=[tok(out_w[0], fidx), tok(out_w[1], bidx), tok(out_w[2], fidx), tok(out_w[3], bidx)],
        out_shape=[jax.ShapeDtypeStruct((bsz, ttot, w), F32) for w in out_w],
        scratch_shapes=[pltpu.VMEM((NDIR, A_HEADS, A_DK, A_DV), F32),
                        pltpu.VMEM((NDIR, BQ_W, B_DV), F32)],
        compiler_params=pltpu.CompilerParams(dimension_semantics=("arbitrary", "arbitrary"),
                                             vmem_limit_bytes=VMEM_LIMIT),
        name="scan",
    )(*ins, *ins, *cargs)


def _head_norm(o, w, z, n_heads, dv):
    parts = []
    for h in range(n_heads):
        oh = o[:, h * dv:(h + 1) * dv]
        parts.append(oh * lax.rsqrt(jnp.mean(oh * oh, axis=-1, keepdims=True) + EPS))
    return jnp.concatenate(parts, axis=-1) * w * _silu(z)


def _rms(x, w):
    return x * lax.rsqrt(jnp.mean(x * x, axis=-1, keepdims=True) + EPS) * w


def _post_kernel(*refs):
    ns = POST_SUBTILES
    x_ref = refs[0]
    sub = [refs[1 + 6 * s:7 + 6 * s] for s in range(ns)]
    (g1_ref, sh2_ref, sc2_ref, g2_ref, gn_ref, ln_ref, wo_ref, fn_ref,
     wg_ref, wu_ref, wd_ref, fin_ref, o_ref) = refs[1 + 6 * ns:]
    tm = TOK_TILE
    for s in range(ns):
        ogf, ogb, olf, olb, az, bg = sub[s]
        rows = pl.ds(s * tm, tm)
        gdn = _head_norm(ogf[0] + ogb[0], gn_ref[...], az[0], A_HEADS, A_DV)
        gla = _head_norm(olf[0] + olb[0], ln_ref[...], bg[0], B_HEADS, B_DV)
        mix = jnp.concatenate([gdn, gla], axis=-1)
        x1 = x_ref[0, rows, :] + g1_ref[...] * _dot(mix, wo_ref[...])
        h2b = _bf(_rms(x1, fn_ref[...]) * (1.0 + sc2_ref[...]) + sh2_ref[...])
        gate = jnp.dot(h2b, wg_ref[...], preferred_element_type=F32)
        up = jnp.dot(h2b, wu_ref[...], preferred_element_type=F32)
        y = _dot(_silu(gate) * up, wd_ref[...])
        x2 = x1 + g2_ref[...] * y
        o_ref[0, rows, :] = _rms(x2, fin_ref[...])


def _post(x, ogf, ogb, olf, olb, az, bg, mod4, gn, ln, wo, fn, wg, wu, wd, fin, ctx_len):
    bsz, seq, d = x.shape
    tm = TOK_TILE
    ns = POST_SUBTILES
    off = ctx_len // tm
    dff = wg.shape[1]

    def tok(w, s):
        return pl.BlockSpec((1, tm, w), lambda b, j: (b, ns * j + s + off, 0))

    def xtok():
        return pl.BlockSpec((1, ns * tm, d), lambda b, j: (b, j, 0))

    def modspec(k):
        return pl.BlockSpec((None, None, 1, d), lambda b, j: (b, k, 0, 0))

    def const(shape, single=True):
        idx = lambda b, j: tuple(0 for _ in shape)
        if single:
            return pl.BlockSpec(shape, idx, pipeline_mode=pl.Buffered(1))
        return pl.BlockSpec(shape, idx)

    widths = [A_W, A_W, BV_W, BV_W, A_W, BV_W]
    return pl.pallas_call(
        _post_kernel,
        grid=(bsz, seq // (ns * tm)),
        in_specs=[xtok()] + [tok(w, s) for s in range(ns) for w in widths] + [
                  modspec(2), modspec(3), modspec(4), modspec(5),
                  const((1, A_W), False), const((1, BV_W), False), const((d, d)), const((1, d), False),
                  const((d, dff)), const((d, dff)), const((dff, d)), const((1, d), False)],
        out_specs=xtok(),
        out_shape=jax.ShapeDtypeStruct((bsz, seq, d), F32),
        compiler_params=pltpu.CompilerParams(dimension_semantics=("arbitrary", "arbitrary"),
                                             vmem_limit_bytes=VMEM_LIMIT),
        name="post",
    )(x, *([ogf, ogb, olf, olb, az, bg] * ns), mod4, mod4, mod4, mod4, gn, ln, wo, fn, wg, wu, wd, fin)


def kernel(x, c, ctx, c_ctx, w_mod, b_mod, attn_norm, w_in, conv_w, a_log, dt_bias, gdn_norm, gla_w2, gla_b,
           gla_norm, w_out, ffn_norm, w_gate, w_up, w_down, final_norm):
    bsz, seq, d = x.shape
    ctx_len = ctx.shape[1]
    assert w_mod.shape[0] == 1, "single-layer block"
    assert ctx_len == TOK_TILE and seq % (TOK_TILE * POST_SUBTILES) == 0
    assert TOK_TILE % GRID_W == 0 and TOK_TILE % (CHUNK * STEP_CHUNKS) == 0

    rows = -(-(bsz + 1) // 8) * 8
    cc = jnp.zeros((rows, d), F32).at[:bsz].set(c).at[rows - 1].set(c_ctx)
    mod = _modulation(cc, w_mod[0], b_mod[0][None, :])
    mod4 = mod.reshape(rows, 6, 1, d)

    w = w_in[0]
    o1 = 4 * A_W
    nb = NDIR * A_HEADS
    o2 = o1 + 2 * nb
    o3 = o2 + 2 * BQ_W + 2 * BV_W
    nlr = NDIR * GLA_RANK
    win = jnp.concatenate([w[:, :o1], w[:, o2:o3], w[:, o1:o2], w[:, o3:o3 + nlr],
                           jnp.zeros((d, SMALL_W - 2 * nb - nlr), F32)], axis=1).astype(BF16)
    gparams = jnp.zeros((2, SMALL_W), F32)
    gparams = gparams.at[0, nb:2 * nb].set(dt_bias[0].reshape(-1)).at[1, nb:2 * nb].set(a_log[0].reshape(-1))
    w2 = jnp.zeros((SMALL_W, NDIR * BQ_W), F32)
    for n in range(NDIR):
        w2 = w2.at[2 * nb + n * GLA_RANK:2 * nb + (n + 1) * GLA_RANK, n * BQ_W:(n + 1) * BQ_W].set(gla_w2[0, n])
    w2h = w2.astype(BF16)
    w2m = (w2 - w2h.astype(F32)).astype(BF16)

    aq, ak, av, az, bq, bk, bv, bg, gk, sm = _projection(
        x, ctx, mod4, attn_norm, win, conv_w[0], gparams, w2h, w2m, gla_b[0].reshape(1, -1))

    ttot = ctx_len + seq
    smT = sm[:, :, :16].reshape(bsz, ttot // CHUNK, CHUNK, 16).transpose(0, 1, 3, 2)
    ogf, ogb, olf, olb = _scans(aq, ak, av, sm, smT, bq, bk, bv, gk, ctx_len)

    tile_w = lambda v, n: jnp.tile(v.reshape(1, -1), (1, n))
    return _post(x, ogf, ogb, olf, olb, az, bg, mod4,
                 tile_w(gdn_norm[0], A_HEADS), tile_w(gla_norm[0], B_HEADS),
                 w_out[0].astype(BF16), ffn_norm, w_gate[0].astype(BF16), w_up[0].astype(BF16),
                 w_down[0].astype(BF16), final_norm.reshape(1, -1), ctx_len)
```

```python
import functools

import numpy as np
import jax
import jax.numpy as jnp
from jax import lax
from jax.experimental import pallas as pl
from jax.experimental.pallas import tpu as pltpu

F32 = jnp.float32
BF16 = jnp.bfloat16

GRID_W = 64
A_HEADS, A_DK, A_DV = 4, 128, 128
B_HEADS, B_DK, B_DV = 4, 64, 128
NDIR = 2
GLA_RANK = 16
GLA_NORMALIZER = 16.0
EPS = 1e-6

CHUNK = 64
STEP_CHUNKS = 4
SUBLANES = 8
TOK_TILE = 256
POST_SUBTILES = 2
SMALL_W = 128
LEVELS = int(np.log2(CHUNK))

A_W = A_HEADS * A_DK
BQ_W = B_HEADS * B_DK
BV_W = B_HEADS * B_DV
P_COLS = 4 * A_W + 2 * BQ_W + 2 * BV_W + SMALL_W

VMEM_LIMIT = 56 * 1024 * 1024


def _bf(x):
    return x.astype(BF16)


def _dot(a, b):
    return jnp.dot(_bf(a), _bf(b), preferred_element_type=F32)


def _dot_nt(a, b):
    return lax.dot_general(_bf(a), _bf(b), (((1,), (1,)), ((), ())), preferred_element_type=F32)


def _dot_tn(a, b):
    return lax.dot_general(_bf(a), _bf(b), (((0,), (0,)), ((), ())), preferred_element_type=F32)


def _split3(x):
    hi = _bf(x)
    r1 = x - hi.astype(F32)
    mid = _bf(r1)
    lo = _bf(r1 - mid.astype(F32))
    return hi, mid, lo


def _dot_exact_lhs(m, x, terms=3):
    d = lambda p: jnp.dot(m, p, preferred_element_type=F32)
    parts = _split3(x)[:terms]
    acc = d(parts[0])
    for p in parts[1:]:
        acc = acc + d(p)
    return acc


def _dot_exact_rhs_nt(x, m):
    hi, mid, lo = _split3(x)
    d = lambda p: lax.dot_general(p, m, (((1,), (1,)), ((), ())), preferred_element_type=F32)
    return d(hi) + d(mid) + d(lo)


def _sigmoid(x):
    return 0.5 * jnp.tanh(0.5 * x) + 0.5


def _silu(x):
    h = 0.5 * x
    return h + h * jnp.tanh(h)


def _softplus(x):
    return jnp.maximum(x, 0.0) + jnp.log(1.0 + jnp.exp(-jnp.abs(x)))


def _later_rows(x, m, fwd):
    n = x.shape[0]
    return jnp.concatenate([x[s + m:s + 2 * m] if fwd else x[s:s + m] for s in range(0, n, 2 * m)], axis=0)


def _merge_later_rows(full, later, m, fwd):
    n = full.shape[0]
    pieces = []
    for i, s in enumerate(range(0, n, 2 * m)):
        lat = later[i * m:(i + 1) * m]
        pieces += [full[s:s + m], lat] if fwd else [lat, full[s + m:s + 2 * m]]
    return jnp.concatenate(pieces, axis=0)


def _mod_kernel(cc_ref, w_ref, b_ref, o_ref):
    o_ref[...] = _dot(_silu(cc_ref[...]), w_ref[...]) + b_ref[...]


def _modulation(cc, w_mod, b_mod):
    rows, d = cc.shape
    n = w_mod.shape[1]
    tn = 1536
    return pl.pallas_call(
        _mod_kernel,
        grid=(n // tn,),
        in_specs=[pl.BlockSpec((rows, d), lambda j: (0, 0)),
                  pl.BlockSpec((d, tn), lambda j: (0, j)),
                  pl.BlockSpec((1, tn), lambda j: (0, j))],
        out_specs=pl.BlockSpec((rows, tn), lambda j: (0, j)),
        out_shape=jax.ShapeDtypeStruct((rows, n), F32),
        compiler_params=pltpu.CompilerParams(dimension_semantics=("arbitrary",),
                                             vmem_limit_bytes=VMEM_LIMIT),
        name="mod",
    )(cc, w_mod, b_mod)


def _proj_kernel(n_ctx_tiles, ctx_len,
                 x_ref, ctx_ref, sh_ref, sc_ref, nw_ref, win_ref, cw_ref, gp_ref, w2h_ref, w2m_ref, gb_ref,
                 aq_ref, ak_ref, av_ref, az_ref, bq_ref, bk_ref, bv_ref, bg_ref, gk_ref, sm_ref):
    j = pl.program_id(1)
    is_ctx = j < n_ctx_tiles
    xin = jnp.where(is_ctx, ctx_ref[0], x_ref[0])
    ms = jnp.mean(xin * xin, axis=-1, keepdims=True)
    h = xin * lax.rsqrt(ms + EPS) * (nw_ref[...] * (1.0 + sc_ref[...])) + sh_ref[...]
    hb = _bf(h)
    tm = xin.shape[0]
    sub = 8
    sublane = lax.broadcasted_iota(jnp.int32, (1, sub, 1), 1)
    cw = cw_ref[...]

    def mm(c0, c1):
        return jnp.dot(hb, win_ref[:, c0:c1], preferred_element_type=F32)

    def conv_silu(u, c0):
        wu = u.shape[1]
        u3 = u.reshape(tm // sub, sub, wu)

        def shifted(down):
            rot = pltpu.roll(u3, 1 if down else sub - 1, 1)
            nt = tm // sub
            tiles_per_row = GRID_W // sub
            zero = jnp.zeros((1, sub, wu), F32)
            nbr = []
            for r in range(nt):
                src = r - 1 if down else r + 1
                row_start = (r if down else src) % tiles_per_row == 0
                if src < 0 or src >= nt:
                    nbr.append(zero)
                elif row_start:
                    nbr.append(jnp.where(is_ctx, rot[src:src + 1], 0.0))
                else:
                    nbr.append(rot[src:src + 1])
            edge = sublane == (0 if down else sub - 1)
            return jnp.where(edge, jnp.concatenate(nbr, axis=0), rot).reshape(tm, wu)

        w = cw[:, c0:c0 + wu]
        return _silu(w[0:1] * shifted(True) + w[1:2] * u + w[2:3] * shifted(False))

    def l2n(t, scale):
        parts = []
        for hd in range(A_HEADS):
            th = t[:, hd * A_DK:(hd + 1) * A_DK]
            parts.append(th * (lax.rsqrt(jnp.sum(th * th, axis=-1, keepdims=True) + EPS) * scale))
        return jnp.concatenate(parts, axis=-1)

    uq, uk, uv = mm(0, A_W), mm(A_W, 2 * A_W), mm(2 * A_W, 3 * A_W)
    o = 3 * A_W
    aq_ref[0] = l2n(conv_silu(uq, 0), A_DK ** -0.5)
    p1 = mm(o, o + A_W + 2 * BQ_W)
    az_ref[0] = p1[:, :A_W]
    bq_ref[0] = p1[:, A_W:A_W + BQ_W] * (B_DK ** -0.5)
    bk_ref[0] = p1[:, A_W + BQ_W:]
    o += A_W + 2 * BQ_W
    ak_ref[0] = l2n(conv_silu(uk, A_W), 1.0)
    bv_ref[0] = mm(o, o + BV_W)
    o += BV_W
    av_ref[0] = conv_silu(uv, 2 * A_W)
    p3 = mm(o, P_COLS)
    bg_ref[0] = p3[:, :BV_W]

    ps = p3[:, BV_W:]
    gp = gp_ref[...]
    lane = lax.broadcasted_iota(jnp.int32, (1, SMALL_W), 1)
    beta = _sigmoid(ps)
    g = -jnp.exp(gp[1:2]) * _softplus(ps + gp[0:1])
    nb = NDIR * A_HEADS
    sm_ref[0] = jnp.where(lane < nb, beta, jnp.where(lane < 2 * nb, g, 0.0))

    ph = _bf(ps)
    pm = _bf(ps - ph.astype(F32))
    d = lambda a, b: jnp.dot(a, b, preferred_element_type=F32)
    pre = d(ph, w2h_ref[...]) + d(ph, w2m_ref[...]) + d(pm, w2h_ref[...]) + gb_ref[...]
    gk_ref[0] = -_softplus(-pre) * (1.0 / GLA_NORMALIZER)


def _projection(x, ctx, mod4, attn_norm, win, conv_w, gparams, w2h, w2m, gla_b):
    bsz, seq, d = x.shape
    ctx_len = ctx.shape[1]
    tm = TOK_TILE
    nct = ctx_len // tm
    nxt = seq // tm
    ttot = ctx_len + seq
    mod_rows = mod4.shape[0]

    def tok(w):
        return pl.BlockSpec((1, tm, w), lambda b, j: (b, j, 0))

    const = lambda shape: pl.BlockSpec(shape, lambda b, j: tuple(0 for _ in shape))
    widths = [A_W, A_W, A_W, A_W, BQ_W, BQ_W, BV_W, BV_W, NDIR * BQ_W, SMALL_W]
    return pl.pallas_call(
        functools.partial(_proj_kernel, nct, ctx_len),
        grid=(bsz, nct + nxt),
        in_specs=[
            pl.BlockSpec((1, tm, d), lambda b, j: (b, jnp.maximum(j - nct, 0), 0)),
            pl.BlockSpec((1, tm, d), lambda b, j: (b, jnp.minimum(j, nct - 1), 0)),
            pl.BlockSpec((None, None, 1, d), lambda b, j: (jnp.where(j < nct, mod_rows - 1, b), 0, 0, 0)),
            pl.BlockSpec((None, None, 1, d), lambda b, j: (jnp.where(j < nct, mod_rows - 1, b), 1, 0, 0)),
            const((1, d)), const((d, P_COLS)), const(conv_w.shape), const(gparams.shape),
            const(w2h.shape), const(w2m.shape), const(gla_b.shape),
        ],
        out_specs=[tok(w) for w in widths],
        out_shape=[jax.ShapeDtypeStruct((bsz, ttot, w), F32) for w in widths],
        compiler_params=pltpu.CompilerParams(dimension_semantics=("arbitrary", "arbitrary"),
                                             vmem_limit_bytes=VMEM_LIMIT),
        name="proj",
    )(x, ctx, mod4, mod4, attn_norm, win, conv_w, gparams, w2h, w2m, gla_b)


def _scan_consts():
    c = CHUNK
    i = np.arange(c)[:, None]
    t = np.arange(c)[None, :]
    out = []
    for d in range(NDIR):
        incl = (t <= i) if d == 0 else (t >= i)
        strict = (t < i) if d == 0 else (t > i)
        tri = incl.astype(np.float32)
        masks = []
        for lv in range(LEVELS):
            m = c >> (lv + 1)
            start = (np.arange(c) // (2 * m)) * (2 * m)
            later = (np.arange(c) % (2 * m) >= m) if d == 0 else (np.arange(c) % (2 * m) < m)
            same = (start[:, None] == start[None, :])
            masks.append((same & later[:, None] & ~later[None, :]).astype(np.float32))
        masks.append(np.eye(c, dtype=np.float32))
        tile2 = lambda mk: np.tile(mk, (1, 2))
        out.append(dict(
            tri=jnp.asarray(tri, BF16),
            tri2=jnp.asarray(np.tile(tri, (2, 1)), BF16),
            lmask2=jnp.asarray(np.stack([tile2(mk) for mk in masks]), F32),
            lmaskv=jnp.asarray(np.stack([np.tile(mk, (B_HEADS, 1)) for mk in masks]), F32),
            incl2=jnp.asarray(tile2(incl.astype(np.float32))),
            strict2=jnp.asarray(tile2(strict.astype(np.float32))),
        ))
    return out


def _gdn_stages(views, s_ref, state, need_out):
    c = CHUNK
    nh = A_HEADS
    npair = nh // 2
    w2 = 2 * c
    first = lax.broadcasted_iota(jnp.int32, (1, w2), 1) < c
    first_k = lax.broadcasted_iota(jnp.int32, (1, 2 * A_DK), 1) < A_DK
    lane_k = lax.broadcasted_iota(jnp.int32, (1, A_W), 1) // A_DK
    km = [lane_k == h for h in range(nh)]

    def expand(cols):
        out = cols[nh - 1]
        for h in range(nh - 2, -1, -1):
            out = jnp.where(km[h], cols[h], out)
        return out

    def blockdiag(x):
        xb = _bf(x)
        return jnp.concatenate([jnp.where(first, xb, 0), jnp.where(first, 0, xb)], axis=0)

    def heads(x, w):
        return [x[:, h * w:(h + 1) * w] for h in range(nh)]

    def prep():
        for vw in views:
            d = vw["d"]
            last = c - 1 if d == 0 else 0
            base = NDIR * nh + nh * d
            sm = vw["sm"]
            gc_all = _dot_exact_lhs(vw["tri"], sm)
            gcr_all = _dot_exact_rhs_nt(vw["smT"], vw["tri2"])
            gcc = [gc_all[:, base + h:base + h + 1] for h in range(nh)]
            gtot = [gc_all[last:last + 1, base + h:base + h + 1] for h in range(nh)]
            beta = [sm[:, nh * d + h:nh * d + h + 1] for h in range(nh)]
            egc = [jnp.exp(g) for g in gcc]
            vw["gl"] = [jnp.exp(g) for g in gtot]
            inc = vw["incl2"] > 0.5
            ak, aq = vw["ak"], vw["aq"]
            beta_k = expand(beta)
            kb = ak * beta_k
            egc_k = expand(egc)
            kbq = _bf(jnp.concatenate([kb, aq], axis=0) if need_out else kb)
            akb = _bf(ak)
            vw["decay"], vw["kbq"], vw["kst"] = [], [], []
            for p in range(npair):
                h0, h1 = 2 * p, 2 * p + 1
                diff = (jnp.where(first, gcc[h0], gcc[h1])
                        - jnp.where(first, gcr_all[base + h0:base + h0 + 1, :], gcr_all[base + h1:base + h1 + 1, :]))
                vw["decay"].append(jnp.where(inc, jnp.exp(jnp.where(inc, diff, 0.0)), 0.0))
                kp = akb[:, 2 * A_DK * p:2 * A_DK * (p + 1)]
                vw["kst"].append(jnp.concatenate([jnp.where(first_k, kp, 0), jnp.where(first_k, 0, kp)], axis=0))
                vw["kbq"].append(kbq[:, 2 * A_DK * p:2 * A_DK * (p + 1)])
            vb = heads(vw["av"] * beta_k, A_DV)
            kbe = heads(kb * egc_k, A_DK)
            rhs = [jnp.concatenate([vb[h], kbe[h]], axis=1) for h in range(nh)]
            vw["rhs"] = [_bf(jnp.concatenate([rhs[2 * p], rhs[2 * p + 1]], axis=0)) for p in range(npair)]
            vw["qe"] = heads(aq * egc_k, A_DK) if need_out else None
            vw["kdec"] = heads(_bf(ak * expand([jnp.exp(gtot[h] - gcc[h]) for h in range(nh)])), A_DK)
        for vw in views:
            vw["m"] = [_dot_nt(vw["kbq"][p], vw["kst"][p]) for p in range(npair)]
        for vw in views:
            lm = vw["lmask2"]
            vw["a"] = [vw["m"][p][:c] * vw["decay"][p] * vw["strict2"] for p in range(npair)]
            if need_out:
                vw["qk"] = [blockdiag(vw["m"][p][c:] * vw["decay"][p]) for p in range(npair)]
            vw["t"] = [lm[LEVELS] - vw["a"][p] * lm[LEVELS - 1] for p in range(npair)]

    def level(lv):
        m = c >> (lv + 1)
        cut = m % SUBLANES == 0

        def run():
            for vw in views:
                fwd = vw["d"] == 0
                rows = (lambda x: _later_rows(x, m, fwd)) if cut else (lambda x: x)
                vw["tl"] = [_dot(rows(vw["t"][p]), blockdiag(vw["a"][p] * vw["lmask2"][lv])) for p in range(npair)]
            for vw in views:
                fwd = vw["d"] == 0
                for p in range(npair):
                    t = vw["t"][p]
                    upd = _dot(vw["tl"][p], blockdiag(t))
                    vw["t"][p] = _merge_later_rows(t, _later_rows(t, m, fwd) - upd, m, fwd) if cut else t - upd
        return run

    def solve():
        for vw in views:
            rp = [_dot(blockdiag(vw["t"][p]), vw["rhs"][p]) for p in range(npair)]
            vw["r"] = [rp[h // 2][(h % 2) * c:(h % 2 + 1) * c] for h in range(nh)]
        for vw in views:
            vw["lhs"] = [_bf(jnp.concatenate([vw["r"][h][:, A_DV:], vw["qe"][h]], axis=0) if need_out
                             else vw["r"][h][:, A_DV:]) for h in range(nh)]

    def recur(pos, is_last):
        def run():
            cur = [vw for vw in views if vw["pos"] == pos]
            for vw in cur:
                d = vw["d"]
                for h in range(nh):
                    if (d, h) not in state:
                        state[(d, h)] = s_ref[d, h]
                vw["ws"] = [_dot(vw["lhs"][h], state[(d, h)]) for h in range(nh)]
            for vw in cur:
                vw["vn"] = [vw["r"][h][:, :A_DV] - vw["ws"][h][:c] for h in range(nh)]
            for vw in cur if need_out else []:
                for p in range(npair):
                    vn2 = _bf(jnp.concatenate([vw["vn"][2 * p], vw["vn"][2 * p + 1]], axis=0))
                    oq = _dot(vw["qk"][p], vn2)
                    for e in range(2):
                        h = 2 * p + e
                        vw["og"][0, vw["rows"], h * A_DV:(h + 1) * A_DV] = vw["ws"][h][c:] + oq[e * c:(e + 1) * c]
            for vw in cur:
                d = vw["d"]
                for h in range(nh):
                    state[(d, h)] = state[(d, h)] * vw["gl"][h] + _dot_tn(vw["kdec"][h], vw["vn"][h])
                    if is_last:
                        s_ref[d, h] = state[(d, h)]
        return run

    return [prep] + [level(lv) for lv in range(LEVELS - 2, -1, -1)] + [solve], recur


def _gla_stages(views, st_ref, state, need_out):
    c = CHUNK
    nh = B_HEADS
    lane_q = lax.broadcasted_iota(jnp.int32, (1, BQ_W), 1) // B_DK
    row = lax.broadcasted_iota(jnp.int32, (c, 1), 0)

    def stack_heads(x):
        return jnp.concatenate([jnp.where(lane_q == h, x, 0) for h in range(nh)], axis=0)

    def level_abs(vw, lv):
        m = c >> (lv + 1)
        fwd = vw["d"] == 0
        bc, na = vw["bc"], vw["na"]
        if m >= 4:
            refs = [s + (m - 1 if fwd else m) for s in range(0, c, 2 * m)]
            ref_rows = jnp.concatenate([jnp.broadcast_to(bc[r:r + 1, :], (2 * m, BQ_W)) for r in refs], axis=0)
            return jnp.abs(bc - ref_rows)
        r = row % (2 * m)
        up = pltpu.roll(na, c - 1, 0)
        dn = pltpu.roll(na, 1, 0)
        if m == 2:
            if fwd:
                return jnp.where(r == 0, up, jnp.where(r == 1, 0.0, jnp.where(r == 2, na, na + dn)))
            return jnp.where(r == 0, na + up, jnp.where(r == 1, na, jnp.where(r == 2, 0.0, dn)))
        return jnp.where(r == (1 if fwd else 0), na, 0.0)

    def prep():
        for vw in views:
            vw["bc"] = _dot_exact_lhs(vw["tri"], vw["gk"])
            vw["na"] = -vw["gk"]
            vw["sacc"] = [jnp.zeros((c, c), F32) for _ in range(nh)]

    def level(lv):
        def run():
            for vw in views:
                m = c >> (lv + 1)
                fwd = vw["d"] == 0
                if lv < LEVELS:
                    wgt = jnp.exp(-level_abs(vw, lv))
                    ql, kl = _bf(vw["bq"] * wgt), _bf(vw["bk"] * wgt)
                else:
                    ql, kl = _bf(vw["bq"]), _bf(vw["bk"])
                if lv < LEVELS and m % SUBLANES == 0:
                    sl = _dot_nt(stack_heads(_later_rows(ql, m, fwd)), kl)
                    mask = _later_rows(vw["lmaskv"][lv, :c, :], m, fwd)
                    hc = c // 2
                    for h in range(nh):
                        old = vw["sacc"][h]
                        new = _later_rows(old, m, fwd) + sl[h * hc:(h + 1) * hc] * mask
                        vw["sacc"][h] = _merge_later_rows(old, new, m, fwd)
                else:
                    sl = _dot_nt(stack_heads(ql), kl) * vw["lmaskv"][lv]
                    vw["sacc"] = [vw["sacc"][h] + sl[h * c:(h + 1) * c] for h in range(nh)]
        return run

    def intra():
        for vw in views:
            last = c - 1 if vw["d"] == 0 else 0
            bc = vw["bc"]
            bct = bc[last:last + 1]
            vw["bvh"] = [_bf(vw["bv"][:, h * B_DV:(h + 1) * B_DV]) for h in range(nh)]
            if need_out:
                vw["oi"] = [_dot(vw["sacc"][h], vw["bvh"][h]) for h in range(nh)]
                vw["qd"] = stack_heads(_bf(vw["bq"] * jnp.exp(bc)))
            kd = vw["bk"] * jnp.exp(bct - bc)
            kdg = jnp.concatenate([kd, jnp.broadcast_to(jnp.exp(bct), (8, BQ_W))], axis=0).T
            vw["kdT"] = _bf(kdg[:, :c])
            vw["lgl"] = kdg[:, c:c + 1]

    def recur(pos, is_last):
        def run():
            for vw in views:
                if vw["pos"] != pos:
                    continue
                d = vw["d"]
                if d not in state:
                    state[d] = st_ref[d]
                if need_out:
                    oint = _dot(vw["qd"], state[d])
                    for h in range(nh):
                        vw["ol"][0, vw["rows"], h * B_DV:(h + 1) * B_DV] = vw["oi"][h] + oint[h * c:(h + 1) * c]
                upd = jnp.concatenate([_dot(vw["kdT"][h * B_DK:(h + 1) * B_DK], vw["bvh"][h]) for h in range(nh)],
                                      axis=0)
                state[d] = state[d] * vw["lgl"] + upd
                if is_last:
                    st_ref[d] = state[d]
        return run

    return [prep] + ([level(lv) for lv in range(LEVELS + 1)] if need_out else []) + [intra], recur


def _scan_kernel(need_out, *refs):
    n_in = 9
    n_c = 6
    k = 2 * n_in + 2 * n_c
    if need_out:
        s_in, st_in, og_f, og_b, ol_f, ol_b, s_ref, st_ref = refs[k:]
    else:
        s_out, st_out, s_ref, st_ref = refs[k:]
        og_f = og_b = ol_f = ol_b = None
    c = CHUNK

    @pl.when(pl.program_id(1) == 0)
    def _():
        if need_out:
            s_ref[...] = s_in[0]
            st_ref[...] = st_in[0]
        else:
            s_ref[...] = jnp.zeros_like(s_ref)
            st_ref[...] = jnp.zeros_like(st_ref)

    views = []
    for d, (og, ol) in enumerate(((og_f, ol_f), (og_b, ol_b))):
        aq, ak, av, sm, smT, bq, bk, bv, gk = refs[d * n_in:(d + 1) * n_in]
        tri, tri2, lmask2, lmaskv, incl2, strict2 = refs[2 * n_in + d * n_c:2 * n_in + (d + 1) * n_c]
        order = range(STEP_CHUNKS) if d == 0 else range(STEP_CHUNKS - 1, -1, -1)
        for pos, j in enumerate(order):
            rows = pl.ds(j * c, c)
            views.append(dict(d=d, pos=pos, rows=rows, aq=aq[0, rows, :], ak=ak[0, rows, :], av=av[0, rows, :],
                              sm=sm[0, rows, :], smT=smT[j], bq=bq[0, rows, :], bk=bk[0, rows, :],
                              bv=bv[0, rows, :], gk=gk[0, rows, :], tri=tri[...], tri2=tri2[...], lmask2=lmask2,
                              lmaskv=lmaskv, incl2=incl2[...], strict2=strict2[...], og=og, ol=ol))
    gdn, gdn_recur = _gdn_stages(views, s_ref, {}, need_out)
    gla, gla_recur = _gla_stages(views, st_ref, {}, need_out)
    for i in range(max(len(gdn), len(gla))):
        if i < len(gdn):
            gdn[i]()
        if i < len(gla):
            gla[i]()
    for pos in range(STEP_CHUNKS):
        gdn_recur(pos, pos == STEP_CHUNKS - 1)()
        gla_recur(pos, pos == STEP_CHUNKS - 1)()

    if not need_out:
        @pl.when(pl.program_id(1) == pl.num_programs(1) - 1)
        def _():
            s_out[0] = s_ref[...]
            st_out[0] = st_ref[...]


def _scans(aq, ak, av, sm, smT, bq, bk, bv, gk, ctx_len):
    bsz, ttot, _ = aq.shape
    c = CHUNK
    tb = STEP_CHUNKS * c
    ncs = ctx_len // tb
    nxs = ttot // tb - ncs
    consts = _scan_consts()
    const = lambda a: pl.BlockSpec(a.shape, lambda b, s: tuple(0 for _ in a.shape))
    cargs = []
    for d in range(NDIR):
        cd = consts[d]
        cargs += [cd["tri"], cd["tri2"], cd["lmask2"], cd["lmaskv"], cd["incl2"], cd["strict2"]]
    ins = [aq, ak, av, sm, smT, bq, bk, bv, gk]
    s_shape = (NDIR, A_HEADS, A_DK, A_DV)
    st_shape = (NDIR, BQ_W, B_DV)
    state_specs = [pl.BlockSpec((1,) + s_shape, lambda b, s: (b, 0, 0, 0, 0)),
                   pl.BlockSpec((1,) + st_shape, lambda b, s: (b, 0, 0, 0))]
    state_shapes = [jax.ShapeDtypeStruct((bsz,) + s_shape, F32), jax.ShapeDtypeStruct((bsz,) + st_shape, F32)]
    scratch = [pltpu.VMEM(s_shape, F32), pltpu.VMEM(st_shape, F32)]
    params = pltpu.CompilerParams(dimension_semantics=("arbitrary", "arbitrary"), vmem_limit_bytes=VMEM_LIMIT)

    def in_specs(first, n):
        def tok(w, idx, lane_blk=0):
            return pl.BlockSpec((1, tb, w), lambda b, s: (b, idx(s), lane_blk))

        def dir_specs(idx, d):
            return [tok(A_W, idx), tok(A_W, idx), tok(A_W, idx), tok(SMALL_W, idx),
                    pl.BlockSpec((None, STEP_CHUNKS, 16, c), lambda b, s: (b, idx(s), 0, 0)),
                    tok(BQ_W, idx), tok(BQ_W, idx), tok(BV_W, idx), tok(BQ_W, idx, d)]

        return dir_specs(lambda s: first + s, 0) + dir_specs(lambda s: first + n - 1 - s, 1) + [const(a) for a in cargs]

    s_ctx, st_ctx = pl.pallas_call(
        functools.partial(_scan_kernel, False),
        grid=(bsz, ncs),
        in_specs=in_specs(0, ncs),
        out_specs=state_specs,
        out_shape=state_shapes,
        scratch_shapes=scratch,
        compiler_params=params,
        name="scan_ctx",
    )(*ins, *ins, *cargs)

    def otok(w, idx):
        return pl.BlockSpec((1, tb, w), lambda b, s: (b, idx(s), 0))

    out_w = [A_HEADS * A_DV, A_HEADS * A_DV, BV_W, BV_W]
    fwd, bwd = (lambda s: s), (lambda s: nxs - 1 - s)
    return pl.pallas_call(
        functools.partial(_scan_kernel, True),
        grid=(bsz, nxs),
        in_specs=in_specs(ncs, nxs) + state_specs,
        out_specs=[otok(out_w[0], fwd), otok(out_w[1], bwd), otok(out_w[2], fwd), otok(out_w[3], bwd)],
        out_shape=[jax.ShapeDtypeStruct((bsz, nxs * tb, w), F32) for w in out_w],
        scratch_shapes=scratch,
        compiler_params=params,
        name="scan",
    )(*ins, *ins, *cargs, s_ctx, st_ctx)


def _head_norm(o, w, z, n_heads, dv):
    parts = []
    for h in range(n_heads):
        oh = o[:, h * dv:(h + 1) * dv]
        parts.append(oh * lax.rsqrt(jnp.mean(oh * oh, axis=-1, keepdims=True) + EPS))
    return jnp.concatenate(parts, axis=-1) * w * _silu(z)


def _rms(x, w):
    return x * lax.rsqrt(jnp.mean(x * x, axis=-1, keepdims=True) + EPS) * w


def _post_kernel(*refs):
    ns = POST_SUBTILES
    x_ref = refs[0]
    sub = [refs[1 + 6 * s:7 + 6 * s] for s in range(ns)]
    (g1_ref, sh2_ref, sc2_ref, g2_ref, gn_ref, ln_ref, wo_ref, fn_ref,
     wg_ref, wu_ref, wd_ref, fin_ref, o_ref) = refs[1 + 6 * ns:]
    tm = TOK_TILE
    for s in range(ns):
        ogf, ogb, olf, olb, az, bg = sub[s]
        rows = pl.ds(s * tm, tm)
        gdn = _head_norm(ogf[0] + ogb[0], gn_ref[...], az[0], A_HEADS, A_DV)
        gla = _head_norm(olf[0] + olb[0], ln_ref[...], bg[0], B_HEADS, B_DV)
        mix = jnp.concatenate([gdn, gla], axis=-1)
        x1 = x_ref[0, rows, :] + g1_ref[...] * _dot(mix, wo_ref[...])
        h2b = _bf(_rms(x1, fn_ref[...]) * (1.0 + sc2_ref[...]) + sh2_ref[...])
        gate = jnp.dot(h2b, wg_ref[...], preferred_element_type=F32)
        up = jnp.dot(h2b, wu_ref[...], preferred_element_type=F32)
        y = _dot(_silu(gate) * up, wd_ref[...])
        x2 = x1 + g2_ref[...] * y
        o_ref[0, rows, :] = _rms(x2, fin_ref[...])


def _post(x, ogf, ogb, olf, olb, az, bg, mod4, gn, ln, wo, fn, wg, wu, wd, fin, ctx_len):
    bsz, seq, d = x.shape
    tm = TOK_TILE
    ns = POST_SUBTILES
    off = ctx_len // tm
    dff = wg.shape[1]

    def tok(w, o, s):
        return pl.BlockSpec((1, tm, w), lambda b, j: (b, ns * j + s + o, 0))

    def xtok():
        return pl.BlockSpec((1, ns * tm, d), lambda b, j: (b, j, 0))

    def modspec(k):
        return pl.BlockSpec((None, None, 1, d), lambda b, j: (b, k, 0, 0))

    def const(shape, single=True):
        idx = lambda b, j: tuple(0 for _ in shape)
        if single:
            return pl.BlockSpec(shape, idx, pipeline_mode=pl.Buffered(1))
        return pl.BlockSpec(shape, idx)

    widths = [(A_W, 0), (A_W, 0), (BV_W, 0), (BV_W, 0), (A_W, off), (BV_W, off)]
    return pl.pallas_call(
        _post_kernel,
        grid=(bsz, seq // (ns * tm)),
        in_specs=[xtok()] + [tok(w, o, s) for s in range(ns) for w, o in widths] + [
                  modspec(2), modspec(3), modspec(4), modspec(5),
                  const((1, A_W), False), const((1, BV_W), False), const((d, d)), const((1, d), False),
                  const((d, dff)), const((d, dff)), const((dff, d)), const((1, d), False)],
        out_specs=xtok(),
        out_shape=jax.ShapeDtypeStruct((bsz, seq, d), F32),
        compiler_params=pltpu.CompilerParams(dimension_semantics=("arbitrary", "arbitrary"),
                                             vmem_limit_bytes=VMEM_LIMIT),
        name="post",
    )(x, *([ogf, ogb, olf, olb, az, bg] * ns), mod4, mod4, mod4, mod4, gn, ln, wo, fn, wg, wu, wd, fin)


def kernel(x, c, ctx, c_ctx, w_mod, b_mod, attn_norm, w_in, conv_w, a_log, dt_bias, gdn_norm, gla_w2, gla_b,
           gla_norm, w_out, ffn_norm, w_gate, w_up, w_down, final_norm):
    bsz, seq, d = x.shape
    ctx_len = ctx.shape[1]
    assert w_mod.shape[0] == 1, "single-layer block"
    assert ctx_len == TOK_TILE and seq % (TOK_TILE * POST_SUBTILES) == 0
    assert TOK_TILE % GRID_W == 0 and TOK_TILE % (CHUNK * STEP_CHUNKS) == 0

    rows = -(-(bsz + 1) // 8) * 8
    cc = jnp.zeros((rows, d), F32).at[:bsz].set(c).at[rows - 1].set(c_ctx)
    mod = _modulation(cc, w_mod[0], b_mod[0][None, :])
    mod4 = mod.reshape(rows, 6, 1, d)

    w = w_in[0]
    o1 = 4 * A_W
    nb = NDIR * A_HEADS
    o2 = o1 + 2 * nb
    o3 = o2 + 2 * BQ_W + 2 * BV_W
    nlr = NDIR * GLA_RANK
    win = jnp.concatenate([w[:, :o1], w[:, o2:o3], w[:, o1:o2], w[:, o3:o3 + nlr],
                           jnp.zeros((d, SMALL_W - 2 * nb - nlr), F32)], axis=1).astype(BF16)
    gparams = jnp.zeros((2, SMALL_W), F32)
    gparams = gparams.at[0, nb:2 * nb].set(dt_bias[0].reshape(-1)).at[1, nb:2 * nb].set(a_log[0].reshape(-1))
    w2 = jnp.zeros((SMALL_W, NDIR * BQ_W), F32)
    for n in range(NDIR):
        w2 = w2.at[2 * nb + n * GLA_RANK:2 * nb + (n + 1) * GLA_RANK, n * BQ_W:(n + 1) * BQ_W].set(gla_w2[0, n])
    w2h = w2.astype(BF16)
    w2m = (w2 - w2h.astype(F32)).astype(BF16)

    aq, ak, av, az, bq, bk, bv, bg, gk, sm = _projection(
        x, ctx, mod4, attn_norm, win, conv_w[0], gparams, w2h, w2m, gla_b[0].reshape(1, -1))

    ttot = ctx_len + seq
    smT = sm[:, :, :16].reshape(bsz, ttot // CHUNK, CHUNK, 16).transpose(0, 1, 3, 2)
    ogf, ogb, olf, olb = _scans(aq, ak, av, sm, smT, bq, bk, bv, gk, ctx_len)

    tile_w = lambda v, n: jnp.tile(v.reshape(1, -1), (1, n))
    return _post(x, ogf, ogb, olf, olb, az, bg, mod4,
                 tile_w(gdn_norm[0], A_HEADS), tile_w(gla_norm[0], B_HEADS),
                 w_out[0].astype(BF16), ffn_norm, w_gate[0].astype(BF16), w_up[0].astype(BF16),
                 w_down[0].astype(BF16), final_norm.reshape(1, -1), ctx_len)
```

```python
import functools

import numpy as np
import jax
import jax.numpy as jnp
from jax import lax
from jax.experimental import pallas as pl
from jax.experimental.pallas import tpu as pltpu

F32 = jnp.float32
BF16 = jnp.bfloat16

GRID_W = 64
A_HEADS, A_DK, A_DV = 4, 128, 128
B_HEADS, B_DK, B_DV = 4, 64, 128
NDIR = 2
GLA_RANK = 16
GLA_NORMALIZER = 16.0
EPS = 1e-6

CHUNK = 64
CTX_STEP_CHUNKS = 4
STEP_CHUNKS = 8
SUBLANES = 8
TOK_TILE = 256
POST_SUBTILES = 2
SMALL_W = 128
LEVELS = int(np.log2(CHUNK))

A_W = A_HEADS * A_DK
BQ_W = B_HEADS * B_DK
BV_W = B_HEADS * B_DV
P_COLS = 4 * A_W + 2 * BQ_W + 2 * BV_W + SMALL_W

V7X_VMEM_BYTES = 64 * 1024 * 1024
VMEM_LIMIT = V7X_VMEM_BYTES - 8 * 1024 * 1024


def _bf(x):
    return x.astype(BF16)


def _dot(a, b):
    return jnp.dot(_bf(a), _bf(b), preferred_element_type=F32)


def _dot_nt(a, b):
    return lax.dot_general(_bf(a), _bf(b), (((1,), (1,)), ((), ())), preferred_element_type=F32)


def _dot_tn(a, b):
    return lax.dot_general(_bf(a), _bf(b), (((0,), (0,)), ((), ())), preferred_element_type=F32)


def _split3(x):
    hi = _bf(x)
    r1 = x - hi.astype(F32)
    mid = _bf(r1)
    lo = _bf(r1 - mid.astype(F32))
    return hi, mid, lo


def _dot_exact_lhs(m, x, terms=3):
    d = lambda p: jnp.dot(m, p, preferred_element_type=F32)
    parts = _split3(x)[:terms]
    acc = d(parts[0])
    for p in parts[1:]:
        acc = acc + d(p)
    return acc


def _dot_exact_rhs_nt(x, m):
    hi, mid, lo = _split3(x)
    d = lambda p: lax.dot_general(p, m, (((1,), (1,)), ((), ())), preferred_element_type=F32)
    return d(hi) + d(mid) + d(lo)


def _sigmoid(x):
    return 0.5 * jnp.tanh(0.5 * x) + 0.5


def _silu(x):
    h = 0.5 * x
    return h + h * jnp.tanh(h)


def _softplus(x):
    return jnp.maximum(x, 0.0) + jnp.log(1.0 + jnp.exp(-jnp.abs(x)))


def _later_rows(x, m, fwd):
    n = x.shape[0]
    return jnp.concatenate([x[s + m:s + 2 * m] if fwd else x[s:s + m] for s in range(0, n, 2 * m)], axis=0)


def _merge_later_rows(full, later, m, fwd):
    n = full.shape[0]
    pieces = []
    for i, s in enumerate(range(0, n, 2 * m)):
        lat = later[i * m:(i + 1) * m]
        pieces += [full[s:s + m], lat] if fwd else [lat, full[s + m:s + 2 * m]]
    return jnp.concatenate(pieces, axis=0)


def _mod_kernel(cc_ref, w_ref, b_ref, o_ref):
    o_ref[...] = _dot(_silu(cc_ref[...]), w_ref[...]) + b_ref[...]


def _modulation(cc, w_mod, b_mod):
    rows, d = cc.shape
    n = w_mod.shape[1]
    tn = 1536
    return pl.pallas_call(
        _mod_kernel,
        grid=(n // tn,),
        in_specs=[pl.BlockSpec((rows, d), lambda j: (0, 0)),
                  pl.BlockSpec((d, tn), lambda j: (0, j)),
                  pl.BlockSpec((1, tn), lambda j: (0, j))],
        out_specs=pl.BlockSpec((rows, tn), lambda j: (0, j)),
        out_shape=jax.ShapeDtypeStruct((rows, n), F32),
        compiler_params=pltpu.CompilerParams(dimension_semantics=("arbitrary",),
                                             vmem_limit_bytes=VMEM_LIMIT),
        name="mod",
    )(cc, w_mod, b_mod)


def _proj_kernel(n_ctx_tiles, ctx_len,
                 x_ref, ctx_ref, sh_ref, sc_ref, nw_ref, win_ref, cw_ref, gp_ref, w2h_ref, w2m_ref, gb_ref,
                 aq_ref, ak_ref, av_ref, az_ref, bq_ref, bk_ref, bv_ref, bg_ref, gk_ref, sm_ref):
    j = pl.program_id(1)
    is_ctx = j < n_ctx_tiles
    xin = jnp.where(is_ctx, ctx_ref[0], x_ref[0])
    ms = jnp.mean(xin * xin, axis=-1, keepdims=True)
    h = xin * lax.rsqrt(ms + EPS) * (nw_ref[...] * (1.0 + sc_ref[...])) + sh_ref[...]
    hb = _bf(h)
    tm = xin.shape[0]
    sub = SUBLANES
    sublane = lax.broadcasted_iota(jnp.int32, (1, sub, 1), 1)
    cw = cw_ref[...]

    def mm(c0, c1):
        return jnp.dot(hb, win_ref[:, c0:c1], preferred_element_type=F32)

    def conv_silu(u, c0):
        wu = u.shape[1]
        u3 = u.reshape(tm // sub, sub, wu)

        def shifted(down):
            rot = pltpu.roll(u3, 1 if down else sub - 1, 1)
            nt = tm // sub
            tiles_per_row = GRID_W // sub
            zero = jnp.zeros((1, sub, wu), F32)
            nbr = []
            for r in range(nt):
                src = r - 1 if down else r + 1
                row_start = (r if down else src) % tiles_per_row == 0
                if src < 0 or src >= nt:
                    nbr.append(zero)
                elif row_start:
                    nbr.append(jnp.where(is_ctx, rot[src:src + 1], 0.0))
                else:
                    nbr.append(rot[src:src + 1])
            edge = sublane == (0 if down else sub - 1)
            return jnp.where(edge, jnp.concatenate(nbr, axis=0), rot).reshape(tm, wu)

        w = cw[:, c0:c0 + wu]
        return _silu(w[0:1] * shifted(True) + w[1:2] * u + w[2:3] * shifted(False))

    def l2n(t, scale):
        parts = []
        for hd in range(A_HEADS):
            th = t[:, hd * A_DK:(hd + 1) * A_DK]
            parts.append(th * (lax.rsqrt(jnp.sum(th * th, axis=-1, keepdims=True) + EPS) * scale))
        return jnp.concatenate(parts, axis=-1)

    uq, uk, uv = mm(0, A_W), mm(A_W, 2 * A_W), mm(2 * A_W, 3 * A_W)
    o = 3 * A_W
    aq_ref[0] = l2n(conv_silu(uq, 0), A_DK ** -0.5)
    p1 = mm(o, o + A_W + 2 * BQ_W)
    az_ref[0] = p1[:, :A_W]
    bq_ref[0] = p1[:, A_W:A_W + BQ_W] * (B_DK ** -0.5)
    bk_ref[0] = p1[:, A_W + BQ_W:]
    o += A_W + 2 * BQ_W
    ak_ref[0] = l2n(conv_silu(uk, A_W), 1.0)
    bv_ref[0] = mm(o, o + BV_W)
    o += BV_W
    av_ref[0] = conv_silu(uv, 2 * A_W)
    p3 = mm(o, P_COLS)
    bg_ref[0] = p3[:, :BV_W]

    ps = p3[:, BV_W:]
    gp = gp_ref[...]
    lane = lax.broadcasted_iota(jnp.int32, (1, SMALL_W), 1)
    beta = _sigmoid(ps)
    g = -jnp.exp(gp[1:2]) * _softplus(ps + gp[0:1])
    nb = NDIR * A_HEADS
    sm_ref[0] = jnp.where(lane < nb, beta, jnp.where(lane < 2 * nb, g, 0.0))

    ph = _bf(ps)
    pm = _bf(ps - ph.astype(F32))
    d = lambda a, b: jnp.dot(a, b, preferred_element_type=F32)
    pre = d(ph, w2h_ref[...]) + d(ph, w2m_ref[...]) + d(pm, w2h_ref[...]) + gb_ref[...]
    gk_ref[0] = -_softplus(-pre) * (1.0 / GLA_NORMALIZER)


def _projection(x, ctx, mod4, attn_norm, win, conv_w, gparams, w2h, w2m, gla_b):
    bsz, seq, d = x.shape
    ctx_len = ctx.shape[1]
    tm = TOK_TILE
    nct = ctx_len // tm
    nxt = seq // tm
    ttot = ctx_len + seq
    mod_rows = mod4.shape[0]

    def tok(w):
        return pl.BlockSpec((1, tm, w), lambda b, j: (b, jnp.where(j < nct, nxt + j, j - nct), 0))

    const = lambda shape: pl.BlockSpec(shape, lambda b, j: tuple(0 for _ in shape))
    widths = [A_W, A_W, A_W, A_W, BQ_W, BQ_W, BV_W, BV_W, NDIR * BQ_W, SMALL_W]
    return pl.pallas_call(
        functools.partial(_proj_kernel, nct, ctx_len),
        grid=(bsz, nct + nxt),
        in_specs=[
            pl.BlockSpec((1, tm, d), lambda b, j: (b, jnp.maximum(j - nct, 0), 0)),
            pl.BlockSpec((1, tm, d), lambda b, j: (b, jnp.minimum(j, nct - 1), 0)),
            pl.BlockSpec((None, None, 1, d), lambda b, j: (jnp.where(j < nct, mod_rows - 1, b), 0, 0, 0)),
            pl.BlockSpec((None, None, 1, d), lambda b, j: (jnp.where(j < nct, mod_rows - 1, b), 1, 0, 0)),
            const((1, d)), const((d, P_COLS)), const(conv_w.shape), const(gparams.shape),
            const(w2h.shape), const(w2m.shape), const(gla_b.shape),
        ],
        out_specs=[tok(w) for w in widths],
        out_shape=[jax.ShapeDtypeStruct((bsz, ttot, w), F32) for w in widths],
        compiler_params=pltpu.CompilerParams(dimension_semantics=("arbitrary", "arbitrary"),
                                             vmem_limit_bytes=VMEM_LIMIT),
        name="proj",
    )(x, ctx, mod4, mod4, attn_norm, win, conv_w, gparams, w2h, w2m, gla_b)


def _scan_consts():
    c = CHUNK
    i = np.arange(c)[:, None]
    t = np.arange(c)[None, :]
    out = []
    for d in range(NDIR):
        incl = (t <= i) if d == 0 else (t >= i)
        strict = (t < i) if d == 0 else (t > i)
        tri = incl.astype(np.float32)
        masks = []
        for lv in range(LEVELS):
            m = c >> (lv + 1)
            start = (np.arange(c) // (2 * m)) * (2 * m)
            later = (np.arange(c) % (2 * m) >= m) if d == 0 else (np.arange(c) % (2 * m) < m)
            same = (start[:, None] == start[None, :])
            masks.append((same & later[:, None] & ~later[None, :]).astype(np.float32))
        masks.append(np.eye(c, dtype=np.float32))
        tile2 = lambda mk: np.tile(mk, (1, 2))
        out.append(dict(
            tri=jnp.asarray(tri, BF16),
            tri2=jnp.asarray(np.tile(tri, (2, 1)), BF16),
            lmask2=jnp.asarray(np.stack([tile2(mk) for mk in masks]), F32),
            lmaskv=jnp.asarray(np.stack([np.tile(mk, (B_HEADS, 1)) for mk in masks]), F32),
            incl2=jnp.asarray(tile2(incl.astype(np.float32))),
            strict2=jnp.asarray(tile2(strict.astype(np.float32))),
        ))
    return out


def _gdn_stages(views, s_ref, state, need_out):
    c = CHUNK
    nh = A_HEADS
    npair = nh // 2
    w2 = 2 * c
    first = lax.broadcasted_iota(jnp.int32, (1, w2), 1) < c
    first_k = lax.broadcasted_iota(jnp.int32, (1, 2 * A_DK), 1) < A_DK
    lane_k = lax.broadcasted_iota(jnp.int32, (1, A_W), 1) // A_DK
    km = [lane_k == h for h in range(nh)]

    def expand(cols):
        out = cols[nh - 1]
        for h in range(nh - 2, -1, -1):
            out = jnp.where(km[h], cols[h], out)
        return out

    def blockdiag(x):
        xb = _bf(x)
        return jnp.concatenate([jnp.where(first, xb, 0), jnp.where(first, 0, xb)], axis=0)

    def heads(x, w):
        return [x[:, h * w:(h + 1) * w] for h in range(nh)]

    def prep():
        for vw in views:
            d = vw["d"]
            last = c - 1 if d == 0 else 0
            base = NDIR * nh + nh * d
            sm = vw["sm"]
            gc_all = _dot_exact_lhs(vw["tri"], sm)
            gcr_all = _dot_exact_rhs_nt(vw["smT"], vw["tri2"])
            gcc = [gc_all[:, base + h:base + h + 1] for h in range(nh)]
            gtot = [gc_all[last:last + 1, base + h:base + h + 1] for h in range(nh)]
            beta = [sm[:, nh * d + h:nh * d + h + 1] for h in range(nh)]
            egc = [jnp.exp(g) for g in gcc]
            vw["gl"] = [jnp.exp(g) for g in gtot]
            inc = vw["incl2"] > 0.5
            ak, aq = vw["ak"], vw["aq"]
            beta_k = expand(beta)
            kb = ak * beta_k
            egc_k = expand(egc)
            kbq = _bf(jnp.concatenate([kb, aq], axis=0) if need_out else kb)
            akb = _bf(ak)
            vw["decay"], vw["kbq"], vw["kst"] = [], [], []
            for p in range(npair):
                h0, h1 = 2 * p, 2 * p + 1
                diff = (jnp.where(first, gcc[h0], gcc[h1])
                        - jnp.where(first, gcr_all[base + h0:base + h0 + 1, :], gcr_all[base + h1:base + h1 + 1, :]))
                vw["decay"].append(jnp.where(inc, jnp.exp(jnp.where(inc, diff, 0.0)), 0.0))
                kp = akb[:, 2 * A_DK * p:2 * A_DK * (p + 1)]
                vw["kst"].append(jnp.concatenate([jnp.where(first_k, kp, 0), jnp.where(first_k, 0, kp)], axis=0))
                vw["kbq"].append(kbq[:, 2 * A_DK * p:2 * A_DK * (p + 1)])
            vb = heads(vw["av"] * beta_k, A_DV)
            kbe = heads(kb * egc_k, A_DK)
            rhs = [jnp.concatenate([vb[h], kbe[h]], axis=1) for h in range(nh)]
            vw["rhs"] = [_bf(jnp.concatenate([rhs[2 * p], rhs[2 * p + 1]], axis=0)) for p in range(npair)]
            vw["qe"] = heads(aq * egc_k, A_DK) if need_out else None
            vw["kdec"] = heads(_bf(ak * expand([jnp.exp(gtot[h] - gcc[h]) for h in range(nh)])), A_DK)
        for vw in views:
            vw["m"] = [_dot_nt(vw["kbq"][p], vw["kst"][p]) for p in range(npair)]
        for vw in views:
            lm = vw["lmask2"]
            vw["a"] = [vw["m"][p][:c] * vw["decay"][p] * vw["strict2"] for p in range(npair)]
            if need_out:
                vw["qk"] = [blockdiag(vw["m"][p][c:] * vw["decay"][p]) for p in range(npair)]
            vw["t"] = [lm[LEVELS] - vw["a"][p] * lm[LEVELS - 1] for p in range(npair)]

    def level(lv):
        m = c >> (lv + 1)
        cut = m % SUBLANES == 0

        def run():
            for vw in views:
                fwd = vw["d"] == 0
                rows = (lambda x: _later_rows(x, m, fwd)) if cut else (lambda x: x)
                vw["tl"] = [_dot(rows(vw["t"][p]), blockdiag(vw["a"][p] * vw["lmask2"][lv])) for p in range(npair)]
            for vw in views:
                fwd = vw["d"] == 0
                for p in range(npair):
                    t = vw["t"][p]
                    upd = _dot(vw["tl"][p], blockdiag(t))
                    vw["t"][p] = _merge_later_rows(t, _later_rows(t, m, fwd) - upd, m, fwd) if cut else t - upd
        return run

    def solve():
        for vw in views:
            rp = [_dot(blockdiag(vw["t"][p]), vw["rhs"][p]) for p in range(npair)]
            vw["r"] = [rp[h // 2][(h % 2) * c:(h % 2 + 1) * c] for h in range(nh)]
        for vw in views:
            vw["lhs"] = [_bf(jnp.concatenate([vw["r"][h][:, A_DV:], vw["qe"][h]], axis=0) if need_out
                             else vw["r"][h][:, A_DV:]) for h in range(nh)]

    def recur(pos, is_last):
        def run():
            cur = [vw for vw in views if vw["pos"] == pos]
            for vw in cur:
                d = vw["d"]
                for h in range(nh):
                    if (d, h) not in state:
                        state[(d, h)] = s_ref[d, h]
                vw["ws"] = [_dot(vw["lhs"][h], state[(d, h)]) for h in range(nh)]
            for vw in cur:
                vw["vn"] = [vw["r"][h][:, :A_DV] - vw["ws"][h][:c] for h in range(nh)]
            for vw in cur if need_out else []:
                for p in range(npair):
                    vn2 = _bf(jnp.concatenate([vw["vn"][2 * p], vw["vn"][2 * p + 1]], axis=0))
                    oq = _dot(vw["qk"][p], vn2)
                    for e in range(2):
                        h = 2 * p + e
                        vw["og"][0, vw["rows"], h * A_DV:(h + 1) * A_DV] = vw["ws"][h][c:] + oq[e * c:(e + 1) * c]
            for vw in cur:
                d = vw["d"]
                for h in range(nh):
                    state[(d, h)] = state[(d, h)] * vw["gl"][h] + _dot_tn(vw["kdec"][h], vw["vn"][h])
                    if is_last:
                        s_ref[d, h] = state[(d, h)]
        return run

    return [prep] + [level(lv) for lv in range(LEVELS - 2, -1, -1)] + [solve], recur


def _gla_stages(views, st_ref, state, need_out):
    c = CHUNK
    nh = B_HEADS
    lane_q = lax.broadcasted_iota(jnp.int32, (1, BQ_W), 1) // B_DK
    row = lax.broadcasted_iota(jnp.int32, (c, 1), 0)

    def stack_heads(x):
        return jnp.concatenate([jnp.where(lane_q == h, x, 0) for h in range(nh)], axis=0)

    def level_abs(vw, lv):
        m = c >> (lv + 1)
        fwd = vw["d"] == 0
        bc, na = vw["bc"], vw["na"]
        if m >= 4:
            refs = [s + (m - 1 if fwd else m) for s in range(0, c, 2 * m)]
            ref_rows = jnp.concatenate([jnp.broadcast_to(bc[r:r + 1, :], (2 * m, BQ_W)) for r in refs], axis=0)
            return jnp.abs(bc - ref_rows)
        r = row % (2 * m)
        up = pltpu.roll(na, c - 1, 0)
        dn = pltpu.roll(na, 1, 0)
        if m == 2:
            if fwd:
                return jnp.where(r == 0, up, jnp.where(r == 1, 0.0, jnp.where(r == 2, na, na + dn)))
            return jnp.where(r == 0, na + up, jnp.where(r == 1, na, jnp.where(r == 2, 0.0, dn)))
        return jnp.where(r == (1 if fwd else 0), na, 0.0)

    def prep():
        for vw in views:
            vw["bc"] = _dot_exact_lhs(vw["tri"], vw["gk"])
            vw["na"] = -vw["gk"]
            vw["sacc"] = [jnp.zeros((c, c), F32) for _ in range(nh)]

    def level(lv):
        def run():
            for vw in views:
                m = c >> (lv + 1)
                fwd = vw["d"] == 0
                if lv < LEVELS:
                    wgt = jnp.exp(-level_abs(vw, lv))
                    ql, kl = _bf(vw["bq"] * wgt), _bf(vw["bk"] * wgt)
                else:
                    ql, kl = _bf(vw["bq"]), _bf(vw["bk"])
                if lv < LEVELS and m % SUBLANES == 0:
                    sl = _dot_nt(stack_heads(_later_rows(ql, m, fwd)), kl)
                    mask = _later_rows(vw["lmaskv"][lv, :c, :], m, fwd)
                    hc = c // 2
                    for h in range(nh):
                        old = vw["sacc"][h]
                        new = _later_rows(old, m, fwd) + sl[h * hc:(h + 1) * hc] * mask
                        vw["sacc"][h] = _merge_later_rows(old, new, m, fwd)
                else:
                    sl = _dot_nt(stack_heads(ql), kl) * vw["lmaskv"][lv]
                    vw["sacc"] = [vw["sacc"][h] + sl[h * c:(h + 1) * c] for h in range(nh)]
        return run

    def intra():
        for vw in views:
            last = c - 1 if vw["d"] == 0 else 0
            bc = vw["bc"]
            bct = bc[last:last + 1]
            bvh = [_bf(vw["bv"][:, h * B_DV:(h + 1) * B_DV]) for h in range(nh)]
            kd = vw["bk"] * jnp.exp(bct - bc)
            kdg = jnp.concatenate([kd, jnp.broadcast_to(jnp.exp(bct), (SUBLANES, BQ_W))], axis=0).T
            kdT = _bf(kdg[:, :c])
            vw["lgl"] = kdg[:, c:c + 1]
            if need_out:
                both = [_dot(jnp.concatenate([_bf(vw["sacc"][h]), kdT[h * B_DK:(h + 1) * B_DK]], axis=0), bvh[h])
                        for h in range(nh)]
                vw["oi"] = [b[:c] for b in both]
                vw["upd"] = jnp.concatenate([b[c:] for b in both], axis=0)
                vw["qd"] = stack_heads(_bf(vw["bq"] * jnp.exp(bc)))
            else:
                vw["upd"] = jnp.concatenate([_dot(kdT[h * B_DK:(h + 1) * B_DK], bvh[h]) for h in range(nh)], axis=0)

    def recur(pos, is_last):
        def run():
            for vw in views:
                if vw["pos"] != pos:
                    continue
                d = vw["d"]
                if d not in state:
                    state[d] = st_ref[d]
                if need_out:
                    oint = _dot(vw["qd"], state[d])
                    for h in range(nh):
                        vw["ol"][0, vw["rows"], h * B_DV:(h + 1) * B_DV] = vw["oi"][h] + oint[h * c:(h + 1) * c]
                state[d] = state[d] * vw["lgl"] + vw["upd"]
                if is_last:
                    st_ref[d] = state[d]
        return run

    return [prep] + ([level(lv) for lv in range(LEVELS + 1)] if need_out else []) + [intra], recur


def _scan_kernel(need_out, nchunks, *refs):
    n_in = 9
    n_c = 6
    k = 2 * n_in + 2 * n_c
    if need_out:
        s_in, st_in, og_f, og_b, ol_f, ol_b, s_ref, st_ref = refs[k:]
    else:
        s_out, st_out, s_ref, st_ref = refs[k:]
        og_f = og_b = ol_f = ol_b = None
    c = CHUNK

    @pl.when(pl.program_id(1) == 0)
    def _():
        if need_out:
            s_ref[...] = s_in[0]
            st_ref[...] = st_in[0]
        else:
            s_ref[...] = jnp.zeros_like(s_ref)
            st_ref[...] = jnp.zeros_like(st_ref)

    views = []
    for d, (og, ol) in enumerate(((og_f, ol_f), (og_b, ol_b))):
        aq, ak, av, sm, smT, bq, bk, bv, gk = refs[d * n_in:(d + 1) * n_in]
        tri, tri2, lmask2, lmaskv, incl2, strict2 = refs[2 * n_in + d * n_c:2 * n_in + (d + 1) * n_c]
        order = range(nchunks) if d == 0 else range(nchunks - 1, -1, -1)
        for pos, j in enumerate(order):
            rows = pl.ds(j * c, c)
            views.append(dict(d=d, pos=pos, rows=rows, aq=aq[0, rows, :], ak=ak[0, rows, :], av=av[0, rows, :],
                              sm=sm[0, rows, :], smT=smT[j], bq=bq[0, rows, :], bk=bk[0, rows, :],
                              bv=bv[0, rows, :], gk=gk[0, rows, :], tri=tri[...], tri2=tri2[...], lmask2=lmask2,
                              lmaskv=lmaskv, incl2=incl2[...], strict2=strict2[...], og=og, ol=ol))
    gdn, gdn_recur = _gdn_stages(views, s_ref, {}, need_out)
    gla, gla_recur = _gla_stages(views, st_ref, {}, need_out)
    for i in range(max(len(gdn), len(gla))):
        if i < len(gdn):
            gdn[i]()
        if i < len(gla):
            gla[i]()
    for pos in range(nchunks):
        gdn_recur(pos, pos == nchunks - 1)()
        gla_recur(pos, pos == nchunks - 1)()

    if not need_out:
        @pl.when(pl.program_id(1) == pl.num_programs(1) - 1)
        def _():
            s_out[0] = s_ref[...]
            st_out[0] = st_ref[...]


def _scans(aq, ak, av, sm, smT, bq, bk, bv, gk, ctx_len):
    bsz, ttot, _ = aq.shape
    seq = ttot - ctx_len
    c = CHUNK
    consts = _scan_consts()
    const = lambda a: pl.BlockSpec(a.shape, lambda b, s: tuple(0 for _ in a.shape))
    cargs = []
    for d in range(NDIR):
        cd = consts[d]
        cargs += [cd["tri"], cd["tri2"], cd["lmask2"], cd["lmaskv"], cd["incl2"], cd["strict2"]]
    ins = [aq, ak, av, sm, smT, bq, bk, bv, gk]
    s_shape = (NDIR, A_HEADS, A_DK, A_DV)
    st_shape = (NDIR, BQ_W, B_DV)
    state_specs = [pl.BlockSpec((1,) + s_shape, lambda b, s: (b, 0, 0, 0, 0)),
                   pl.BlockSpec((1,) + st_shape, lambda b, s: (b, 0, 0, 0))]
    state_shapes = [jax.ShapeDtypeStruct((bsz,) + s_shape, F32), jax.ShapeDtypeStruct((bsz,) + st_shape, F32)]
    scratch = [pltpu.VMEM(s_shape, F32), pltpu.VMEM(st_shape, F32)]
    params = pltpu.CompilerParams(dimension_semantics=("arbitrary", "arbitrary"), vmem_limit_bytes=VMEM_LIMIT)

    def in_specs(nchunks, first, n):
        tb = nchunks * c

        def tok(w, idx, lane_blk=0):
            return pl.BlockSpec((1, tb, w), lambda b, s: (b, idx(s), lane_blk))

        def dir_specs(idx, d):
            return [tok(A_W, idx), tok(A_W, idx), tok(A_W, idx), tok(SMALL_W, idx),
                    pl.BlockSpec((None, nchunks, 16, c), lambda b, s: (b, idx(s), 0, 0)),
                    tok(BQ_W, idx), tok(BQ_W, idx), tok(BV_W, idx), tok(BQ_W, idx, d)]

        return dir_specs(lambda s: first + s, 0) + dir_specs(lambda s: first + n - 1 - s, 1) + [const(a) for a in cargs]

    tbc = CTX_STEP_CHUNKS * c
    ncs = ctx_len // tbc
    s_ctx, st_ctx = pl.pallas_call(
        functools.partial(_scan_kernel, False, CTX_STEP_CHUNKS),
        grid=(bsz, ncs),
        in_specs=in_specs(CTX_STEP_CHUNKS, seq // tbc, ncs),
        out_specs=state_specs,
        out_shape=state_shapes,
        scratch_shapes=scratch,
        compiler_params=params,
        name="scan_ctx",
    )(*ins, *ins, *cargs)

    tb = STEP_CHUNKS * c
    nxs = seq // tb

    def otok(w, idx):
        return pl.BlockSpec((1, tb, w), lambda b, s: (b, idx(s), 0))

    out_w = [A_HEADS * A_DV, A_HEADS * A_DV, BV_W, BV_W]
    fwd, bwd = (lambda s: s), (lambda s: nxs - 1 - s)
    return pl.pallas_call(
        functools.partial(_scan_kernel, True, STEP_CHUNKS),
        grid=(bsz, nxs),
        in_specs=in_specs(STEP_CHUNKS, 0, nxs) + state_specs,
        out_specs=[otok(out_w[0], fwd), otok(out_w[1], bwd), otok(out_w[2], fwd), otok(out_w[3], bwd)],
        out_shape=[jax.ShapeDtypeStruct((bsz, seq, w), F32) for w in out_w],
        scratch_shapes=scratch,
        compiler_params=params,
        name="scan",
    )(*ins, *ins, *cargs, s_ctx, st_ctx)


def _head_norm(o, w, z, n_heads, dv):
    parts = []
    for h in range(n_heads):
        oh = o[:, h * dv:(h + 1) * dv]
        parts.append(oh * lax.rsqrt(jnp.mean(oh * oh, axis=-1, keepdims=True) + EPS))
    return jnp.concatenate(parts, axis=-1) * w * _silu(z)


def _rms(x, w):
    return x * lax.rsqrt(jnp.mean(x * x, axis=-1, keepdims=True) + EPS) * w


def _post_kernel(*refs):
    ns = POST_SUBTILES
    x_ref = refs[0]
    sub = [refs[1 + 6 * s:7 + 6 * s] for s in range(ns)]
    (g1_ref, sh2_ref, sc2_ref, g2_ref, gn_ref, ln_ref, wo_ref, fn_ref,
     wg_ref, wu_ref, wd_ref, fin_ref, o_ref) = refs[1 + 6 * ns:]
    tm = TOK_TILE
    for s in range(ns):
        ogf, ogb, olf, olb, az, bg = sub[s]
        rows = pl.ds(s * tm, tm)
        gdn = _head_norm(ogf[0] + ogb[0], gn_ref[...], az[0], A_HEADS, A_DV)
        gla = _head_norm(olf[0] + olb[0], ln_ref[...], bg[0], B_HEADS, B_DV)
        mix = jnp.concatenate([gdn, gla], axis=-1)
        x1 = x_ref[0, rows, :] + g1_ref[...] * _dot(mix, wo_ref[...])
        h2b = _bf(_rms(x1, fn_ref[...]) * (1.0 + sc2_ref[...]) + sh2_ref[...])
        gate = jnp.dot(h2b, wg_ref[...], preferred_element_type=F32)
        up = jnp.dot(h2b, wu_ref[...], preferred_element_type=F32)
        y = _dot(_silu(gate) * up, wd_ref[...])
        x2 = x1 + g2_ref[...] * y
        o_ref[0, rows, :] = _rms(x2, fin_ref[...])


def _post(x, ogf, ogb, olf, olb, az, bg, mod4, gn, ln, wo, fn, wg, wu, wd, fin, ctx_len):
    bsz, seq, d = x.shape
    tm = TOK_TILE
    ns = POST_SUBTILES
    dff = wg.shape[1]

    def tok(w, s):
        return pl.BlockSpec((1, tm, w), lambda b, j: (b, ns * j + s, 0))

    def xtok():
        return pl.BlockSpec((1, ns * tm, d), lambda b, j: (b, j, 0))

    def modspec(k):
        return pl.BlockSpec((None, None, 1, d), lambda b, j: (b, k, 0, 0))

    def const(shape, single=True):
        idx = lambda b, j: tuple(0 for _ in shape)
        if single:
            return pl.BlockSpec(shape, idx, pipeline_mode=pl.Buffered(1))
        return pl.BlockSpec(shape, idx)

    widths = [A_W, A_W, BV_W, BV_W, A_W, BV_W]
    return pl.pallas_call(
        _post_kernel,
        grid=(bsz, seq // (ns * tm)),
        in_specs=[xtok()] + [tok(w, s) for s in range(ns) for w in widths] + [
                  modspec(2), modspec(3), modspec(4), modspec(5),
                  const((1, A_W), False), const((1, BV_W), False), const((d, d)), const((1, d), False),
                  const((d, dff)), const((d, dff)), const((dff, d)), const((1, d), False)],
        out_specs=xtok(),
        out_shape=jax.ShapeDtypeStruct((bsz, seq, d), F32),
        compiler_params=pltpu.CompilerParams(dimension_semantics=("arbitrary", "arbitrary"),
                                             vmem_limit_bytes=VMEM_LIMIT),
        name="post",
    )(x, *([ogf, ogb, olf, olb, az, bg] * ns), mod4, mod4, mod4, mod4, gn, ln, wo, fn, wg, wu, wd, fin)


def kernel(x, c, ctx, c_ctx, w_mod, b_mod, attn_norm, w_in, conv_w, a_log, dt_bias, gdn_norm, gla_w2, gla_b,
           gla_norm, w_out, ffn_norm, w_gate, w_up, w_down, final_norm):
    bsz, seq, d = x.shape
    ctx_len = ctx.shape[1]
    assert w_mod.shape[0] == 1, "single-layer block"
    assert ctx_len == TOK_TILE and seq % (TOK_TILE * POST_SUBTILES) == 0
    assert TOK_TILE % GRID_W == 0 and ctx_len % (CHUNK * CTX_STEP_CHUNKS) == 0 and seq % (CHUNK * STEP_CHUNKS) == 0

    rows = -(-(bsz + 1) // 8) * 8
    cc = jnp.zeros((rows, d), F32).at[:bsz].set(c).at[rows - 1].set(c_ctx)
    mod = _modulation(cc, w_mod[0], b_mod[0][None, :])
    mod4 = mod.reshape(rows, 6, 1, d)

    w = w_in[0]
    o1 = 4 * A_W
    nb = NDIR * A_HEADS
    o2 = o1 + 2 * nb
    o3 = o2 + 2 * BQ_W + 2 * BV_W
    nlr = NDIR * GLA_RANK
    win = jnp.concatenate([w[:, :o1], w[:, o2:o3], w[:, o1:o2], w[:, o3:o3 + nlr],
                           jnp.zeros((d, SMALL_W - 2 * nb - nlr), F32)], axis=1).astype(BF16)
    gparams = jnp.zeros((2, SMALL_W), F32)
    gparams = gparams.at[0, nb:2 * nb].set(dt_bias[0].reshape(-1)).at[1, nb:2 * nb].set(a_log[0].reshape(-1))
    w2 = jnp.zeros((SMALL_W, NDIR * BQ_W), F32)
    for n in range(NDIR):
        w2 = w2.at[2 * nb + n * GLA_RANK:2 * nb + (n + 1) * GLA_RANK, n * BQ_W:(n + 1) * BQ_W].set(gla_w2[0, n])
    w2h = w2.astype(BF16)
    w2m = (w2 - w2h.astype(F32)).astype(BF16)

    aq, ak, av, az, bq, bk, bv, bg, gk, sm = _projection(
        x, ctx, mod4, attn_norm, win, conv_w[0], gparams, w2h, w2m, gla_b[0].reshape(1, -1))

    ttot = ctx_len + seq
    smT = sm[:, :, :16].reshape(bsz, ttot // CHUNK, CHUNK, 16).transpose(0, 1, 3, 2)
    ogf, ogb, olf, olb = _scans(aq, ak, av, sm, smT, bq, bk, bv, gk, ctx_len)

    tile_w = lambda v, n: jnp.tile(v.reshape(1, -1), (1, n))
    return _post(x, ogf, ogb, olf, olb, az, bg, mod4,
                 tile_w(gdn_norm[0], A_HEADS), tile_w(gla_norm[0], B_HEADS),
                 w_out[0].astype(BF16), ffn_norm, w_gate[0].astype(BF16), w_up[0].astype(BF16),
                 w_down[0].astype(BF16), final_norm.reshape(1, -1), ctx_len)
```

```python
import functools

import numpy as np
import jax
import jax.numpy as jnp
from jax import lax
from jax.experimental import pallas as pl
from jax.experimental.pallas import tpu as pltpu

F32 = jnp.float32
BF16 = jnp.bfloat16

GRID_W = 64
A_HEADS, A_DK, A_DV = 4, 128, 128
B_HEADS, B_DK, B_DV = 4, 64, 128
NDIR = 2
GLA_RANK = 16
GLA_NORMALIZER = 16.0
EPS = 1e-6

CHUNK = 64
CTX_STEP_CHUNKS = 4
STEP_CHUNKS = 8
SUBLANES = 8
TOK_TILE = 256
POST_SUBTILES = 2
SMALL_W = 128
LEVELS = int(np.log2(CHUNK))

A_W = A_HEADS * A_DK
BQ_W = B_HEADS * B_DK
BV_W = B_HEADS * B_DV
P_COLS = 4 * A_W + 2 * BQ_W + 2 * BV_W + SMALL_W

V7X_VMEM_BYTES = 64 * 1024 * 1024
VMEM_LIMIT = V7X_VMEM_BYTES - 8 * 1024 * 1024


def _bf(x):
    return x.astype(BF16)


def _dot(a, b):
    return jnp.dot(_bf(a), _bf(b), preferred_element_type=F32)


def _dot_nt(a, b):
    return lax.dot_general(_bf(a), _bf(b), (((1,), (1,)), ((), ())), preferred_element_type=F32)


def _dot_tn(a, b):
    return lax.dot_general(_bf(a), _bf(b), (((0,), (0,)), ((), ())), preferred_element_type=F32)


def _split3(x):
    hi = _bf(x)
    r1 = x - hi.astype(F32)
    mid = _bf(r1)
    lo = _bf(r1 - mid.astype(F32))
    return hi, mid, lo


def _dot_exact_lhs(m, x, terms=3):
    d = lambda p: jnp.dot(m, p, preferred_element_type=F32)
    parts = _split3(x)[:terms]
    acc = d(parts[0])
    for p in parts[1:]:
        acc = acc + d(p)
    return acc


def _dot_exact_rhs_nt(x, m):
    hi, mid, lo = _split3(x)
    d = lambda p: lax.dot_general(p, m, (((1,), (1,)), ((), ())), preferred_element_type=F32)
    return d(hi) + d(mid) + d(lo)


def _sigmoid(x):
    return 0.5 * jnp.tanh(0.5 * x) + 0.5


def _silu(x):
    h = 0.5 * x
    return h + h * jnp.tanh(h)


def _softplus(x):
    return jnp.maximum(x, 0.0) + jnp.log(1.0 + jnp.exp(-jnp.abs(x)))


def _later_rows(x, m, fwd):
    n = x.shape[0]
    return jnp.concatenate([x[s + m:s + 2 * m] if fwd else x[s:s + m] for s in range(0, n, 2 * m)], axis=0)


def _merge_later_rows(full, later, m, fwd):
    n = full.shape[0]
    pieces = []
    for i, s in enumerate(range(0, n, 2 * m)):
        lat = later[i * m:(i + 1) * m]
        pieces += [full[s:s + m], lat] if fwd else [lat, full[s + m:s + 2 * m]]
    return jnp.concatenate(pieces, axis=0)


def _mod_kernel(cc_ref, w_ref, b_ref, o_ref):
    o_ref[...] = _dot(_silu(cc_ref[...]), w_ref[...]) + b_ref[...]


def _modulation(cc, w_mod, b_mod):
    rows, d = cc.shape
    n = w_mod.shape[1]
    tn = 1536
    return pl.pallas_call(
        _mod_kernel,
        grid=(n // tn,),
        in_specs=[pl.BlockSpec((rows, d), lambda j: (0, 0)),
                  pl.BlockSpec((d, tn), lambda j: (0, j)),
                  pl.BlockSpec((1, tn), lambda j: (0, j))],
        out_specs=pl.BlockSpec((rows, tn), lambda j: (0, j)),
        out_shape=jax.ShapeDtypeStruct((rows, n), F32),
        compiler_params=pltpu.CompilerParams(dimension_semantics=("arbitrary",),
                                             vmem_limit_bytes=VMEM_LIMIT),
        name="mod",
    )(cc, w_mod, b_mod)


def _proj_kernel(n_ctx_tiles, ctx_len,
                 x_ref, ctx_ref, sh_ref, sc_ref, nw_ref, win_ref, cw_ref, gp_ref, w2h_ref, w2m_ref, gb_ref,
                 aq_ref, ak_ref, av_ref, az_ref, bq_ref, bk_ref, bv_ref, bg_ref, gk_ref, sm_ref):
    j = pl.program_id(1)
    is_ctx = j < n_ctx_tiles
    xin = jnp.where(is_ctx, ctx_ref[0], x_ref[0])
    ms = jnp.mean(xin * xin, axis=-1, keepdims=True)
    h = xin * lax.rsqrt(ms + EPS) * (nw_ref[...] * (1.0 + sc_ref[...])) + sh_ref[...]
    hb = _bf(h)
    tm = xin.shape[0]
    sub = SUBLANES
    sublane = lax.broadcasted_iota(jnp.int32, (1, sub, 1), 1)
    cw_half = 0.5 * cw_ref[...]

    def mm(c0, c1):
        return jnp.dot(hb, win_ref[:, c0:c1], preferred_element_type=F32)

    def conv_silu(u, c0):
        wu = u.shape[1]
        u3 = u.reshape(tm // sub, sub, wu)

        def shifted(down):
            rot = pltpu.roll(u3, 1 if down else sub - 1, 1)
            nt = tm // sub
            tiles_per_row = GRID_W // sub
            zero = jnp.zeros((1, sub, wu), F32)
            nbr = []
            for r in range(nt):
                src = r - 1 if down else r + 1
                row_start = (r if down else src) % tiles_per_row == 0
                if src < 0 or src >= nt:
                    nbr.append(zero)
                elif row_start:
                    nbr.append(jnp.where(is_ctx, rot[src:src + 1], 0.0))
                else:
                    nbr.append(rot[src:src + 1])
            edge = sublane == (0 if down else sub - 1)
            return jnp.where(edge, jnp.concatenate(nbr, axis=0), rot).reshape(tm, wu)

        w = cw_half[:, c0:c0 + wu]
        h = w[0:1] * shifted(True) + w[1:2] * u + w[2:3] * shifted(False)
        return h + h * jnp.tanh(h)

    def l2n(t, scale):
        parts = []
        for hd in range(A_HEADS):
            th = t[:, hd * A_DK:(hd + 1) * A_DK]
            parts.append(th * (lax.rsqrt(jnp.sum(th * th, axis=-1, keepdims=True) + EPS) * scale))
        return jnp.concatenate(parts, axis=-1)

    o = 3 * A_W
    uq = mm(0, A_W)
    p1 = mm(o, o + A_W + 2 * BQ_W)
    o += A_W + 2 * BQ_W
    uk = mm(A_W, 2 * A_W)
    p2 = mm(o, o + BV_W)
    o += BV_W
    uv = mm(2 * A_W, 3 * A_W)
    p3 = mm(o, P_COLS)
    aq_ref[0] = l2n(conv_silu(uq, 0), A_DK ** -0.5)
    az_ref[0] = p1[:, :A_W]
    bq_ref[0] = p1[:, A_W:A_W + BQ_W]
    bk_ref[0] = p1[:, A_W + BQ_W:]
    ak_ref[0] = l2n(conv_silu(uk, A_W), 1.0)
    bv_ref[0] = p2
    av_ref[0] = conv_silu(uv, 2 * A_W)
    bg_ref[0] = p3[:, :BV_W]

    ps = p3[:, BV_W:]
    gp = gp_ref[...]
    lane = lax.broadcasted_iota(jnp.int32, (1, SMALL_W), 1)
    beta = _sigmoid(ps)
    g = -jnp.exp(gp[1:2]) * _softplus(ps + gp[0:1])
    nb = NDIR * A_HEADS
    sm_ref[0] = jnp.where(lane < nb, beta, jnp.where(lane < 2 * nb, g, 0.0))

    ph = _bf(ps)
    pm = _bf(ps - ph.astype(F32))
    d = lambda a, b: jnp.dot(a, b, preferred_element_type=F32)
    pre = d(ph, w2h_ref[...]) + d(ph, w2m_ref[...]) + d(pm, w2h_ref[...]) + gb_ref[...]
    gk_ref[0] = -_softplus(-pre) * (1.0 / GLA_NORMALIZER)


def _projection(x, ctx, mod4, attn_norm, win, conv_w, gparams, w2h, w2m, gla_b):
    bsz, seq, d = x.shape
    ctx_len = ctx.shape[1]
    tm = TOK_TILE
    nct = ctx_len // tm
    nxt = seq // tm
    ttot = ctx_len + seq
    mod_rows = mod4.shape[0]

    def tok(w):
        return pl.BlockSpec((1, tm, w), lambda b, j: (b, jnp.where(j < nct, nxt + j, j - nct), 0))

    const = lambda shape: pl.BlockSpec(shape, lambda b, j: tuple(0 for _ in shape))
    widths = [A_W, A_W, A_W, A_W, BQ_W, BQ_W, BV_W, BV_W, NDIR * BQ_W, SMALL_W]
    return pl.pallas_call(
        functools.partial(_proj_kernel, nct, ctx_len),
        grid=(bsz, nct + nxt),
        in_specs=[
            pl.BlockSpec((1, tm, d), lambda b, j: (b, jnp.maximum(j - nct, 0), 0)),
            pl.BlockSpec((1, tm, d), lambda b, j: (b, jnp.minimum(j, nct - 1), 0)),
            pl.BlockSpec((None, None, 1, d), lambda b, j: (jnp.where(j < nct, mod_rows - 1, b), 0, 0, 0)),
            pl.BlockSpec((None, None, 1, d), lambda b, j: (jnp.where(j < nct, mod_rows - 1, b), 1, 0, 0)),
            const((1, d)), const((d, P_COLS)), const(conv_w.shape), const(gparams.shape),
            const(w2h.shape), const(w2m.shape), const(gla_b.shape),
        ],
        out_specs=[tok(w) for w in widths],
        out_shape=[jax.ShapeDtypeStruct((bsz, ttot, w), F32) for w in widths],
        compiler_params=pltpu.CompilerParams(dimension_semantics=("arbitrary", "arbitrary"),
                                             vmem_limit_bytes=VMEM_LIMIT),
        name="proj",
    )(x, ctx, mod4, mod4, attn_norm, win, conv_w, gparams, w2h, w2m, gla_b)


def _scan_consts():
    c = CHUNK
    i = np.arange(c)[:, None]
    t = np.arange(c)[None, :]
    out = []
    for d in range(NDIR):
        incl = (t <= i) if d == 0 else (t >= i)
        strict = (t < i) if d == 0 else (t > i)
        tri = incl.astype(np.float32)
        masks = []
        for lv in range(LEVELS):
            m = c >> (lv + 1)
            start = (np.arange(c) // (2 * m)) * (2 * m)
            later = (np.arange(c) % (2 * m) >= m) if d == 0 else (np.arange(c) % (2 * m) < m)
            same = (start[:, None] == start[None, :])
            masks.append((same & later[:, None] & ~later[None, :]).astype(np.float32))
        masks.append(np.eye(c, dtype=np.float32))
        tile2 = lambda mk: np.tile(mk, (1, 2))
        out.append(dict(
            tri=jnp.asarray(tri, BF16),
            tri2=jnp.asarray(np.tile(tri, (2, 1)), BF16),
            lmask2=jnp.asarray(np.stack([tile2(mk) for mk in masks]), F32),
            lmaskv=jnp.asarray(np.stack([np.tile(mk, (B_HEADS, 1)) for mk in masks]), F32),
            incl2=jnp.asarray(tile2(incl.astype(np.float32))),
            strict2=jnp.asarray(tile2(strict.astype(np.float32))),
        ))
    return out


def _gdn_stages(views, s_ref, state, need_out):
    c = CHUNK
    nh = A_HEADS
    npair = nh // 2
    w2 = 2 * c
    first = lax.broadcasted_iota(jnp.int32, (1, w2), 1) < c
    first_k = lax.broadcasted_iota(jnp.int32, (1, 2 * A_DK), 1) < A_DK
    lane_k = lax.broadcasted_iota(jnp.int32, (1, A_W), 1) // A_DK
    km = [lane_k == h for h in range(nh)]

    def expand(cols):
        out = cols[nh - 1]
        for h in range(nh - 2, -1, -1):
            out = jnp.where(km[h], cols[h], out)
        return out

    def blockdiag(x):
        xb = _bf(x)
        return jnp.concatenate([jnp.where(first, xb, 0), jnp.where(first, 0, xb)], axis=0)

    def heads(x, w):
        return [x[:, h * w:(h + 1) * w] for h in range(nh)]

    def prep():
        for vw in views:
            d = vw["d"]
            last = c - 1 if d == 0 else 0
            base = NDIR * nh + nh * d
            sm = vw["sm"]
            gc_all = _dot_exact_lhs(vw["tri"], sm)
            gcr_all = _dot_exact_rhs_nt(vw["smT"], vw["tri2"])
            gcc = [gc_all[:, base + h:base + h + 1] for h in range(nh)]
            gtot = [gc_all[last:last + 1, base + h:base + h + 1] for h in range(nh)]
            beta = [sm[:, nh * d + h:nh * d + h + 1] for h in range(nh)]
            egc = [jnp.exp(g) for g in gcc]
            vw["gl"] = [jnp.exp(g) for g in gtot]
            inc = vw["incl2"] > 0.5
            ak, aq = vw["ak"], vw["aq"]
            beta_k = expand(beta)
            kb = ak * beta_k
            egc_k = expand(egc)
            kbq = _bf(jnp.concatenate([kb, aq], axis=0) if need_out else kb)
            akb = _bf(ak)
            vw["decay"], vw["kbq"], vw["kst"] = [], [], []
            for p in range(npair):
                h0, h1 = 2 * p, 2 * p + 1
                diff = (jnp.where(first, gcc[h0], gcc[h1])
                        - jnp.where(first, gcr_all[base + h0:base + h0 + 1, :], gcr_all[base + h1:base + h1 + 1, :]))
                vw["decay"].append(jnp.where(inc, jnp.exp(jnp.where(inc, diff, 0.0)), 0.0))
                kp = akb[:, 2 * A_DK * p:2 * A_DK * (p + 1)]
                vw["kst"].append(jnp.concatenate([jnp.where(first_k, kp, 0), jnp.where(first_k, 0, kp)], axis=0))
                vw["kbq"].append(kbq[:, 2 * A_DK * p:2 * A_DK * (p + 1)])
            vb = heads(vw["av"] * beta_k, A_DV)
            kbe = heads(kb * egc_k, A_DK)
            rhs = [jnp.concatenate([vb[h], kbe[h]], axis=1) for h in range(nh)]
            vw["rhs"] = [_bf(jnp.concatenate([rhs[2 * p], rhs[2 * p + 1]], axis=0)) for p in range(npair)]
            vw["qe"] = heads(aq * egc_k, A_DK) if need_out else None
            vw["kdec"] = heads(_bf(ak * expand([jnp.exp(gtot[h] - gcc[h]) for h in range(nh)])), A_DK)
        for vw in views:
            vw["m"] = [_dot_nt(vw["kbq"][p], vw["kst"][p]) for p in range(npair)]
        for vw in views:
            lm = vw["lmask2"]
            vw["a"] = [vw["m"][p][:c] * vw["decay"][p] * vw["strict2"] for p in range(npair)]
            if need_out:
                vw["qk"] = [blockdiag(vw["m"][p][c:] * vw["decay"][p]) for p in range(npair)]
            vw["t"] = [lm[LEVELS] - vw["a"][p] * lm[LEVELS - 1] for p in range(npair)]

    def level(lv):
        m = c >> (lv + 1)
        cut = m % SUBLANES == 0

        def run():
            for vw in views:
                fwd = vw["d"] == 0
                rows = (lambda x: _later_rows(x, m, fwd)) if cut else (lambda x: x)
                vw["tl"] = [_dot(rows(vw["t"][p]), blockdiag(vw["a"][p] * vw["lmask2"][lv])) for p in range(npair)]
            for vw in views:
                fwd = vw["d"] == 0
                for p in range(npair):
                    t = vw["t"][p]
                    upd = _dot(vw["tl"][p], blockdiag(t))
                    vw["t"][p] = _merge_later_rows(t, _later_rows(t, m, fwd) - upd, m, fwd) if cut else t - upd
        return run

    def solve():
        for vw in views:
            rp = [_dot(blockdiag(vw["t"][p]), vw["rhs"][p]) for p in range(npair)]
            vw["r"] = [rp[h // 2][(h % 2) * c:(h % 2 + 1) * c] for h in range(nh)]
        for vw in views:
            vw["lhs"] = [_bf(jnp.concatenate([vw["r"][h][:, A_DV:], vw["qe"][h]], axis=0) if need_out
                             else vw["r"][h][:, A_DV:]) for h in range(nh)]

    def recur(pos, is_last):
        def run():
            cur = [vw for vw in views if vw["pos"] == pos]
            for vw in cur:
                d = vw["d"]
                for h in range(nh):
                    if (d, h) not in state:
                        state[(d, h)] = s_ref[d, h]
                vw["ws"] = [_dot(vw["lhs"][h], state[(d, h)]) for h in range(nh)]
            for vw in cur:
                vw["vn"] = [vw["r"][h][:, :A_DV] - vw["ws"][h][:c] for h in range(nh)]
            for vw in cur if need_out else []:
                for p in range(npair):
                    vn2 = _bf(jnp.concatenate([vw["vn"][2 * p], vw["vn"][2 * p + 1]], axis=0))
                    oq = _dot(vw["qk"][p], vn2)
                    for e in range(2):
                        h = 2 * p + e
                        vw["og"][0, vw["rows"], h * A_DV:(h + 1) * A_DV] = vw["ws"][h][c:] + oq[e * c:(e + 1) * c]
            for vw in cur:
                d = vw["d"]
                for h in range(nh):
                    state[(d, h)] = state[(d, h)] * vw["gl"][h] + _dot_tn(vw["kdec"][h], vw["vn"][h])
                    if is_last:
                        s_ref[d, h] = state[(d, h)]
        return run

    return [prep] + [level(lv) for lv in range(LEVELS - 2, -1, -1)] + [solve], recur


def _gla_stages(views, st_ref, state, need_out):
    c = CHUNK
    nh = B_HEADS
    lane_q = lax.broadcasted_iota(jnp.int32, (1, BQ_W), 1) // B_DK
    row = lax.broadcasted_iota(jnp.int32, (c, 1), 0)

    def stack_heads(x):
        return jnp.concatenate([jnp.where(lane_q == h, x, 0) for h in range(nh)], axis=0)

    def level_abs(vw, lv):
        m = c >> (lv + 1)
        fwd = vw["d"] == 0
        bc, na = vw["bc"], vw["na"]
        if m >= 4:
            refs = [s + (m - 1 if fwd else m) for s in range(0, c, 2 * m)]
            ref_rows = jnp.concatenate([jnp.broadcast_to(bc[r:r + 1, :], (2 * m, BQ_W)) for r in refs], axis=0)
            return jnp.abs(bc - ref_rows)
        r = row % (2 * m)
        up = pltpu.roll(na, c - 1, 0)
        dn = pltpu.roll(na, 1, 0)
        if m == 2:
            if fwd:
                return jnp.where(r == 0, up, jnp.where(r == 1, 0.0, jnp.where(r == 2, na, na + dn)))
            return jnp.where(r == 0, na + up, jnp.where(r == 1, na, jnp.where(r == 2, 0.0, dn)))
        return jnp.where(r == (1 if fwd else 0), na, 0.0)

    def prep():
        for vw in views:
            vw["bc"] = _dot_exact_lhs(vw["tri"], vw["gk"])
            vw["na"] = -vw["gk"]
            vw["sacc"] = [jnp.zeros((c, c), F32) for _ in range(nh)]

    def level(lv):
        def run():
            for vw in views:
                m = c >> (lv + 1)
                fwd = vw["d"] == 0
                if lv < LEVELS:
                    wgt = jnp.exp(-level_abs(vw, lv))
                    ql, kl = _bf(vw["bq"] * wgt), _bf(vw["bk"] * wgt)
                else:
                    ql, kl = _bf(vw["bq"]), _bf(vw["bk"])
                if lv < LEVELS and m % SUBLANES == 0:
                    sl = _dot_nt(stack_heads(_later_rows(ql, m, fwd)), kl)
                    mask = _later_rows(vw["lmaskv"][lv, :c, :], m, fwd)
                    hc = c // 2
                    for h in range(nh):
                        old = vw["sacc"][h]
                        new = _later_rows(old, m, fwd) + sl[h * hc:(h + 1) * hc] * mask
                        vw["sacc"][h] = _merge_later_rows(old, new, m, fwd)
                else:
                    sl = _dot_nt(stack_heads(ql), kl) * vw["lmaskv"][lv]
                    vw["sacc"] = [vw["sacc"][h] + sl[h * c:(h + 1) * c] for h in range(nh)]
        return run

    def intra():
        for vw in views:
            last = c - 1 if vw["d"] == 0 else 0
            bc = vw["bc"]
            bct = bc[last:last + 1]
            bvh = [_bf(vw["bv"][:, h * B_DV:(h + 1) * B_DV]) for h in range(nh)]
            kd = vw["bk"] * jnp.exp(bct - bc)
            kdg = jnp.concatenate([kd, jnp.broadcast_to(jnp.exp(bct), (SUBLANES, BQ_W))], axis=0).T
            kdT = _bf(kdg[:, :c])
            vw["lgl"] = kdg[:, c:c + 1]
            if need_out:
                both = [_dot(jnp.concatenate([_bf(vw["sacc"][h]), kdT[h * B_DK:(h + 1) * B_DK]], axis=0), bvh[h])
                        for h in range(nh)]
                vw["oi"] = [b[:c] for b in both]
                vw["upd"] = jnp.concatenate([b[c:] for b in both], axis=0)
                vw["qd"] = stack_heads(_bf(vw["bq"] * jnp.exp(bc)))
            else:
                vw["upd"] = jnp.concatenate([_dot(kdT[h * B_DK:(h + 1) * B_DK], bvh[h]) for h in range(nh)], axis=0)

    def recur(pos, is_last):
        def run():
            for vw in views:
                if vw["pos"] != pos:
                    continue
                d = vw["d"]
                if d not in state:
                    state[d] = st_ref[d]
                if need_out:
                    oint = _dot(vw["qd"], state[d])
                    for h in range(nh):
                        vw["ol"][0, vw["rows"], h * B_DV:(h + 1) * B_DV] = vw["oi"][h] + oint[h * c:(h + 1) * c]
                state[d] = state[d] * vw["lgl"] + vw["upd"]
                if is_last:
                    st_ref[d] = state[d]
        return run

    return [prep] + ([level(lv) for lv in range(LEVELS + 1)] if need_out else []) + [intra], recur


def _scan_kernel(need_out, nchunks, *refs):
    n_in = 9
    n_c = 6
    k = 2 * n_in + 2 * n_c
    if need_out:
        s_in, st_in, og_f, og_b, ol_f, ol_b, s_ref, st_ref = refs[k:]
    else:
        s_out, st_out, s_ref, st_ref = refs[k:]
        og_f = og_b = ol_f = ol_b = None
    c = CHUNK

    @pl.when(pl.program_id(1) == 0)
    def _():
        if need_out:
            s_ref[...] = s_in[0]
            st_ref[...] = st_in[0]
        else:
            s_ref[...] = jnp.zeros_like(s_ref)
            st_ref[...] = jnp.zeros_like(st_ref)

    views = []
    for d, (og, ol) in enumerate(((og_f, ol_f), (og_b, ol_b))):
        aq, ak, av, sm, smT, bq, bk, bv, gk = refs[d * n_in:(d + 1) * n_in]
        tri, tri2, lmask2, lmaskv, incl2, strict2 = refs[2 * n_in + d * n_c:2 * n_in + (d + 1) * n_c]
        order = range(nchunks) if d == 0 else range(nchunks - 1, -1, -1)
        for pos, j in enumerate(order):
            rows = pl.ds(j * c, c)
            views.append(dict(d=d, pos=pos, rows=rows, aq=aq[0, rows, :], ak=ak[0, rows, :], av=av[0, rows, :],
                              sm=sm[0, rows, :], smT=smT[j], bq=bq[0, rows, :], bk=bk[0, rows, :],
                              bv=bv[0, rows, :], gk=gk[0, rows, :], tri=tri[...], tri2=tri2[...], lmask2=lmask2,
                              lmaskv=lmaskv, incl2=incl2[...], strict2=strict2[...], og=og, ol=ol))
    gdn, gdn_recur = _gdn_stages(views, s_ref, {}, need_out)
    gla, gla_recur = _gla_stages(views, st_ref, {}, need_out)
    for i in range(max(len(gdn), len(gla))):
        if i < len(gdn):
            gdn[i]()
        if i < len(gla):
            gla[i]()
    for pos in range(nchunks):
        gdn_recur(pos, pos == nchunks - 1)()
        gla_recur(pos, pos == nchunks - 1)()

    if not need_out:
        @pl.when(pl.program_id(1) == pl.num_programs(1) - 1)
        def _():
            s_out[0] = s_ref[...]
            st_out[0] = st_ref[...]


def _scans(aq, ak, av, sm, smT, bq, bk, bv, gk, ctx_len):
    bsz, ttot, _ = aq.shape
    seq = ttot - ctx_len
    c = CHUNK
    consts = _scan_consts()
    const = lambda a: pl.BlockSpec(a.shape, lambda b, s: tuple(0 for _ in a.shape))
    cargs = []
    for d in range(NDIR):
        cd = consts[d]
        cargs += [cd["tri"], cd["tri2"], cd["lmask2"], cd["lmaskv"], cd["incl2"], cd["strict2"]]
    ins = [aq, ak, av, sm, smT, bq, bk, bv, gk]
    s_shape = (NDIR, A_HEADS, A_DK, A_DV)
    st_shape = (NDIR, BQ_W, B_DV)
    state_specs = [pl.BlockSpec((1,) + s_shape, lambda b, s: (b, 0, 0, 0, 0)),
                   pl.BlockSpec((1,) + st_shape, lambda b, s: (b, 0, 0, 0))]
    state_shapes = [jax.ShapeDtypeStruct((bsz,) + s_shape, F32), jax.ShapeDtypeStruct((bsz,) + st_shape, F32)]
    scratch = [pltpu.VMEM(s_shape, F32), pltpu.VMEM(st_shape, F32)]
    params = pltpu.CompilerParams(dimension_semantics=("arbitrary", "arbitrary"), vmem_limit_bytes=VMEM_LIMIT)

    def in_specs(nchunks, first, n):
        tb = nchunks * c

        def tok(w, idx, lane_blk=0):
            return pl.BlockSpec((1, tb, w), lambda b, s: (b, idx(s), lane_blk))

        def dir_specs(idx, d):
            return [tok(A_W, idx), tok(A_W, idx), tok(A_W, idx), tok(SMALL_W, idx),
                    pl.BlockSpec((None, nchunks, 16, c), lambda b, s: (b, idx(s), 0, 0)),
                    tok(BQ_W, idx), tok(BQ_W, idx), tok(BV_W, idx), tok(BQ_W, idx, d)]

        return dir_specs(lambda s: first + s, 0) + dir_specs(lambda s: first + n - 1 - s, 1) + [const(a) for a in cargs]

    tbc = CTX_STEP_CHUNKS * c
    ncs = ctx_len // tbc
    s_ctx, st_ctx = pl.pallas_call(
        functools.partial(_scan_kernel, False, CTX_STEP_CHUNKS),
        grid=(bsz, ncs),
        in_specs=in_specs(CTX_STEP_CHUNKS, seq // tbc, ncs),
        out_specs=state_specs,
        out_shape=state_shapes,
        scratch_shapes=scratch,
        compiler_params=params,
        name="scan_ctx",
    )(*ins, *ins, *cargs)

    tb = STEP_CHUNKS * c
    nxs = seq // tb

    def otok(w, idx):
        return pl.BlockSpec((1, tb, w), lambda b, s: (b, idx(s), 0))

    out_w = [A_HEADS * A_DV, A_HEADS * A_DV, BV_W, BV_W]
    fwd, bwd = (lambda s: s), (lambda s: nxs - 1 - s)
    return pl.pallas_call(
        functools.partial(_scan_kernel, True, STEP_CHUNKS),
        grid=(bsz, nxs),
        in_specs=in_specs(STEP_CHUNKS, 0, nxs) + state_specs,
        out_specs=[otok(out_w[0], fwd), otok(out_w[1], bwd), otok(out_w[2], fwd), otok(out_w[3], bwd)],
        out_shape=[jax.ShapeDtypeStruct((bsz, seq, w), F32) for w in out_w],
        scratch_shapes=scratch,
        compiler_params=params,
        name="scan",
    )(*ins, *ins, *cargs, s_ctx, st_ctx)


def _head_norm(o, w, z, n_heads, dv):
    parts = []
    for h in range(n_heads):
        oh = o[:, h * dv:(h + 1) * dv]
        parts.append(oh * lax.rsqrt(jnp.mean(oh * oh, axis=-1, keepdims=True) + EPS))
    return jnp.concatenate(parts, axis=-1) * w * _silu(z)


def _rms(x, w):
    return x * lax.rsqrt(jnp.mean(x * x, axis=-1, keepdims=True) + EPS) * w


def _post_kernel(*refs):
    ns = POST_SUBTILES
    x_ref = refs[0]
    sub = [refs[1 + 6 * s:7 + 6 * s] for s in range(ns)]
    (g1_ref, sh2_ref, sc2_ref, g2_ref, gn_ref, ln_ref, wo_ref, fn_ref,
     wg_ref, wu_ref, wd_ref, fin_ref, o_ref) = refs[1 + 6 * ns:]
    tm = TOK_TILE
    for s in range(ns):
        ogf, ogb, olf, olb, az, bg = sub[s]
        rows = pl.ds(s * tm, tm)
        gdn = _head_norm(ogf[0] + ogb[0], gn_ref[...], az[0], A_HEADS, A_DV)
        gla = _head_norm(olf[0] + olb[0], ln_ref[...], bg[0], B_HEADS, B_DV)
        mix = jnp.concatenate([gdn, gla], axis=-1)
        x1 = x_ref[0, rows, :] + g1_ref[...] * _dot(mix, wo_ref[...])
        h2b = _bf(_rms(x1, fn_ref[...]) * (1.0 + sc2_ref[...]) + sh2_ref[...])
        gate = jnp.dot(h2b, wg_ref[...], preferred_element_type=F32)
        up = jnp.dot(h2b, wu_ref[...], preferred_element_type=F32)
        y = _dot(_silu(gate) * up, wd_ref[...])
        x2 = x1 + g2_ref[...] * y
        o_ref[0, rows, :] = _rms(x2, fin_ref[...])


def _post(x, ogf, ogb, olf, olb, az, bg, mod4, gn, ln, wo, fn, wg, wu, wd, fin):
    bsz, seq, d = x.shape
    tm = TOK_TILE
    ns = POST_SUBTILES
    dff = wg.shape[1]

    def tok(w, s):
        return pl.BlockSpec((1, tm, w), lambda b, j: (b, ns * j + s, 0))

    def xtok():
        return pl.BlockSpec((1, ns * tm, d), lambda b, j: (b, j, 0))

    def modspec(k):
        return pl.BlockSpec((None, None, 1, d), lambda b, j: (b, k, 0, 0))

    def const(shape, single=True):
        idx = lambda b, j: tuple(0 for _ in shape)
        if single:
            return pl.BlockSpec(shape, idx, pipeline_mode=pl.Buffered(1))
        return pl.BlockSpec(shape, idx)

    widths = [A_W, A_W, BV_W, BV_W, A_W, BV_W]
    return pl.pallas_call(
        _post_kernel,
        grid=(bsz, seq // (ns * tm)),
        in_specs=[xtok()] + [tok(w, s) for s in range(ns) for w in widths] + [
                  modspec(2), modspec(3), modspec(4), modspec(5),
                  const((1, A_W), False), const((1, BV_W), False), const((d, d)), const((1, d), False),
                  const((d, dff)), const((d, dff)), const((dff, d)), const((1, d), False)],
        out_specs=xtok(),
        out_shape=jax.ShapeDtypeStruct((bsz, seq, d), F32),
        compiler_params=pltpu.CompilerParams(dimension_semantics=("arbitrary", "arbitrary"),
                                             vmem_limit_bytes=VMEM_LIMIT),
        name="post",
    )(x, *([ogf, ogb, olf, olb, az, bg] * ns), mod4, mod4, mod4, mod4, gn, ln, wo, fn, wg, wu, wd, fin)


def kernel(x, c, ctx, c_ctx, w_mod, b_mod, attn_norm, w_in, conv_w, a_log, dt_bias, gdn_norm, gla_w2, gla_b,
           gla_norm, w_out, ffn_norm, w_gate, w_up, w_down, final_norm):
    bsz, seq, d = x.shape
    ctx_len = ctx.shape[1]
    assert w_mod.shape[0] == 1, "single-layer block"
    assert ctx_len == TOK_TILE and seq % (TOK_TILE * POST_SUBTILES) == 0
    assert TOK_TILE % GRID_W == 0 and ctx_len % (CHUNK * CTX_STEP_CHUNKS) == 0 and seq % (CHUNK * STEP_CHUNKS) == 0

    rows = -(-(bsz + 1) // 8) * 8
    cc = jnp.zeros((rows, d), F32).at[:bsz].set(c).at[rows - 1].set(c_ctx)
    mod = _modulation(cc, w_mod[0], b_mod[0][None, :])
    mod4 = mod.reshape(rows, 6, 1, d)

    w = w_in[0]
    o1 = 4 * A_W
    nb = NDIR * A_HEADS
    o2 = o1 + 2 * nb
    o3 = o2 + 2 * BQ_W + 2 * BV_W
    nlr = NDIR * GLA_RANK
    win = jnp.concatenate([w[:, :o1], w[:, o2:o2 + BQ_W] * B_DK ** -0.5, w[:, o2 + BQ_W:o3], w[:, o1:o2],
                           w[:, o3:o3 + nlr], jnp.zeros((d, SMALL_W - 2 * nb - nlr), F32)], axis=1).astype(BF16)
    gparams = jnp.zeros((2, SMALL_W), F32)
    gparams = gparams.at[0, nb:2 * nb].set(dt_bias[0].reshape(-1)).at[1, nb:2 * nb].set(a_log[0].reshape(-1))
    w2 = jnp.zeros((SMALL_W, NDIR * BQ_W), F32)
    for n in range(NDIR):
        w2 = w2.at[2 * nb + n * GLA_RANK:2 * nb + (n + 1) * GLA_RANK, n * BQ_W:(n + 1) * BQ_W].set(gla_w2[0, n])
    w2h = w2.astype(BF16)
    w2m = (w2 - w2h.astype(F32)).astype(BF16)

    aq, ak, av, az, bq, bk, bv, bg, gk, sm = _projection(
        x, ctx, mod4, attn_norm, win, conv_w[0], gparams, w2h, w2m, gla_b[0].reshape(1, -1))

    ttot = ctx_len + seq
    smT = sm[:, :, :16].reshape(bsz, ttot // CHUNK, CHUNK, 16).transpose(0, 1, 3, 2)
    ogf, ogb, olf, olb = _scans(aq, ak, av, sm, smT, bq, bk, bv, gk, ctx_len)

    tile_w = lambda v, n: jnp.tile(v.reshape(1, -1), (1, n))
    return _post(x, ogf, ogb, olf, olb, az, bg, mod4,
                 tile_w(gdn_norm[0], A_HEADS), tile_w(gla_norm[0], B_HEADS),
                 w_out[0].astype(BF16), ffn_norm, w_gate[0].astype(BF16), w_up[0].astype(BF16),
                 w_down[0].astype(BF16), final_norm.reshape(1, -1))
```

```python
import functools

import numpy as np
import jax
import jax.numpy as jnp
from jax import lax
from jax.experimental import pallas as pl
from jax.experimental.pallas import tpu as pltpu

F32 = jnp.float32
BF16 = jnp.bfloat16

GRID_W = 64
A_HEADS, A_DK, A_DV = 4, 128, 128
B_HEADS, B_DK, B_DV = 4, 64, 128
NDIR = 2
GLA_RANK = 16
GLA_NORMALIZER = 16.0
EPS = 1e-6

CHUNK = 64
CTX_STEP_CHUNKS = 4
STEP_CHUNKS = 8
SUBLANES = 8
TOK_TILE = 256
POST_TILE = 512
FFN_CHUNK = 1024
SMALL_W = 128
LEVELS = int(np.log2(CHUNK))

A_W = A_HEADS * A_DK
BQ_W = B_HEADS * B_DK
BV_W = B_HEADS * B_DV
P_COLS = 4 * A_W + 2 * BQ_W + 2 * BV_W + SMALL_W

V7X_VMEM_BYTES = 64 * 1024 * 1024
VMEM_LIMIT = V7X_VMEM_BYTES - 8 * 1024 * 1024


def _bf(x):
    return x.astype(BF16)


def _dot(a, b):
    return jnp.dot(_bf(a), _bf(b), preferred_element_type=F32)


def _dot_nt(a, b):
    return lax.dot_general(_bf(a), _bf(b), (((1,), (1,)), ((), ())), preferred_element_type=F32)


def _dot_tn(a, b):
    return lax.dot_general(_bf(a), _bf(b), (((0,), (0,)), ((), ())), preferred_element_type=F32)


def _split3(x):
    hi = _bf(x)
    r1 = x - hi.astype(F32)
    mid = _bf(r1)
    lo = _bf(r1 - mid.astype(F32))
    return hi, mid, lo


def _dot_exact_lhs(m, x, terms=3):
    d = lambda p: jnp.dot(m, p, preferred_element_type=F32)
    parts = _split3(x)[:terms]
    acc = d(parts[0])
    for p in parts[1:]:
        acc = acc + d(p)
    return acc


def _dot_exact_rhs_nt(x, m):
    hi, mid, lo = _split3(x)
    d = lambda p: lax.dot_general(p, m, (((1,), (1,)), ((), ())), preferred_element_type=F32)
    return d(hi) + d(mid) + d(lo)


def _sigmoid(x):
    return 0.5 * jnp.tanh(0.5 * x) + 0.5


def _silu(x):
    h = 0.5 * x
    return h + h * jnp.tanh(h)


def _softplus(x):
    return jnp.maximum(x, 0.0) + jnp.log(1.0 + jnp.exp(-jnp.abs(x)))


def _later_rows(x, m, fwd):
    n = x.shape[0]
    return jnp.concatenate([x[s + m:s + 2 * m] if fwd else x[s:s + m] for s in range(0, n, 2 * m)], axis=0)


def _merge_later_rows(full, later, m, fwd):
    n = full.shape[0]
    pieces = []
    for i, s in enumerate(range(0, n, 2 * m)):
        lat = later[i * m:(i + 1) * m]
        pieces += [full[s:s + m], lat] if fwd else [lat, full[s + m:s + 2 * m]]
    return jnp.concatenate(pieces, axis=0)


def _mod_kernel(cc_ref, w_ref, b_ref, o_ref):
    o_ref[...] = _dot(_silu(cc_ref[...]), w_ref[...]) + b_ref[...]


def _modulation(cc, w_mod, b_mod):
    rows, d = cc.shape
    n = w_mod.shape[1]
    tn = 1536
    return pl.pallas_call(
        _mod_kernel,
        grid=(n // tn,),
        in_specs=[pl.BlockSpec((rows, d), lambda j: (0, 0)),
                  pl.BlockSpec((d, tn), lambda j: (0, j)),
                  pl.BlockSpec((1, tn), lambda j: (0, j))],
        out_specs=pl.BlockSpec((rows, tn), lambda j: (0, j)),
        out_shape=jax.ShapeDtypeStruct((rows, n), F32),
        compiler_params=pltpu.CompilerParams(dimension_semantics=("arbitrary",),
                                             vmem_limit_bytes=VMEM_LIMIT),
        name="mod",
    )(cc, w_mod, b_mod)


def _proj_kernel(n_ctx_tiles, ctx_len,
                 x_ref, ctx_ref, sh_ref, sc_ref, nw_ref, win_ref, cw_ref, gp_ref, w2h_ref, w2m_ref, gb_ref,
                 aq_ref, ak_ref, av_ref, az_ref, bq_ref, bk_ref, bv_ref, bg_ref, gk_ref, sm_ref, smt_ref):
    j = pl.program_id(1)
    is_ctx = j < n_ctx_tiles
    xin = jnp.where(is_ctx, ctx_ref[0], x_ref[0])
    ms = jnp.mean(xin * xin, axis=-1, keepdims=True)
    h = xin * lax.rsqrt(ms + EPS) * (nw_ref[...] * (1.0 + sc_ref[...])) + sh_ref[...]
    hb = _bf(h)
    tm = xin.shape[0]
    sub = SUBLANES
    sublane = lax.broadcasted_iota(jnp.int32, (1, sub, 1), 1)
    cw_half = 0.5 * cw_ref[...]

    def mm(c0, c1):
        return jnp.dot(hb, win_ref[:, c0:c1], preferred_element_type=F32)

    def conv_silu(u, c0):
        wu = u.shape[1]
        u3 = u.reshape(tm // sub, sub, wu)

        def shifted(down):
            rot = pltpu.roll(u3, 1 if down else sub - 1, 1)
            nt = tm // sub
            tiles_per_row = GRID_W // sub
            zero = jnp.zeros((1, sub, wu), F32)
            nbr = []
            for r in range(nt):
                src = r - 1 if down else r + 1
                row_start = (r if down else src) % tiles_per_row == 0
                if src < 0 or src >= nt:
                    nbr.append(zero)
                elif row_start:
                    nbr.append(jnp.where(is_ctx, rot[src:src + 1], 0.0))
                else:
                    nbr.append(rot[src:src + 1])
            edge = sublane == (0 if down else sub - 1)
            return jnp.where(edge, jnp.concatenate(nbr, axis=0), rot).reshape(tm, wu)

        w = cw_half[:, c0:c0 + wu]
        h = w[0:1] * shifted(True) + w[1:2] * u + w[2:3] * shifted(False)
        return h + h * jnp.tanh(h)

    def l2n(t, scale):
        parts = []
        for hd in range(A_HEADS):
            th = t[:, hd * A_DK:(hd + 1) * A_DK]
            parts.append(th * (lax.rsqrt(jnp.sum(th * th, axis=-1, keepdims=True) + EPS) * scale))
        return jnp.concatenate(parts, axis=-1)

    o = 3 * A_W
    uq = mm(0, A_W)
    p1 = mm(o, o + A_W + 2 * BQ_W)
    o += A_W + 2 * BQ_W
    uk = mm(A_W, 2 * A_W)
    p2 = mm(o, o + BV_W)
    o += BV_W
    uv = mm(2 * A_W, 3 * A_W)
    p3 = mm(o, P_COLS)
    aq_ref[0] = l2n(conv_silu(uq, 0), A_DK ** -0.5)
    az_ref[0] = p1[:, :A_W]
    bq_ref[0] = p1[:, A_W:A_W + BQ_W]
    bk_ref[0] = p1[:, A_W + BQ_W:]
    ak_ref[0] = l2n(conv_silu(uk, A_W), 1.0)
    bv_ref[0] = p2
    av_ref[0] = conv_silu(uv, 2 * A_W)
    bg_ref[0] = p3[:, :BV_W]

    ps = p3[:, BV_W:]
    gp = gp_ref[...]
    lane = lax.broadcasted_iota(jnp.int32, (1, SMALL_W), 1)
    beta = _sigmoid(ps)
    g = -jnp.exp(gp[1:2]) * _softplus(ps + gp[0:1])
    nb = NDIR * A_HEADS
    sm = jnp.where(lane < nb, beta, jnp.where(lane < 2 * nb, g, 0.0))
    sm_ref[0] = sm
    for jc in range(tm // CHUNK):
        smt_ref[0, jc] = sm[jc * CHUNK:(jc + 1) * CHUNK, :].T[:2 * nb, :]

    ph = _bf(ps)
    pm = _bf(ps - ph.astype(F32))
    d = lambda a, b: jnp.dot(a, b, preferred_element_type=F32)
    pre = d(ph, w2h_ref[...]) + d(ph, w2m_ref[...]) + d(pm, w2h_ref[...]) + gb_ref[...]
    gk_ref[0] = -_softplus(-pre) * (1.0 / GLA_NORMALIZER)


def _projection(x, ctx, mod4, attn_norm, win, conv_w, gparams, w2h, w2m, gla_b):
    bsz, seq, d = x.shape
    ctx_len = ctx.shape[1]
    tm = TOK_TILE
    nct = ctx_len // tm
    nxt = seq // tm
    ttot = ctx_len + seq
    mod_rows = mod4.shape[0]

    def tile(j):
        return jnp.where(j < nct, nxt + j, j - nct)

    def tok(w):
        return pl.BlockSpec((1, tm, w), lambda b, j: (b, tile(j), 0))

    cpt = tm // CHUNK
    nsm = 2 * NDIR * A_HEADS

    const = lambda shape: pl.BlockSpec(shape, lambda b, j: tuple(0 for _ in shape))
    widths = [A_W, A_W, A_W, A_W, BQ_W, BQ_W, BV_W, BV_W, NDIR * BQ_W, SMALL_W]
    return pl.pallas_call(
        functools.partial(_proj_kernel, nct, ctx_len),
        grid=(bsz, nct + nxt),
        in_specs=[
            pl.BlockSpec((1, tm, d), lambda b, j: (b, jnp.maximum(j - nct, 0), 0)),
            pl.BlockSpec((1, tm, d), lambda b, j: (b, jnp.minimum(j, nct - 1), 0)),
            pl.BlockSpec((None, None, 1, d), lambda b, j: (jnp.where(j < nct, mod_rows - 1, b), 0, 0, 0)),
            pl.BlockSpec((None, None, 1, d), lambda b, j: (jnp.where(j < nct, mod_rows - 1, b), 1, 0, 0)),
            const((1, d)), const((d, P_COLS)), const(conv_w.shape), const(gparams.shape),
            const(w2h.shape), const(w2m.shape), const(gla_b.shape),
        ],
        out_specs=[tok(w) for w in widths] + [
            pl.BlockSpec((1, cpt, nsm, CHUNK), lambda b, j: (b, tile(j), 0, 0))],
        out_shape=[jax.ShapeDtypeStruct((bsz, ttot, w), F32) for w in widths] + [
            jax.ShapeDtypeStruct((bsz, ttot // CHUNK, nsm, CHUNK), F32)],
        compiler_params=pltpu.CompilerParams(dimension_semantics=("arbitrary", "arbitrary"),
                                             vmem_limit_bytes=VMEM_LIMIT),
        name="proj",
    )(x, ctx, mod4, mod4, attn_norm, win, conv_w, gparams, w2h, w2m, gla_b)


def _scan_consts():
    c = CHUNK
    i = np.arange(c)[:, None]
    t = np.arange(c)[None, :]
    out = []
    for d in range(NDIR):
        incl = (t <= i) if d == 0 else (t >= i)
        strict = (t < i) if d == 0 else (t > i)
        tri = incl.astype(np.float32)
        masks = []
        for lv in range(LEVELS):
            m = c >> (lv + 1)
            start = (np.arange(c) // (2 * m)) * (2 * m)
            later = (np.arange(c) % (2 * m) >= m) if d == 0 else (np.arange(c) % (2 * m) < m)
            same = (start[:, None] == start[None, :])
            masks.append((same & later[:, None] & ~later[None, :]).astype(np.float32))
        masks.append(np.eye(c, dtype=np.float32))
        tile2 = lambda mk: np.tile(mk, (1, 2))
        out.append(dict(
            tri=jnp.asarray(tri, BF16),
            tri2=jnp.asarray(np.tile(tri, (2, 1)), BF16),
            lmask2=jnp.asarray(np.stack([tile2(mk) for mk in masks]), F32),
            lmaskv=jnp.asarray(np.stack([np.tile(mk, (B_HEADS, 1)) for mk in masks]), F32),
            incl2=jnp.asarray(tile2(incl.astype(np.float32))),
            strict2=jnp.asarray(tile2(strict.astype(np.float32))),
        ))
    return out


def _gdn_stages(views, s_ref, state, need_out):
    c = CHUNK
    nh = A_HEADS
    npair = nh // 2
    w2 = 2 * c
    first = lax.broadcasted_iota(jnp.int32, (1, w2), 1) < c
    first_k = lax.broadcasted_iota(jnp.int32, (1, 2 * A_DK), 1) < A_DK
    lane_k = lax.broadcasted_iota(jnp.int32, (1, A_W), 1) // A_DK
    km = [lane_k == h for h in range(nh)]

    def expand(cols):
        out = cols[nh - 1]
        for h in range(nh - 2, -1, -1):
            out = jnp.where(km[h], cols[h], out)
        return out

    def blockdiag(x):
        xb = _bf(x)
        return jnp.concatenate([jnp.where(first, xb, 0), jnp.where(first, 0, xb)], axis=0)

    def heads(x, w):
        return [x[:, h * w:(h + 1) * w] for h in range(nh)]

    def prep():
        for vw in views:
            d = vw["d"]
            last = c - 1 if d == 0 else 0
            base = NDIR * nh + nh * d
            sm = vw["sm"]
            gc_all = _dot_exact_lhs(vw["tri"], sm)
            gcr_all = _dot_exact_rhs_nt(vw["smT"], vw["tri2"])
            gcc = [gc_all[:, base + h:base + h + 1] for h in range(nh)]
            gtot = [gc_all[last:last + 1, base + h:base + h + 1] for h in range(nh)]
            beta = [sm[:, nh * d + h:nh * d + h + 1] for h in range(nh)]
            egc = [jnp.exp(g) for g in gcc]
            vw["gl"] = [jnp.exp(g) for g in gtot]
            inc = vw["incl2"] > 0.5
            ak, aq = vw["ak"], vw["aq"]
            beta_k = expand(beta)
            kb = ak * beta_k
            egc_k = expand(egc)
            kbq = _bf(jnp.concatenate([kb, aq], axis=0) if need_out else kb)
            akb = _bf(ak)
            vw["decay"], vw["kbq"], vw["kst"] = [], [], []
            for p in range(npair):
                h0, h1 = 2 * p, 2 * p + 1
                diff = (jnp.where(first, gcc[h0], gcc[h1])
                        - jnp.where(first, gcr_all[base + h0:base + h0 + 1, :], gcr_all[base + h1:base + h1 + 1, :]))
                vw["decay"].append(jnp.where(inc, jnp.exp(jnp.where(inc, diff, 0.0)), 0.0))
                kp = akb[:, 2 * A_DK * p:2 * A_DK * (p + 1)]
                vw["kst"].append(jnp.concatenate([jnp.where(first_k, kp, 0), jnp.where(first_k, 0, kp)], axis=0))
                vw["kbq"].append(kbq[:, 2 * A_DK * p:2 * A_DK * (p + 1)])
            vb = heads(vw["av"] * beta_k, A_DV)
            kbe = heads(kb * egc_k, A_DK)
            rhs = [jnp.concatenate([vb[h], kbe[h]], axis=1) for h in range(nh)]
            vw["rhs"] = [_bf(jnp.concatenate([rhs[2 * p], rhs[2 * p + 1]], axis=0)) for p in range(npair)]
            vw["qe"] = heads(aq * egc_k, A_DK) if need_out else None
            vw["kdec"] = heads(_bf(ak * expand([jnp.exp(gtot[h] - gcc[h]) for h in range(nh)])), A_DK)
        for vw in views:
            vw["m"] = [_dot_nt(vw["kbq"][p], vw["kst"][p]) for p in range(npair)]
        for vw in views:
            lm = vw["lmask2"]
            vw["a"] = [vw["m"][p][:c] * vw["decay"][p] * vw["strict2"] for p in range(npair)]
            if need_out:
                vw["qk"] = [blockdiag(vw["m"][p][c:] * vw["decay"][p]) for p in range(npair)]
            vw["t"] = [lm[LEVELS] - vw["a"][p] * lm[LEVELS - 1] for p in range(npair)]

    def level(lv):
        m = c >> (lv + 1)
        cut = m % SUBLANES == 0

        def run():
            for vw in views:
                fwd = vw["d"] == 0
                rows = (lambda x: _later_rows(x, m, fwd)) if cut else (lambda x: x)
                vw["tl"] = [_dot(rows(vw["t"][p]), blockdiag(vw["a"][p] * vw["lmask2"][lv])) for p in range(npair)]
            for vw in views:
                fwd = vw["d"] == 0
                for p in range(npair):
                    t = vw["t"][p]
                    upd = _dot(vw["tl"][p], blockdiag(t))
                    vw["t"][p] = _merge_later_rows(t, _later_rows(t, m, fwd) - upd, m, fwd) if cut else t - upd
        return run

    def solve():
        for vw in views:
            rp = [_dot(blockdiag(vw["t"][p]), vw["rhs"][p]) for p in range(npair)]
            vw["r"] = [rp[h // 2][(h % 2) * c:(h % 2 + 1) * c] for h in range(nh)]
        for vw in views:
            vw["lhs"] = [_bf(jnp.concatenate([vw["r"][h][:, A_DV:], vw["qe"][h]], axis=0) if need_out
                             else vw["r"][h][:, A_DV:]) for h in range(nh)]

    def recur(pos, is_last):
        def run():
            cur = [vw for vw in views if vw["pos"] == pos]
            for vw in cur:
                d = vw["d"]
                for h in range(nh):
                    if (d, h) not in state:
                        state[(d, h)] = s_ref[d, h]
                vw["ws"] = [_dot(vw["lhs"][h], state[(d, h)]) for h in range(nh)]
            for vw in cur:
                vw["vn"] = [vw["r"][h][:, :A_DV] - vw["ws"][h][:c] for h in range(nh)]
            for vw in cur if need_out else []:
                for p in range(npair):
                    vn2 = _bf(jnp.concatenate([vw["vn"][2 * p], vw["vn"][2 * p + 1]], axis=0))
                    oq = _dot(vw["qk"][p], vn2)
                    for e in range(2):
                        h = 2 * p + e
                        vw["og"][0, vw["rows"], h * A_DV:(h + 1) * A_DV] = vw["ws"][h][c:] + oq[e * c:(e + 1) * c]
            for vw in cur:
                d = vw["d"]
                for h in range(nh):
                    state[(d, h)] = state[(d, h)] * vw["gl"][h] + _dot_tn(vw["kdec"][h], vw["vn"][h])
                    if is_last:
                        s_ref[d, h] = state[(d, h)]
        return run

    return [prep] + [level(lv) for lv in range(LEVELS - 2, -1, -1)] + [solve], recur


def _gla_stages(views, st_ref, state, need_out):
    c = CHUNK
    nh = B_HEADS
    lane_q = lax.broadcasted_iota(jnp.int32, (1, BQ_W), 1) // B_DK
    row = lax.broadcasted_iota(jnp.int32, (c, 1), 0)

    def stack_heads(x):
        return jnp.concatenate([jnp.where(lane_q == h, x, 0) for h in range(nh)], axis=0)

    def level_abs(vw, lv):
        m = c >> (lv + 1)
        fwd = vw["d"] == 0
        bc, na = vw["bc"], vw["na"]
        if m >= 4:
            refs = [s + (m - 1 if fwd else m) for s in range(0, c, 2 * m)]
            ref_rows = jnp.concatenate([jnp.broadcast_to(bc[r:r + 1, :], (2 * m, BQ_W)) for r in refs], axis=0)
            return jnp.abs(bc - ref_rows)
        r = row % (2 * m)
        up = pltpu.roll(na, c - 1, 0)
        dn = pltpu.roll(na, 1, 0)
        if m == 2:
            if fwd:
                return jnp.where(r == 0, up, jnp.where(r == 1, 0.0, jnp.where(r == 2, na, na + dn)))
            return jnp.where(r == 0, na + up, jnp.where(r == 1, na, jnp.where(r == 2, 0.0, dn)))
        return jnp.where(r == (1 if fwd else 0), na, 0.0)

    def prep():
        for vw in views:
            vw["bc"] = _dot_exact_lhs(vw["tri"], vw["gk"])
            vw["na"] = -vw["gk"]
            vw["sacc"] = [jnp.zeros((c, c), F32) for _ in range(nh)]

    def level(lv):
        def run():
            for vw in views:
                m = c >> (lv + 1)
                fwd = vw["d"] == 0
                if lv < LEVELS:
                    wgt = jnp.exp(-level_abs(vw, lv))
                    ql, kl = _bf(vw["bq"] * wgt), _bf(vw["bk"] * wgt)
                else:
                    ql, kl = _bf(vw["bq"]), _bf(vw["bk"])
                if lv < LEVELS and m % SUBLANES == 0:
                    sl = _dot_nt(stack_heads(_later_rows(ql, m, fwd)), kl)
                    mask = _later_rows(vw["lmaskv"][lv, :c, :], m, fwd)
                    hc = c // 2
                    for h in range(nh):
                        old = vw["sacc"][h]
                        new = _later_rows(old, m, fwd) + sl[h * hc:(h + 1) * hc] * mask
                        vw["sacc"][h] = _merge_later_rows(old, new, m, fwd)
                else:
                    sl = _dot_nt(stack_heads(ql), kl) * vw["lmaskv"][lv]
                    vw["sacc"] = [vw["sacc"][h] + sl[h * c:(h + 1) * c] for h in range(nh)]
        return run

    def intra():
        for vw in views:
            last = c - 1 if vw["d"] == 0 else 0
            bc = vw["bc"]
            bct = bc[last:last + 1]
            bvh = [_bf(vw["bv"][:, h * B_DV:(h + 1) * B_DV]) for h in range(nh)]
            kd = vw["bk"] * jnp.exp(bct - bc)
            kdg = jnp.concatenate([kd, jnp.broadcast_to(jnp.exp(bct), (SUBLANES, BQ_W))], axis=0).T
            kdT = _bf(kdg[:, :c])
            vw["lgl"] = kdg[:, c:c + 1]
            if need_out:
                both = [_dot(jnp.concatenate([_bf(vw["sacc"][h]), kdT[h * B_DK:(h + 1) * B_DK]], axis=0), bvh[h])
                        for h in range(nh)]
                vw["oi"] = [b[:c] for b in both]
                vw["upd"] = jnp.concatenate([b[c:] for b in both], axis=0)
                vw["qd"] = stack_heads(_bf(vw["bq"] * jnp.exp(bc)))
            else:
                vw["upd"] = jnp.concatenate([_dot(kdT[h * B_DK:(h + 1) * B_DK], bvh[h]) for h in range(nh)], axis=0)

    def recur(pos, is_last):
        def run():
            for vw in views:
                if vw["pos"] != pos:
                    continue
                d = vw["d"]
                if d not in state:
                    state[d] = st_ref[d]
                if need_out:
                    oint = _dot(vw["qd"], state[d])
                    for h in range(nh):
                        vw["ol"][0, vw["rows"], h * B_DV:(h + 1) * B_DV] = vw["oi"][h] + oint[h * c:(h + 1) * c]
                state[d] = state[d] * vw["lgl"] + vw["upd"]
                if is_last:
                    st_ref[d] = state[d]
        return run

    return [prep] + ([level(lv) for lv in range(LEVELS + 1)] if need_out else []) + [intra], recur


def _scan_kernel(need_out, nchunks, *refs):
    n_in = 9
    n_c = 6
    k = 2 * n_in + 2 * n_c
    if need_out:
        s_in, st_in, og_f, og_b, ol_f, ol_b, s_ref, st_ref = refs[k:]
    else:
        s_out, st_out, s_ref, st_ref = refs[k:]
        og_f = og_b = ol_f = ol_b = None
    c = CHUNK

    @pl.when(pl.program_id(1) == 0)
    def _():
        if need_out:
            s_ref[...] = s_in[0]
            st_ref[...] = st_in[0]
        else:
            s_ref[...] = jnp.zeros_like(s_ref)
            st_ref[...] = jnp.zeros_like(st_ref)

    views = []
    for d, (og, ol) in enumerate(((og_f, ol_f), (og_b, ol_b))):
        aq, ak, av, sm, smT, bq, bk, bv, gk = refs[d * n_in:(d + 1) * n_in]
        tri, tri2, lmask2, lmaskv, incl2, strict2 = refs[2 * n_in + d * n_c:2 * n_in + (d + 1) * n_c]
        order = range(nchunks) if d == 0 else range(nchunks - 1, -1, -1)
        for pos, j in enumerate(order):
            rows = pl.ds(j * c, c)
            views.append(dict(d=d, pos=pos, rows=rows, aq=aq[0, rows, :], ak=ak[0, rows, :], av=av[0, rows, :],
                              sm=sm[0, rows, :], smT=smT[j], bq=bq[0, rows, :], bk=bk[0, rows, :],
                              bv=bv[0, rows, :], gk=gk[0, rows, :], tri=tri[...], tri2=tri2[...], lmask2=lmask2,
                              lmaskv=lmaskv, incl2=incl2[...], strict2=strict2[...], og=og, ol=ol))
    gdn, gdn_recur = _gdn_stages(views, s_ref, {}, need_out)
    gla, gla_recur = _gla_stages(views, st_ref, {}, need_out)
    for i in range(max(len(gdn), len(gla))):
        if i < len(gdn):
            gdn[i]()
        if i < len(gla):
            gla[i]()
    for pos in range(nchunks):
        gdn_recur(pos, pos == nchunks - 1)()
        gla_recur(pos, pos == nchunks - 1)()

    if not need_out:
        @pl.when(pl.program_id(1) == pl.num_programs(1) - 1)
        def _():
            s_out[0] = s_ref[...]
            st_out[0] = st_ref[...]


def _scans(aq, ak, av, sm, smT, bq, bk, bv, gk, ctx_len):
    bsz, ttot, _ = aq.shape
    seq = ttot - ctx_len
    c = CHUNK
    consts = _scan_consts()
    const = lambda a: pl.BlockSpec(a.shape, lambda b, s: tuple(0 for _ in a.shape))
    cargs = []
    for d in range(NDIR):
        cd = consts[d]
        cargs += [cd["tri"], cd["tri2"], cd["lmask2"], cd["lmaskv"], cd["incl2"], cd["strict2"]]
    ins = [aq, ak, av, sm, smT, bq, bk, bv, gk]
    s_shape = (NDIR, A_HEADS, A_DK, A_DV)
    st_shape = (NDIR, BQ_W, B_DV)
    state_specs = [pl.BlockSpec((1,) + s_shape, lambda b, s: (b, 0, 0, 0, 0)),
                   pl.BlockSpec((1,) + st_shape, lambda b, s: (b, 0, 0, 0))]
    state_shapes = [jax.ShapeDtypeStruct((bsz,) + s_shape, F32), jax.ShapeDtypeStruct((bsz,) + st_shape, F32)]
    scratch = [pltpu.VMEM(s_shape, F32), pltpu.VMEM(st_shape, F32)]
    params = pltpu.CompilerParams(dimension_semantics=("arbitrary", "arbitrary"), vmem_limit_bytes=VMEM_LIMIT)

    def in_specs(nchunks, first, n):
        tb = nchunks * c

        def tok(w, idx, lane_blk=0):
            return pl.BlockSpec((1, tb, w), lambda b, s: (b, idx(s), lane_blk))

        def dir_specs(idx, d):
            return [tok(A_W, idx), tok(A_W, idx), tok(A_W, idx), tok(SMALL_W, idx),
                    pl.BlockSpec((None, nchunks, 16, c), lambda b, s: (b, idx(s), 0, 0)),
                    tok(BQ_W, idx), tok(BQ_W, idx), tok(BV_W, idx), tok(BQ_W, idx, d)]

        return dir_specs(lambda s: first + s, 0) + dir_specs(lambda s: first + n - 1 - s, 1) + [const(a) for a in cargs]

    tbc = CTX_STEP_CHUNKS * c
    ncs = ctx_len // tbc
    s_ctx, st_ctx = pl.pallas_call(
        functools.partial(_scan_kernel, False, CTX_STEP_CHUNKS),
        grid=(bsz, ncs),
        in_specs=in_specs(CTX_STEP_CHUNKS, seq // tbc, ncs),
        out_specs=state_specs,
        out_shape=state_shapes,
        scratch_shapes=scratch,
        compiler_params=params,
        name="scan_ctx",
    )(*ins, *ins, *cargs)

    tb = STEP_CHUNKS * c
    nxs = seq // tb

    def otok(w, idx):
        return pl.BlockSpec((1, tb, w), lambda b, s: (b, idx(s), 0))

    out_w = [A_HEADS * A_DV, A_HEADS * A_DV, BV_W, BV_W]
    fwd, bwd = (lambda s: s), (lambda s: nxs - 1 - s)
    return pl.pallas_call(
        functools.partial(_scan_kernel, True, STEP_CHUNKS),
        grid=(bsz, nxs),
        in_specs=in_specs(STEP_CHUNKS, 0, nxs) + state_specs,
        out_specs=[otok(out_w[0], fwd), otok(out_w[1], bwd), otok(out_w[2], fwd), otok(out_w[3], bwd)],
        out_shape=[jax.ShapeDtypeStruct((bsz, seq, w), F32) for w in out_w],
        scratch_shapes=scratch,
        compiler_params=params,
        name="scan",
    )(*ins, *ins, *cargs, s_ctx, st_ctx)


def _head_norm(o, w, z, n_heads, dv):
    parts = []
    for h in range(n_heads):
        oh = o[:, h * dv:(h + 1) * dv]
        parts.append(oh * lax.rsqrt(jnp.mean(oh * oh, axis=-1, keepdims=True) + EPS))
    return jnp.concatenate(parts, axis=-1) * w * _silu(z)


def _rms(x, w):
    return x * lax.rsqrt(jnp.mean(x * x, axis=-1, keepdims=True) + EPS) * w


def _post_kernel(x_ref, ogf_ref, ogb_ref, olf_ref, olb_ref, az_ref, bg_ref,
                 g1_ref, sh2_ref, sc2_ref, g2_ref, gn_ref, ln_ref, wo_ref, fn_ref,
                 wg_ref, wu_ref, wd_ref, fin_ref, o_ref):
    gdn = _head_norm(ogf_ref[0] + ogb_ref[0], gn_ref[...], az_ref[0], A_HEADS, A_DV)
    gla = _head_norm(olf_ref[0] + olb_ref[0], ln_ref[...], bg_ref[0], B_HEADS, B_DV)
    mix = jnp.concatenate([gdn, gla], axis=-1)
    x1 = x_ref[0] + g1_ref[...] * _dot(mix, wo_ref[...])
    h2b = _bf(_rms(x1, fn_ref[...]) * (1.0 + sc2_ref[...]) + sh2_ref[...])
    dff = wg_ref.shape[1]
    y = None
    for c0 in range(0, dff, FFN_CHUNK):
        c1 = min(c0 + FFN_CHUNK, dff)
        gate = jnp.dot(h2b, wg_ref[:, c0:c1], preferred_element_type=F32)
        up = jnp.dot(h2b, wu_ref[:, c0:c1], preferred_element_type=F32)
        part = _dot(_silu(gate) * up, wd_ref[c0:c1, :])
        y = part if y is None else y + part
    x2 = x1 + g2_ref[...] * y
    o_ref[0] = _rms(x2, fin_ref[...])


def _post(x, ogf, ogb, olf, olb, az, bg, mod4, gn, ln, wo, fn, wg, wu, wd, fin):
    bsz, seq, d = x.shape
    tm = POST_TILE
    dff = wg.shape[1]

    def tok(w):
        return pl.BlockSpec((1, tm, w), lambda b, j: (b, j, 0))

    def modspec(k):
        return pl.BlockSpec((None, None, 1, d), lambda b, j: (b, k, 0, 0))

    def const(shape, single=True):
        idx = lambda b, j: tuple(0 for _ in shape)
        if single:
            return pl.BlockSpec(shape, idx, pipeline_mode=pl.Buffered(1))
        return pl.BlockSpec(shape, idx)

    widths = [d, A_W, A_W, BV_W, BV_W, A_W, BV_W]
    return pl.pallas_call(
        _post_kernel,
        grid=(bsz, seq // tm),
        in_specs=[tok(w) for w in widths] + [
                  modspec(2), modspec(3), modspec(4), modspec(5),
                  const((1, A_W), False), const((1, BV_W), False), const((d, d)), const((1, d), False),
                  const((d, dff)), const((d, dff)), const((dff, d)), const((1, d), False)],
        out_specs=tok(d),
        out_shape=jax.ShapeDtypeStruct((bsz, seq, d), F32),
        compiler_params=pltpu.CompilerParams(dimension_semantics=("arbitrary", "arbitrary"),
                                             vmem_limit_bytes=VMEM_LIMIT),
        name="post",
    )(x, ogf, ogb, olf, olb, az, bg, mod4, mod4, mod4, mod4, gn, ln, wo, fn, wg, wu, wd, fin)


def kernel(x, c, ctx, c_ctx, w_mod, b_mod, attn_norm, w_in, conv_w, a_log, dt_bias, gdn_norm, gla_w2, gla_b,
           gla_norm, w_out, ffn_norm, w_gate, w_up, w_down, final_norm):
    bsz, seq, d = x.shape
    ctx_len = ctx.shape[1]
    assert w_mod.shape[0] == 1, "single-layer block"
    assert ctx_len == TOK_TILE and seq % TOK_TILE == 0 and seq % POST_TILE == 0
    assert TOK_TILE % GRID_W == 0 and ctx_len % (CHUNK * CTX_STEP_CHUNKS) == 0 and seq % (CHUNK * STEP_CHUNKS) == 0

    rows = -(-(bsz + 1) // 8) * 8
    cc = jnp.zeros((rows, d), F32).at[:bsz].set(c).at[rows - 1].set(c_ctx)
    mod = _modulation(cc, w_mod[0], b_mod[0][None, :])
    mod4 = mod.reshape(rows, 6, 1, d)

    w = w_in[0]
    o1 = 4 * A_W
    nb = NDIR * A_HEADS
    o2 = o1 + 2 * nb
    o3 = o2 + 2 * BQ_W + 2 * BV_W
    nlr = NDIR * GLA_RANK
    win = jnp.concatenate([w[:, :o1], w[:, o2:o2 + BQ_W] * B_DK ** -0.5, w[:, o2 + BQ_W:o3], w[:, o1:o2],
                           w[:, o3:o3 + nlr], jnp.zeros((d, SMALL_W - 2 * nb - nlr), F32)], axis=1).astype(BF16)
    gparams = jnp.zeros((2, SMALL_W), F32)
    gparams = gparams.at[0, nb:2 * nb].set(dt_bias[0].reshape(-1)).at[1, nb:2 * nb].set(a_log[0].reshape(-1))
    w2 = jnp.zeros((SMALL_W, NDIR * BQ_W), F32)
    for n in range(NDIR):
        w2 = w2.at[2 * nb + n * GLA_RANK:2 * nb + (n + 1) * GLA_RANK, n * BQ_W:(n + 1) * BQ_W].set(gla_w2[0, n])
    w2h = w2.astype(BF16)
    w2m = (w2 - w2h.astype(F32)).astype(BF16)

    aq, ak, av, az, bq, bk, bv, bg, gk, sm, smT = _projection(
        x, ctx, mod4, attn_norm, win, conv_w[0], gparams, w2h, w2m, gla_b[0].reshape(1, -1))
    ogf, ogb, olf, olb = _scans(aq, ak, av, sm, smT, bq, bk, bv, gk, ctx_len)

    tile_w = lambda v, n: jnp.tile(v.reshape(1, -1), (1, n))
    return _post(x, ogf, ogb, olf, olb, az, bg, mod4,
                 tile_w(gdn_norm[0], A_HEADS), tile_w(gla_norm[0], B_HEADS),
                 w_out[0].astype(BF16), ffn_norm, w_gate[0].astype(BF16), w_up[0].astype(BF16),
                 w_down[0].astype(BF16), final_norm.reshape(1, -1))
```

```python
import functools

import numpy as np
import jax
import jax.numpy as jnp
from jax import lax
from jax.experimental import pallas as pl
from jax.experimental.pallas import tpu as pltpu

F32 = jnp.float32
BF16 = jnp.bfloat16

GRID_W = 64
A_HEADS, A_DK, A_DV = 4, 128, 128
B_HEADS, B_DK, B_DV = 4, 64, 128
NDIR = 2
GLA_RANK = 16
GLA_NORMALIZER = 16.0
EPS = 1e-6

CHUNK = 64
CTX_STEP_CHUNKS = 4
STEP_CHUNKS = 8
SUBLANES = 8
TOK_TILE = 256
POST_TILE = 512
FFN_CHUNK = 1024
SMALL_W = 128
LEVELS = int(np.log2(CHUNK))

A_W = A_HEADS * A_DK
BQ_W = B_HEADS * B_DK
BV_W = B_HEADS * B_DV
P_COLS = 4 * A_W + 2 * BQ_W + 2 * BV_W + SMALL_W

V7X_VMEM_BYTES = 64 * 1024 * 1024
VMEM_LIMIT = V7X_VMEM_BYTES - 8 * 1024 * 1024


def _bf(x):
    return x.astype(BF16)


def _dot(a, b):
    return jnp.dot(_bf(a), _bf(b), preferred_element_type=F32)


def _dot_nt(a, b):
    return lax.dot_general(_bf(a), _bf(b), (((1,), (1,)), ((), ())), preferred_element_type=F32)


def _dot_tn(a, b):
    return lax.dot_general(_bf(a), _bf(b), (((0,), (0,)), ((), ())), preferred_element_type=F32)


def _split3(x):
    hi = _bf(x)
    r1 = x - hi.astype(F32)
    mid = _bf(r1)
    lo = _bf(r1 - mid.astype(F32))
    return hi, mid, lo


def _dot_exact_lhs(m, x, terms=3):
    d = lambda p: jnp.dot(m, p, preferred_element_type=F32)
    parts = _split3(x)[:terms]
    acc = d(parts[0])
    for p in parts[1:]:
        acc = acc + d(p)
    return acc


def _dot_exact_rhs_nt(x, m):
    hi, mid, lo = _split3(x)
    d = lambda p: lax.dot_general(p, m, (((1,), (1,)), ((), ())), preferred_element_type=F32)
    return d(hi) + d(mid) + d(lo)


def _sigmoid(x):
    return 0.5 * jnp.tanh(0.5 * x) + 0.5


def _silu(x):
    h = 0.5 * x
    return h + h * jnp.tanh(h)


def _softplus(x):
    return jnp.maximum(x, 0.0) + jnp.log(1.0 + jnp.exp(-jnp.abs(x)))


def _later_rows(x, m, fwd):
    n = x.shape[0]
    return jnp.concatenate([x[s + m:s + 2 * m] if fwd else x[s:s + m] for s in range(0, n, 2 * m)], axis=0)


def _merge_later_rows(full, later, m, fwd):
    n = full.shape[0]
    pieces = []
    for i, s in enumerate(range(0, n, 2 * m)):
        lat = later[i * m:(i + 1) * m]
        pieces += [full[s:s + m], lat] if fwd else [lat, full[s + m:s + 2 * m]]
    return jnp.concatenate(pieces, axis=0)


def _mod_kernel(cc_ref, w_ref, b_ref, o_ref):
    o_ref[...] = _dot(_silu(cc_ref[...]), w_ref[...]) + b_ref[...]


def _modulation(cc, w_mod, b_mod):
    rows, d = cc.shape
    n = w_mod.shape[1]
    tn = 1536
    return pl.pallas_call(
        _mod_kernel,
        grid=(n // tn,),
        in_specs=[pl.BlockSpec((rows, d), lambda j: (0, 0)),
                  pl.BlockSpec((d, tn), lambda j: (0, j)),
                  pl.BlockSpec((1, tn), lambda j: (0, j))],
        out_specs=pl.BlockSpec((rows, tn), lambda j: (0, j)),
        out_shape=jax.ShapeDtypeStruct((rows, n), F32),
        compiler_params=pltpu.CompilerParams(dimension_semantics=("arbitrary",),
                                             vmem_limit_bytes=VMEM_LIMIT),
        name="mod",
    )(cc, w_mod, b_mod)


def _regroup_kernel(o1, o2, o3, nlr, w_ref, o_ref):
    nb2 = o2 - o1
    base = o1 + (o3 - o2)
    o_ref[:, :o1] = _bf(w_ref[:, :o1])
    o_ref[:, o1:o1 + BQ_W] = _bf(w_ref[:, o2:o2 + BQ_W] * B_DK ** -0.5)
    o_ref[:, o1 + BQ_W:base] = _bf(w_ref[:, o2 + BQ_W:o3])
    o_ref[:, base:] = jnp.zeros((o_ref.shape[0], SMALL_W), BF16)
    o_ref[:, base:base + nb2] = _bf(w_ref[:, o1:o2])
    o_ref[:, base + nb2:base + nb2 + nlr] = _bf(w_ref[:, o3:o3 + nlr])


def _regroup_weight(w, o1, o2, o3, nlr):
    d, n = w.shape
    tr = 256
    return pl.pallas_call(
        functools.partial(_regroup_kernel, o1, o2, o3, nlr),
        grid=(d // tr,),
        in_specs=[pl.BlockSpec((tr, n), lambda i: (i, 0))],
        out_specs=pl.BlockSpec((tr, P_COLS), lambda i: (i, 0)),
        out_shape=jax.ShapeDtypeStruct((d, P_COLS), BF16),
        compiler_params=pltpu.CompilerParams(dimension_semantics=("arbitrary",), vmem_limit_bytes=VMEM_LIMIT),
        name="regroup",
    )(w)


def _proj_kernel(n_ctx_tiles, ctx_len,
                 x_ref, ctx_ref, sh_ref, sc_ref, nw_ref, win_ref, cw_ref, gp_ref, w2h_ref, w2m_ref, gb_ref,
                 aq_ref, ak_ref, av_ref, az_ref, bq_ref, bk_ref, bv_ref, bg_ref, gk_ref, sm_ref, smt_ref):
    j = pl.program_id(1)
    is_ctx = j < n_ctx_tiles
    xin = jnp.where(is_ctx, ctx_ref[0], x_ref[0])
    ms = jnp.mean(xin * xin, axis=-1, keepdims=True)
    h = xin * lax.rsqrt(ms + EPS) * (nw_ref[...] * (1.0 + sc_ref[...])) + sh_ref[...]
    hb = _bf(h)
    tm = xin.shape[0]
    sub = SUBLANES
    sublane = lax.broadcasted_iota(jnp.int32, (1, sub, 1), 1)
    cw_half = 0.5 * cw_ref[...]

    def mm(c0, c1):
        return jnp.dot(hb, win_ref[:, c0:c1], preferred_element_type=F32)

    def conv_silu(u, c0):
        wu = u.shape[1]
        u3 = u.reshape(tm // sub, sub, wu)

        def shifted(down):
            rot = pltpu.roll(u3, 1 if down else sub - 1, 1)
            nt = tm // sub
            tiles_per_row = GRID_W // sub
            zero = jnp.zeros((1, sub, wu), F32)
            nbr = []
            for r in range(nt):
                src = r - 1 if down else r + 1
                row_start = (r if down else src) % tiles_per_row == 0
                if src < 0 or src >= nt:
                    nbr.append(zero)
                elif row_start:
                    nbr.append(jnp.where(is_ctx, rot[src:src + 1], 0.0))
                else:
                    nbr.append(rot[src:src + 1])
            edge = sublane == (0 if down else sub - 1)
            return jnp.where(edge, jnp.concatenate(nbr, axis=0), rot).reshape(tm, wu)

        w = cw_half[:, c0:c0 + wu]
        h = w[0:1] * shifted(True) + w[1:2] * u + w[2:3] * shifted(False)
        return h + h * jnp.tanh(h)

    def l2n(t, scale):
        parts = []
        for hd in range(A_HEADS):
            th = t[:, hd * A_DK:(hd + 1) * A_DK]
            parts.append(th * (lax.rsqrt(jnp.sum(th * th, axis=-1, keepdims=True) + EPS) * scale))
        return jnp.concatenate(parts, axis=-1)

    o = 3 * A_W
    uq = mm(0, A_W)
    p1 = mm(o, o + A_W + 2 * BQ_W)
    o += A_W + 2 * BQ_W
    uk = mm(A_W, 2 * A_W)
    p2 = mm(o, o + BV_W)
    o += BV_W
    uv = mm(2 * A_W, 3 * A_W)
    p3 = mm(o, P_COLS)
    aq_ref[0] = l2n(conv_silu(uq, 0), A_DK ** -0.5)
    az_ref[0] = p1[:, :A_W]
    bq_ref[0] = p1[:, A_W:A_W + BQ_W]
    bk_ref[0] = p1[:, A_W + BQ_W:]
    ak_ref[0] = l2n(conv_silu(uk, A_W), 1.0)
    bv_ref[0] = p2
    av_ref[0] = conv_silu(uv, 2 * A_W)
    bg_ref[0] = p3[:, :BV_W]

    ps = p3[:, BV_W:]
    gp = gp_ref[...]
    lane = lax.broadcasted_iota(jnp.int32, (1, SMALL_W), 1)
    beta = _sigmoid(ps)
    g = -jnp.exp(gp[1:2]) * _softplus(ps + gp[0:1])
    nb = NDIR * A_HEADS
    sm = jnp.where(lane < nb, beta, jnp.where(lane < 2 * nb, g, 0.0))
    sm_ref[0] = sm
    for jc in range(tm // CHUNK):
        smt_ref[0, jc] = sm[jc * CHUNK:(jc + 1) * CHUNK, :].T[:2 * nb, :]

    ph = _bf(ps)
    pm = _bf(ps - ph.astype(F32))
    d = lambda a, b: jnp.dot(a, b, preferred_element_type=F32)
    pre = d(ph, w2h_ref[...]) + d(ph, w2m_ref[...]) + d(pm, w2h_ref[...]) + gb_ref[...]
    gk_ref[0] = -_softplus(-pre) * (1.0 / GLA_NORMALIZER)


def _projection(x, ctx, mod4, attn_norm, win, conv_w, gparams, w2h, w2m, gla_b):
    bsz, seq, d = x.shape
    ctx_len = ctx.shape[1]
    tm = TOK_TILE
    nct = ctx_len // tm
    nxt = seq // tm
    ttot = ctx_len + seq
    mod_rows = mod4.shape[0]

    def tile(j):
        return jnp.where(j < nct, nxt + j, j - nct)

    def tok(w):
        return pl.BlockSpec((1, tm, w), lambda b, j: (b, tile(j), 0))

    cpt = tm // CHUNK
    nsm = 2 * NDIR * A_HEADS

    const = lambda shape: pl.BlockSpec(shape, lambda b, j: tuple(0 for _ in shape))
    widths = [A_W, A_W, A_W, A_W, BQ_W, BQ_W, BV_W, BV_W, NDIR * BQ_W, SMALL_W]
    return pl.pallas_call(
        functools.partial(_proj_kernel, nct, ctx_len),
        grid=(bsz, nct + nxt),
        in_specs=[
            pl.BlockSpec((1, tm, d), lambda b, j: (b, jnp.maximum(j - nct, 0), 0)),
            pl.BlockSpec((1, tm, d), lambda b, j: (b, jnp.minimum(j, nct - 1), 0)),
            pl.BlockSpec((None, None, 1, d), lambda b, j: (jnp.where(j < nct, mod_rows - 1, b), 0, 0, 0)),
            pl.BlockSpec((None, None, 1, d), lambda b, j: (jnp.where(j < nct, mod_rows - 1, b), 1, 0, 0)),
            const((1, d)), const((d, P_COLS)), const(conv_w.shape), const(gparams.shape),
            const(w2h.shape), const(w2m.shape), const(gla_b.shape),
        ],
        out_specs=[tok(w) for w in widths] + [
            pl.BlockSpec((1, cpt, nsm, CHUNK), lambda b, j: (b, tile(j), 0, 0))],
        out_shape=[jax.ShapeDtypeStruct((bsz, ttot, w), F32) for w in widths] + [
            jax.ShapeDtypeStruct((bsz, ttot // CHUNK, nsm, CHUNK), F32)],
        compiler_params=pltpu.CompilerParams(dimension_semantics=("arbitrary", "arbitrary"),
                                             vmem_limit_bytes=VMEM_LIMIT),
        name="proj",
    )(x, ctx, mod4, mod4, attn_norm, win, conv_w, gparams, w2h, w2m, gla_b)


def _scan_consts():
    c = CHUNK
    i = np.arange(c)[:, None]
    t = np.arange(c)[None, :]
    out = []
    for d in range(NDIR):
        incl = (t <= i) if d == 0 else (t >= i)
        strict = (t < i) if d == 0 else (t > i)
        tri = incl.astype(np.float32)
        masks = []
        for lv in range(LEVELS):
            m = c >> (lv + 1)
            start = (np.arange(c) // (2 * m)) * (2 * m)
            later = (np.arange(c) % (2 * m) >= m) if d == 0 else (np.arange(c) % (2 * m) < m)
            same = (start[:, None] == start[None, :])
            masks.append((same & later[:, None] & ~later[None, :]).astype(np.float32))
        masks.append(np.eye(c, dtype=np.float32))
        tile2 = lambda mk: np.tile(mk, (1, 2))
        out.append(dict(
            tri=jnp.asarray(tri, BF16),
            tri2=jnp.asarray(np.tile(tri, (2, 1)), BF16),
            lmask2=jnp.asarray(np.stack([tile2(mk) for mk in masks]), F32),
            lmaskv=jnp.asarray(np.stack([np.tile(mk, (B_HEADS, 1)) for mk in masks]), F32),
            incl2=jnp.asarray(tile2(incl.astype(np.float32))),
            strict2=jnp.asarray(tile2(strict.astype(np.float32))),
        ))
    return out


def _gdn_stages(views, s_ref, state, need_out):
    c = CHUNK
    nh = A_HEADS
    npair = nh // 2
    w2 = 2 * c
    first = lax.broadcasted_iota(jnp.int32, (1, w2), 1) < c
    first_k = lax.broadcasted_iota(jnp.int32, (1, 2 * A_DK), 1) < A_DK
    lane_k = lax.broadcasted_iota(jnp.int32, (1, A_W), 1) // A_DK
    km = [lane_k == h for h in range(nh)]

    def expand(cols):
        out = cols[nh - 1]
        for h in range(nh - 2, -1, -1):
            out = jnp.where(km[h], cols[h], out)
        return out

    def blockdiag(x):
        xb = _bf(x)
        return jnp.concatenate([jnp.where(first, xb, 0), jnp.where(first, 0, xb)], axis=0)

    def heads(x, w):
        return [x[:, h * w:(h + 1) * w] for h in range(nh)]

    def prep():
        for vw in views:
            d = vw["d"]
            last = c - 1 if d == 0 else 0
            base = NDIR * nh + nh * d
            sm = vw["sm"]
            gc_all = _dot_exact_lhs(vw["tri"], sm)
            gcr_all = _dot_exact_rhs_nt(vw["smT"], vw["tri2"])
            gcc = [gc_all[:, base + h:base + h + 1] for h in range(nh)]
            gtot = [gc_all[last:last + 1, base + h:base + h + 1] for h in range(nh)]
            beta = [sm[:, nh * d + h:nh * d + h + 1] for h in range(nh)]
            egc = [jnp.exp(g) for g in gcc]
            vw["gl"] = [jnp.exp(g) for g in gtot]
            inc = vw["incl2"] > 0.5
            ak, aq = vw["ak"], vw["aq"]
            beta_k = expand(beta)
            kb = ak * beta_k
            egc_k = expand(egc)
            kbq = _bf(jnp.concatenate([kb, aq], axis=0) if need_out else kb)
            akb = _bf(ak)
            vw["decay"], vw["kbq"], vw["kst"] = [], [], []
            for p in range(npair):
                h0, h1 = 2 * p, 2 * p + 1
                diff = (jnp.where(first, gcc[h0], gcc[h1])
                        - jnp.where(first, gcr_all[base + h0:base + h0 + 1, :], gcr_all[base + h1:base + h1 + 1, :]))
                vw["decay"].append(jnp.where(inc, jnp.exp(jnp.where(inc, diff, 0.0)), 0.0))
                kp = akb[:, 2 * A_DK * p:2 * A_DK * (p + 1)]
                vw["kst"].append(jnp.concatenate([jnp.where(first_k, kp, 0), jnp.where(first_k, 0, kp)], axis=0))
                vw["kbq"].append(kbq[:, 2 * A_DK * p:2 * A_DK * (p + 1)])
            vb = heads(vw["av"] * beta_k, A_DV)
            kbe = heads(kb * egc_k, A_DK)
            rhs = [jnp.concatenate([vb[h], kbe[h]], axis=1) for h in range(nh)]
            vw["rhs"] = [_bf(jnp.concatenate([rhs[2 * p], rhs[2 * p + 1]], axis=0)) for p in range(npair)]
            vw["qe"] = heads(aq * egc_k, A_DK) if need_out else None
            vw["kdec"] = heads(_bf(ak * expand([jnp.exp(gtot[h] - gcc[h]) for h in range(nh)])), A_DK)
        for vw in views:
            vw["m"] = [_dot_nt(vw["kbq"][p], vw["kst"][p]) for p in range(npair)]
        for vw in views:
            lm = vw["lmask2"]
            vw["a"] = [vw["m"][p][:c] * vw["decay"][p] * vw["strict2"] for p in range(npair)]
            if need_out:
                vw["qk"] = [blockdiag(vw["m"][p][c:] * vw["decay"][p]) for p in range(npair)]
            vw["t"] = [lm[LEVELS] - vw["a"][p] * lm[LEVELS - 1] for p in range(npair)]

    def level(lv):
        m = c >> (lv + 1)
        cut = m % SUBLANES == 0

        def run():
            for vw in views:
                fwd = vw["d"] == 0
                rows = (lambda x: _later_rows(x, m, fwd)) if cut else (lambda x: x)
                vw["tl"] = [_dot(rows(vw["t"][p]), blockdiag(vw["a"][p] * vw["lmask2"][lv])) for p in range(npair)]
            for vw in views:
                fwd = vw["d"] == 0
                for p in range(npair):
                    t = vw["t"][p]
                    upd = _dot(vw["tl"][p], blockdiag(t))
                    vw["t"][p] = _merge_later_rows(t, _later_rows(t, m, fwd) - upd, m, fwd) if cut else t - upd
        return run

    def solve():
        for vw in views:
            rp = [_dot(blockdiag(vw["t"][p]), vw["rhs"][p]) for p in range(npair)]
            vw["r"] = [rp[h // 2][(h % 2) * c:(h % 2 + 1) * c] for h in range(nh)]
        for vw in views:
            vw["lhs"] = [_bf(jnp.concatenate([vw["r"][h][:, A_DV:], vw["qe"][h]], axis=0) if need_out
                             else vw["r"][h][:, A_DV:]) for h in range(nh)]

    def recur(pos, is_last):
        def run():
            cur = [vw for vw in views if vw["pos"] == pos]
            for vw in cur:
                d = vw["d"]
                for h in range(nh):
                    if (d, h) not in state:
                        state[(d, h)] = s_ref[d, h]
                vw["ws"] = [_dot(vw["lhs"][h], state[(d, h)]) for h in range(nh)]
            for vw in cur:
                vw["vn"] = [vw["r"][h][:, :A_DV] - vw["ws"][h][:c] for h in range(nh)]
            for vw in cur if need_out else []:
                for p in range(npair):
                    vn2 = _bf(jnp.concatenate([vw["vn"][2 * p], vw["vn"][2 * p + 1]], axis=0))
                    oq = _dot(vw["qk"][p], vn2)
                    for e in range(2):
                        h = 2 * p + e
                        vw["og"][0, vw["rows"], h * A_DV:(h + 1) * A_DV] = vw["ws"][h][c:] + oq[e * c:(e + 1) * c]
            for vw in cur:
                d = vw["d"]
                for h in range(nh):
                    state[(d, h)] = state[(d, h)] * vw["gl"][h] + _dot_tn(vw["kdec"][h], vw["vn"][h])
                    if is_last:
                        s_ref[d, h] = state[(d, h)]
        return run

    return [prep] + [level(lv) for lv in range(LEVELS - 2, -1, -1)] + [solve], recur


def _gla_stages(views, st_ref, state, need_out):
    c = CHUNK
    nh = B_HEADS
    lane_q = lax.broadcasted_iota(jnp.int32, (1, BQ_W), 1) // B_DK
    row = lax.broadcasted_iota(jnp.int32, (c, 1), 0)

    def stack_heads(x):
        return jnp.concatenate([jnp.where(lane_q == h, x, 0) for h in range(nh)], axis=0)

    def level_abs(vw, lv):
        m = c >> (lv + 1)
        fwd = vw["d"] == 0
        bc, na = vw["bc"], vw["na"]
        if m >= 4:
            refs = [s + (m - 1 if fwd else m) for s in range(0, c, 2 * m)]
            ref_rows = jnp.concatenate([jnp.broadcast_to(bc[r:r + 1, :], (2 * m, BQ_W)) for r in refs], axis=0)
            return jnp.abs(bc - ref_rows)
        r = row % (2 * m)
        up = pltpu.roll(na, c - 1, 0)
        dn = pltpu.roll(na, 1, 0)
        if m == 2:
            if fwd:
                return jnp.where(r == 0, up, jnp.where(r == 1, 0.0, jnp.where(r == 2, na, na + dn)))
            return jnp.where(r == 0, na + up, jnp.where(r == 1, na, jnp.where(r == 2, 0.0, dn)))
        return jnp.where(r == (1 if fwd else 0), na, 0.0)

    def prep():
        for vw in views:
            vw["bc"] = _dot_exact_lhs(vw["tri"], vw["gk"])
            vw["na"] = -vw["gk"]
            vw["sacc"] = [jnp.zeros((c, c), F32) for _ in range(nh)]

    def level(lv):
        def run():
            for vw in views:
                m = c >> (lv + 1)
                fwd = vw["d"] == 0
                if lv < LEVELS:
                    wgt = jnp.exp(-level_abs(vw, lv))
                    ql, kl = _bf(vw["bq"] * wgt), _bf(vw["bk"] * wgt)
                else:
                    ql, kl = _bf(vw["bq"]), _bf(vw["bk"])
                if lv < LEVELS and m % SUBLANES == 0:
                    sl = _dot_nt(stack_heads(_later_rows(ql, m, fwd)), kl)
                    mask = _later_rows(vw["lmaskv"][lv, :c, :], m, fwd)
                    hc = c // 2
                    for h in range(nh):
                        old = vw["sacc"][h]
                        new = _later_rows(old, m, fwd) + sl[h * hc:(h + 1) * hc] * mask
                        vw["sacc"][h] = _merge_later_rows(old, new, m, fwd)
                else:
                    sl = _dot_nt(stack_heads(ql), kl) * vw["lmaskv"][lv]
                    vw["sacc"] = [vw["sacc"][h] + sl[h * c:(h + 1) * c] for h in range(nh)]
        return run

    def intra():
        for vw in views:
            last = c - 1 if vw["d"] == 0 else 0
            bc = vw["bc"]
            bct = bc[last:last + 1]
            bvh = [_bf(vw["bv"][:, h * B_DV:(h + 1) * B_DV]) for h in range(nh)]
            kd = vw["bk"] * jnp.exp(bct - bc)
            kdg = jnp.concatenate([kd, jnp.broadcast_to(jnp.exp(bct), (SUBLANES, BQ_W))], axis=0).T
            kdT = _bf(kdg[:, :c])
            vw["lgl"] = kdg[:, c:c + 1]
            if need_out:
                both = [_dot(jnp.concatenate([_bf(vw["sacc"][h]), kdT[h * B_DK:(h + 1) * B_DK]], axis=0), bvh[h])
                        for h in range(nh)]
                vw["oi"] = [b[:c] for b in both]
                vw["upd"] = jnp.concatenate([b[c:] for b in both], axis=0)
                vw["qd"] = stack_heads(_bf(vw["bq"] * jnp.exp(bc)))
            else:
                vw["upd"] = jnp.concatenate([_dot(kdT[h * B_DK:(h + 1) * B_DK], bvh[h]) for h in range(nh)], axis=0)

    def recur(pos, is_last):
        def run():
            for vw in views:
                if vw["pos"] != pos:
                    continue
                d = vw["d"]
                if d not in state:
                    state[d] = st_ref[d]
                if need_out:
                    oint = _dot(vw["qd"], state[d])
                    for h in range(nh):
                        vw["ol"][0, vw["rows"], h * B_DV:(h + 1) * B_DV] = vw["oi"][h] + oint[h * c:(h + 1) * c]
                state[d] = state[d] * vw["lgl"] + vw["upd"]
                if is_last:
                    st_ref[d] = state[d]
        return run

    return [prep] + ([level(lv) for lv in range(LEVELS + 1)] if need_out else []) + [intra], recur


def _scan_kernel(need_out, nchunks, *refs):
    n_in = 9
    n_c = 6
    k = 2 * n_in + 2 * n_c
    if need_out:
        s_in, st_in, og_f, og_b, ol_f, ol_b, s_ref, st_ref = refs[k:]
    else:
        s_out, st_out, s_ref, st_ref = refs[k:]
        og_f = og_b = ol_f = ol_b = None
    c = CHUNK

    @pl.when(pl.program_id(1) == 0)
    def _():
        if need_out:
            s_ref[...] = s_in[0]
            st_ref[...] = st_in[0]
        else:
            s_ref[...] = jnp.zeros_like(s_ref)
            st_ref[...] = jnp.zeros_like(st_ref)

    views = []
    for d, (og, ol) in enumerate(((og_f, ol_f), (og_b, ol_b))):
        aq, ak, av, sm, smT, bq, bk, bv, gk = refs[d * n_in:(d + 1) * n_in]
        tri, tri2, lmask2, lmaskv, incl2, strict2 = refs[2 * n_in + d * n_c:2 * n_in + (d + 1) * n_c]
        order = range(nchunks) if d == 0 else range(nchunks - 1, -1, -1)
        for pos, j in enumerate(order):
            rows = pl.ds(j * c, c)
            views.append(dict(d=d, pos=pos, rows=rows, aq=aq[0, rows, :], ak=ak[0, rows, :], av=av[0, rows, :],
                              sm=sm[0, rows, :], smT=smT[j], bq=bq[0, rows, :], bk=bk[0, rows, :],
                              bv=bv[0, rows, :], gk=gk[0, rows, :], tri=tri[...], tri2=tri2[...], lmask2=lmask2,
                              lmaskv=lmaskv, incl2=incl2[...], strict2=strict2[...], og=og, ol=ol))
    gdn, gdn_recur = _gdn_stages(views, s_ref, {}, need_out)
    gla, gla_recur = _gla_stages(views, st_ref, {}, need_out)
    for i in range(max(len(gdn), len(gla))):
        if i < len(gdn):
            gdn[i]()
        if i < len(gla):
            gla[i]()
    for pos in range(nchunks):
        gdn_recur(pos, pos == nchunks - 1)()
        gla_recur(pos, pos == nchunks - 1)()

    if not need_out:
        @pl.when(pl.program_id(1) == pl.num_programs(1) - 1)
        def _():
            s_out[0] = s_ref[...]
            st_out[0] = st_ref[...]


def _scans(aq, ak, av, sm, smT, bq, bk, bv, gk, ctx_len):
    bsz, ttot, _ = aq.shape
    seq = ttot - ctx_len
    c = CHUNK
    consts = _scan_consts()
    const = lambda a: pl.BlockSpec(a.shape, lambda b, s: tuple(0 for _ in a.shape))
    cargs = []
    for d in range(NDIR):
        cd = consts[d]
        cargs += [cd["tri"], cd["tri2"], cd["lmask2"], cd["lmaskv"], cd["incl2"], cd["strict2"]]
    ins = [aq, ak, av, sm, smT, bq, bk, bv, gk]
    s_shape = (NDIR, A_HEADS, A_DK, A_DV)
    st_shape = (NDIR, BQ_W, B_DV)
    state_specs = [pl.BlockSpec((1,) + s_shape, lambda b, s: (b, 0, 0, 0, 0)),
                   pl.BlockSpec((1,) + st_shape, lambda b, s: (b, 0, 0, 0))]
    state_shapes = [jax.ShapeDtypeStruct((bsz,) + s_shape, F32), jax.ShapeDtypeStruct((bsz,) + st_shape, F32)]
    scratch = [pltpu.VMEM(s_shape, F32), pltpu.VMEM(st_shape, F32)]
    params = pltpu.CompilerParams(dimension_semantics=("arbitrary", "arbitrary"), vmem_limit_bytes=VMEM_LIMIT)

    def in_specs(nchunks, first, n):
        tb = nchunks * c

        def tok(w, idx, lane_blk=0):
            return pl.BlockSpec((1, tb, w), lambda b, s: (b, idx(s), lane_blk))

        def dir_specs(idx, d):
            return [tok(A_W, idx), tok(A_W, idx), tok(A_W, idx), tok(SMALL_W, idx),
                    pl.BlockSpec((None, nchunks, 16, c), lambda b, s: (b, idx(s), 0, 0)),
                    tok(BQ_W, idx), tok(BQ_W, idx), tok(BV_W, idx), tok(BQ_W, idx, d)]

        return dir_specs(lambda s: first + s, 0) + dir_specs(lambda s: first + n - 1 - s, 1) + [const(a) for a in cargs]

    tbc = CTX_STEP_CHUNKS * c
    ncs = ctx_len // tbc
    s_ctx, st_ctx = pl.pallas_call(
        functools.partial(_scan_kernel, False, CTX_STEP_CHUNKS),
        grid=(bsz, ncs),
        in_specs=in_specs(CTX_STEP_CHUNKS, seq // tbc, ncs),
        out_specs=state_specs,
        out_shape=state_shapes,
        scratch_shapes=scratch,
        compiler_params=params,
        name="scan_ctx",
    )(*ins, *ins, *cargs)

    tb = STEP_CHUNKS * c
    nxs = seq // tb

    def otok(w, idx):
        return pl.BlockSpec((1, tb, w), lambda b, s: (b, idx(s), 0))

    out_w = [A_HEADS * A_DV, A_HEADS * A_DV, BV_W, BV_W]
    fwd, bwd = (lambda s: s), (lambda s: nxs - 1 - s)
    return pl.pallas_call(
        functools.partial(_scan_kernel, True, STEP_CHUNKS),
        grid=(bsz, nxs),
        in_specs=in_specs(STEP_CHUNKS, 0, nxs) + state_specs,
        out_specs=[otok(out_w[0], fwd), otok(out_w[1], bwd), otok(out_w[2], fwd), otok(out_w[3], bwd)],
        out_shape=[jax.ShapeDtypeStruct((bsz, seq, w), F32) for w in out_w],
        scratch_shapes=scratch,
        compiler_params=params,
        name="scan",
    )(*ins, *ins, *cargs, s_ctx, st_ctx)


def _head_norm(o, w, z, n_heads, dv):
    parts = []
    for h in range(n_heads):
        oh = o[:, h * dv:(h + 1) * dv]
        parts.append(oh * lax.rsqrt(jnp.mean(oh * oh, axis=-1, keepdims=True) + EPS))
    return jnp.concatenate(parts, axis=-1) * w * _silu(z)


def _rms(x, w):
    return x * lax.rsqrt(jnp.mean(x * x, axis=-1, keepdims=True) + EPS) * w


def _post_kernel(x_ref, ogf_ref, ogb_ref, olf_ref, olb_ref, az_ref, bg_ref,
                 g1_ref, sh2_ref, sc2_ref, g2_ref, gn_ref, ln_ref, wo_ref, fn_ref,
                 wg_ref, wu_ref, wd_ref, fin_ref, o_ref):
    gdn = _head_norm(ogf_ref[0] + ogb_ref[0], gn_ref[...], az_ref[0], A_HEADS, A_DV)
    gla = _head_norm(olf_ref[0] + olb_ref[0], ln_ref[...], bg_ref[0], B_HEADS, B_DV)
    mix = jnp.concatenate([gdn, gla], axis=-1)
    x1 = x_ref[0] + g1_ref[...] * _dot(mix, wo_ref[...])
    h2b = _bf(_rms(x1, fn_ref[...]) * (1.0 + sc2_ref[...]) + sh2_ref[...])
    dff = wg_ref.shape[1]
    y = None
    for c0 in range(0, dff, FFN_CHUNK):
        c1 = min(c0 + FFN_CHUNK, dff)
        gate = jnp.dot(h2b, wg_ref[:, c0:c1], preferred_element_type=F32)
        up = jnp.dot(h2b, wu_ref[:, c0:c1], preferred_element_type=F32)
        part = _dot(_silu(gate) * up, wd_ref[c0:c1, :])
        y = part if y is None else y + part
    x2 = x1 + g2_ref[...] * y
    o_ref[0] = _rms(x2, fin_ref[...])


def _post(x, ogf, ogb, olf, olb, az, bg, mod4, gn, ln, wo, fn, wg, wu, wd, fin):
    bsz, seq, d = x.shape
    tm = POST_TILE
    dff = wg.shape[1]

    def tok(w):
        return pl.BlockSpec((1, tm, w), lambda b, j: (b, j, 0))

    def modspec(k):
        return pl.BlockSpec((None, None, 1, d), lambda b, j: (b, k, 0, 0))

    def const(shape, single=True):
        idx = lambda b, j: tuple(0 for _ in shape)
        if single:
            return pl.BlockSpec(shape, idx, pipeline_mode=pl.Buffered(1))
        return pl.BlockSpec(shape, idx)

    widths = [d, A_W, A_W, BV_W, BV_W, A_W, BV_W]
    return pl.pallas_call(
        _post_kernel,
        grid=(bsz, seq // tm),
        in_specs=[tok(w) for w in widths] + [
                  modspec(2), modspec(3), modspec(4), modspec(5),
                  const((1, A_W), False), const((1, BV_W), False), const((d, d)), const((1, d), False),
                  const((d, dff)), const((d, dff)), const((dff, d)), const((1, d), False)],
        out_specs=tok(d),
        out_shape=jax.ShapeDtypeStruct((bsz, seq, d), F32),
        compiler_params=pltpu.CompilerParams(dimension_semantics=("arbitrary", "arbitrary"),
                                             vmem_limit_bytes=VMEM_LIMIT),
        name="post",
    )(x, ogf, ogb, olf, olb, az, bg, mod4, mod4, mod4, mod4, gn, ln, wo, fn, wg, wu, wd, fin)


def kernel(x, c, ctx, c_ctx, w_mod, b_mod, attn_norm, w_in, conv_w, a_log, dt_bias, gdn_norm, gla_w2, gla_b,
           gla_norm, w_out, ffn_norm, w_gate, w_up, w_down, final_norm):
    bsz, seq, d = x.shape
    ctx_len = ctx.shape[1]
    assert w_mod.shape[0] == 1, "single-layer block"
    assert ctx_len == TOK_TILE and seq % TOK_TILE == 0 and seq % POST_TILE == 0
    assert TOK_TILE % GRID_W == 0 and ctx_len % (CHUNK * CTX_STEP_CHUNKS) == 0 and seq % (CHUNK * STEP_CHUNKS) == 0

    rows = -(-(bsz + 1) // 8) * 8
    cc = jnp.zeros((rows, d), F32).at[:bsz].set(c).at[rows - 1].set(c_ctx)
    mod = _modulation(cc, w_mod[0], b_mod[0][None, :])
    mod4 = mod.reshape(rows, 6, 1, d)

    w = w_in[0]
    o1 = 4 * A_W
    nb = NDIR * A_HEADS
    o2 = o1 + 2 * nb
    o3 = o2 + 2 * BQ_W + 2 * BV_W
    nlr = NDIR * GLA_RANK
    win = _regroup_weight(w, o1, o2, o3, nlr)
    gparams = jnp.zeros((2, SMALL_W), F32)
    gparams = gparams.at[0, nb:2 * nb].set(dt_bias[0].reshape(-1)).at[1, nb:2 * nb].set(a_log[0].reshape(-1))
    w2 = jnp.zeros((SMALL_W, NDIR * BQ_W), F32)
    for n in range(NDIR):
        w2 = w2.at[2 * nb + n * GLA_RANK:2 * nb + (n + 1) * GLA_RANK, n * BQ_W:(n + 1) * BQ_W].set(gla_w2[0, n])
    w2h = w2.astype(BF16)
    w2m = (w2 - w2h.astype(F32)).astype(BF16)

    aq, ak, av, az, bq, bk, bv, bg, gk, sm, smT = _projection(
        x, ctx, mod4, attn_norm, win, conv_w[0], gparams, w2h, w2m, gla_b[0].reshape(1, -1))
    ogf, ogb, olf, olb = _scans(aq, ak, av, sm, smT, bq, bk, bv, gk, ctx_len)

    tile_w = lambda v, n: jnp.tile(v.reshape(1, -1), (1, n))
    return _post(x, ogf, ogb, olf, olb, az, bg, mod4,
                 tile_w(gdn_norm[0], A_HEADS), tile_w(gla_norm[0], B_HEADS),
                 w_out[0].astype(BF16), ffn_norm, w_gate[0].astype(BF16), w_up[0].astype(BF16),
                 w_down[0].astype(BF16), final_norm.reshape(1, -1))
```

```python
import functools

import numpy as np
import jax
import jax.numpy as jnp
from jax import lax
from jax.experimental import pallas as pl
from jax.experimental.pallas import tpu as pltpu

F32 = jnp.float32
BF16 = jnp.bfloat16

GRID_W = 64
A_HEADS, A_DK, A_DV = 4, 128, 128
B_HEADS, B_DK, B_DV = 4, 64, 128
NDIR = 2
GLA_RANK = 16
GLA_NORMALIZER = 16.0
EPS = 1e-6

CHUNK = 64
CTX_STEP_CHUNKS = 4
STEP_CHUNKS = 8
SUBLANES = 8
TOK_TILE = 256
POST_TILE = 512
FFN_CHUNK = 1024
SMALL_W = 128
LEVELS = int(np.log2(CHUNK))

A_W = A_HEADS * A_DK
BQ_W = B_HEADS * B_DK
BV_W = B_HEADS * B_DV
P_COLS = 4 * A_W + 2 * BQ_W + 2 * BV_W + SMALL_W

V7X_VMEM_BYTES = 64 * 1024 * 1024
VMEM_LIMIT = V7X_VMEM_BYTES - 8 * 1024 * 1024


def _bf(x):
    return x.astype(BF16)


def _dot(a, b):
    return jnp.dot(_bf(a), _bf(b), preferred_element_type=F32)


def _dot_nt(a, b):
    return lax.dot_general(_bf(a), _bf(b), (((1,), (1,)), ((), ())), preferred_element_type=F32)


def _dot_tn(a, b):
    return lax.dot_general(_bf(a), _bf(b), (((0,), (0,)), ((), ())), preferred_element_type=F32)


def _split3(x):
    hi = _bf(x)
    r1 = x - hi.astype(F32)
    mid = _bf(r1)
    lo = _bf(r1 - mid.astype(F32))
    return hi, mid, lo


def _dot_exact_lhs(m, x, terms=3):
    d = lambda p: jnp.dot(m, p, preferred_element_type=F32)
    parts = _split3(x)[:terms]
    acc = d(parts[0])
    for p in parts[1:]:
        acc = acc + d(p)
    return acc


def _dot_exact_rhs_nt(x, m):
    hi, mid, lo = _split3(x)
    d = lambda p: lax.dot_general(p, m, (((1,), (1,)), ((), ())), preferred_element_type=F32)
    return d(hi) + d(mid) + d(lo)


def _sigmoid(x):
    return 0.5 * jnp.tanh(0.5 * x) + 0.5


def _silu(x):
    h = 0.5 * x
    return h + h * jnp.tanh(h)


def _softplus(x):
    return jnp.maximum(x, 0.0) + jnp.log(1.0 + jnp.exp(-jnp.abs(x)))


def _later_rows(x, m, fwd):
    n = x.shape[0]
    return jnp.concatenate([x[s + m:s + 2 * m] if fwd else x[s:s + m] for s in range(0, n, 2 * m)], axis=0)


def _merge_later_rows(full, later, m, fwd):
    n = full.shape[0]
    pieces = []
    for i, s in enumerate(range(0, n, 2 * m)):
        lat = later[i * m:(i + 1) * m]
        pieces += [full[s:s + m], lat] if fwd else [lat, full[s + m:s + 2 * m]]
    return jnp.concatenate(pieces, axis=0)


def _mod_kernel(cc_ref, w_ref, b_ref, o_ref):
    o_ref[...] = _dot(_silu(cc_ref[...]), w_ref[...]) + b_ref[...]


def _modulation(cc, w_mod, b_mod):
    rows, d = cc.shape
    n = w_mod.shape[1]
    tn = 1536
    return pl.pallas_call(
        _mod_kernel,
        grid=(n // tn,),
        in_specs=[pl.BlockSpec((rows, d), lambda j: (0, 0)),
                  pl.BlockSpec((d, tn), lambda j: (0, j)),
                  pl.BlockSpec((1, tn), lambda j: (0, j))],
        out_specs=pl.BlockSpec((rows, tn), lambda j: (0, j)),
        out_shape=jax.ShapeDtypeStruct((rows, n), F32),
        compiler_params=pltpu.CompilerParams(dimension_semantics=("arbitrary",),
                                             vmem_limit_bytes=VMEM_LIMIT),
        name="mod",
    )(cc, w_mod, b_mod)


def _regroup_kernel(o1, o2, o3, nlr, wt_ref, o_ref):
    pad = jnp.zeros((SMALL_W - (o2 - o1) - nlr, wt_ref.shape[1]), F32)
    rows = jnp.concatenate([wt_ref[:o1, :], wt_ref[o2:o2 + BQ_W, :] * B_DK ** -0.5, wt_ref[o2 + BQ_W:o3, :],
                            wt_ref[o1:o2, :], wt_ref[o3:o3 + nlr, :], pad], axis=0)
    o_ref[...] = _bf(rows.T)


def _regroup_weight(wt, o1, o2, o3, nlr):
    n, d = wt.shape
    tr = 256
    return pl.pallas_call(
        functools.partial(_regroup_kernel, o1, o2, o3, nlr),
        grid=(d // tr,),
        in_specs=[pl.BlockSpec((n, tr), lambda i: (0, i))],
        out_specs=pl.BlockSpec((tr, P_COLS), lambda i: (i, 0)),
        out_shape=jax.ShapeDtypeStruct((d, P_COLS), BF16),
        compiler_params=pltpu.CompilerParams(dimension_semantics=("arbitrary",), vmem_limit_bytes=VMEM_LIMIT),
        name="regroup",
    )(wt)


def _proj_kernel(n_ctx_tiles, ctx_len,
                 x_ref, ctx_ref, sh_ref, sc_ref, nw_ref, win_ref, cw_ref, gp_ref, w2h_ref, w2m_ref, gb_ref,
                 aq_ref, ak_ref, av_ref, az_ref, bq_ref, bk_ref, bv_ref, bg_ref, gk_ref, sm_ref, smt_ref):
    j = pl.program_id(1)
    is_ctx = j < n_ctx_tiles
    xin = jnp.where(is_ctx, ctx_ref[0], x_ref[0])
    ms = jnp.mean(xin * xin, axis=-1, keepdims=True)
    h = xin * lax.rsqrt(ms + EPS) * (nw_ref[...] * (1.0 + sc_ref[...])) + sh_ref[...]
    hb = _bf(h)
    tm = xin.shape[0]
    sub = SUBLANES
    sublane = lax.broadcasted_iota(jnp.int32, (1, sub, 1), 1)
    cw_half = 0.5 * cw_ref[...]

    def mm(c0, c1):
        return jnp.dot(hb, win_ref[:, c0:c1], preferred_element_type=F32)

    def conv_silu(u, c0):
        wu = u.shape[1]
        u3 = u.reshape(tm // sub, sub, wu)

        def shifted(down):
            rot = pltpu.roll(u3, 1 if down else sub - 1, 1)
            nt = tm // sub
            tiles_per_row = GRID_W // sub
            zero = jnp.zeros((1, sub, wu), F32)
            nbr = []
            for r in range(nt):
                src = r - 1 if down else r + 1
                row_start = (r if down else src) % tiles_per_row == 0
                if src < 0 or src >= nt:
                    nbr.append(zero)
                elif row_start:
                    nbr.append(jnp.where(is_ctx, rot[src:src + 1], 0.0))
                else:
                    nbr.append(rot[src:src + 1])
            edge = sublane == (0 if down else sub - 1)
            return jnp.where(edge, jnp.concatenate(nbr, axis=0), rot).reshape(tm, wu)

        w = cw_half[:, c0:c0 + wu]
        h = w[0:1] * shifted(True) + w[1:2] * u + w[2:3] * shifted(False)
        return h + h * jnp.tanh(h)

    def l2n(t, scale):
        parts = []
        for hd in range(A_HEADS):
            th = t[:, hd * A_DK:(hd + 1) * A_DK]
            parts.append(th * (lax.rsqrt(jnp.sum(th * th, axis=-1, keepdims=True) + EPS) * scale))
        return jnp.concatenate(parts, axis=-1)

    o = 3 * A_W
    uq = mm(0, A_W)
    p1 = mm(o, o + A_W + 2 * BQ_W)
    o += A_W + 2 * BQ_W
    uk = mm(A_W, 2 * A_W)
    p2 = mm(o, o + BV_W)
    o += BV_W
    uv = mm(2 * A_W, 3 * A_W)
    p3 = mm(o, P_COLS)
    aq_ref[0] = l2n(conv_silu(uq, 0), A_DK ** -0.5)
    az_ref[0] = p1[:, :A_W]
    bq_ref[0] = p1[:, A_W:A_W + BQ_W]
    bk_ref[0] = p1[:, A_W + BQ_W:]
    ak_ref[0] = l2n(conv_silu(uk, A_W), 1.0)
    bv_ref[0] = p2
    av_ref[0] = conv_silu(uv, 2 * A_W)
    bg_ref[0] = p3[:, :BV_W]

    ps = p3[:, BV_W:]
    gp = gp_ref[...]
    lane = lax.broadcasted_iota(jnp.int32, (1, SMALL_W), 1)
    beta = _sigmoid(ps)
    g = -jnp.exp(gp[1:2]) * _softplus(ps + gp[0:1])
    nb = NDIR * A_HEADS
    sm = jnp.where(lane < nb, beta, jnp.where(lane < 2 * nb, g, 0.0))
    sm_ref[0] = sm
    for jc in range(tm // CHUNK):
        smt_ref[0, jc] = sm[jc * CHUNK:(jc + 1) * CHUNK, :].T[:2 * nb, :]

    ph = _bf(ps)
    pm = _bf(ps - ph.astype(F32))
    d = lambda a, b: jnp.dot(a, b, preferred_element_type=F32)
    pre = d(ph, w2h_ref[...]) + d(ph, w2m_ref[...]) + d(pm, w2h_ref[...]) + gb_ref[...]
    gk_ref[0] = -_softplus(-pre) * (1.0 / GLA_NORMALIZER)


def _projection(x, ctx, mod4, attn_norm, win, conv_w, gparams, w2h, w2m, gla_b):
    bsz, seq, d = x.shape
    ctx_len = ctx.shape[1]
    tm = TOK_TILE
    nct = ctx_len // tm
    nxt = seq // tm
    ttot = ctx_len + seq
    mod_rows = mod4.shape[0]

    def tile(j):
        return jnp.where(j < nct, nxt + j, j - nct)

    def tok(w):
        return pl.BlockSpec((1, tm, w), lambda b, j: (b, tile(j), 0))

    cpt = tm // CHUNK
    nsm = 2 * NDIR * A_HEADS

    const = lambda shape: pl.BlockSpec(shape, lambda b, j: tuple(0 for _ in shape))
    widths = [A_W, A_W, A_W, A_W, BQ_W, BQ_W, BV_W, BV_W, NDIR * BQ_W, SMALL_W]
    return pl.pallas_call(
        functools.partial(_proj_kernel, nct, ctx_len),
        grid=(bsz, nct + nxt),
        in_specs=[
            pl.BlockSpec((1, tm, d), lambda b, j: (b, jnp.maximum(j - nct, 0), 0)),
            pl.BlockSpec((1, tm, d), lambda b, j: (b, jnp.minimum(j, nct - 1), 0)),
            pl.BlockSpec((None, None, 1, d), lambda b, j: (jnp.where(j < nct, mod_rows - 1, b), 0, 0, 0)),
            pl.BlockSpec((None, None, 1, d), lambda b, j: (jnp.where(j < nct, mod_rows - 1, b), 1, 0, 0)),
            const((1, d)), const((d, P_COLS)), const(conv_w.shape), const(gparams.shape),
            const(w2h.shape), const(w2m.shape), const(gla_b.shape),
        ],
        out_specs=[tok(w) for w in widths] + [
            pl.BlockSpec((1, cpt, nsm, CHUNK), lambda b, j: (b, tile(j), 0, 0))],
        out_shape=[jax.ShapeDtypeStruct((bsz, ttot, w), F32) for w in widths] + [
            jax.ShapeDtypeStruct((bsz, ttot // CHUNK, nsm, CHUNK), F32)],
        compiler_params=pltpu.CompilerParams(dimension_semantics=("arbitrary", "arbitrary"),
                                             vmem_limit_bytes=VMEM_LIMIT),
        name="proj",
    )(x, ctx, mod4, mod4, attn_norm, win, conv_w, gparams, w2h, w2m, gla_b)


def _scan_consts():
    c = CHUNK
    i = np.arange(c)[:, None]
    t = np.arange(c)[None, :]
    out = []
    for d in range(NDIR):
        incl = (t <= i) if d == 0 else (t >= i)
        strict = (t < i) if d == 0 else (t > i)
        tri = incl.astype(np.float32)
        masks = []
        for lv in range(LEVELS):
            m = c >> (lv + 1)
            start = (np.arange(c) // (2 * m)) * (2 * m)
            later = (np.arange(c) % (2 * m) >= m) if d == 0 else (np.arange(c) % (2 * m) < m)
            same = (start[:, None] == start[None, :])
            masks.append((same & later[:, None] & ~later[None, :]).astype(np.float32))
        masks.append(np.eye(c, dtype=np.float32))
        tile2 = lambda mk: np.tile(mk, (1, 2))
        out.append(dict(
            tri=jnp.asarray(tri, BF16),
            tri2=jnp.asarray(np.tile(tri, (2, 1)), BF16),
            lmask2=jnp.asarray(np.stack([tile2(mk) for mk in masks]), F32),
            lmaskv=jnp.asarray(np.stack([np.tile(mk, (B_HEADS, 1)) for mk in masks]), F32),
            incl2=jnp.asarray(tile2(incl.astype(np.float32))),
            strict2=jnp.asarray(tile2(strict.astype(np.float32))),
        ))
    return out


def _gdn_stages(views, s_ref, state, need_out):
    c = CHUNK
    nh = A_HEADS
    npair = nh // 2
    w2 = 2 * c
    first = lax.broadcasted_iota(jnp.int32, (1, w2), 1) < c
    first_k = lax.broadcasted_iota(jnp.int32, (1, 2 * A_DK), 1) < A_DK
    lane_k = lax.broadcasted_iota(jnp.int32, (1, A_W), 1) // A_DK
    km = [lane_k == h for h in range(nh)]

    def expand(cols):
        out = cols[nh - 1]
        for h in range(nh - 2, -1, -1):
            out = jnp.where(km[h], cols[h], out)
        return out

    def blockdiag(x):
        xb = _bf(x)
        return jnp.concatenate([jnp.where(first, xb, 0), jnp.where(first, 0, xb)], axis=0)

    def heads(x, w):
        return [x[:, h * w:(h + 1) * w] for h in range(nh)]

    def prep():
        for vw in views:
            d = vw["d"]
            last = c - 1 if d == 0 else 0
            base = NDIR * nh + nh * d
            sm = vw["sm"]
            gc_all = _dot_exact_lhs(vw["tri"], sm)
            gcr_all = _dot_exact_rhs_nt(vw["smT"], vw["tri2"])
            gcc = [gc_all[:, base + h:base + h + 1] for h in range(nh)]
            gtot = [gc_all[last:last + 1, base + h:base + h + 1] for h in range(nh)]
            beta = [sm[:, nh * d + h:nh * d + h + 1] for h in range(nh)]
            egc = [jnp.exp(g) for g in gcc]
            vw["gl"] = [jnp.exp(g) for g in gtot]
            inc = vw["incl2"] > 0.5
            ak, aq = vw["ak"], vw["aq"]
            beta_k = expand(beta)
            kb = ak * beta_k
            egc_k = expand(egc)
            kbq = _bf(jnp.concatenate([kb, aq], axis=0) if need_out else kb)
            akb = _bf(ak)
            vw["decay"], vw["kbq"], vw["kst"] = [], [], []
            for p in range(npair):
                h0, h1 = 2 * p, 2 * p + 1
                diff = (jnp.where(first, gcc[h0], gcc[h1])
                        - jnp.where(first, gcr_all[base + h0:base + h0 + 1, :], gcr_all[base + h1:base + h1 + 1, :]))
                vw["decay"].append(jnp.where(inc, jnp.exp(jnp.where(inc, diff, 0.0)), 0.0))
                kp = akb[:, 2 * A_DK * p:2 * A_DK * (p + 1)]
                vw["kst"].append(jnp.concatenate([jnp.where(first_k, kp, 0), jnp.where(first_k, 0, kp)], axis=0))
                vw["kbq"].append(kbq[:, 2 * A_DK * p:2 * A_DK * (p + 1)])
            vb = heads(vw["av"] * beta_k, A_DV)
            kbe = heads(kb * egc_k, A_DK)
            rhs = [jnp.concatenate([vb[h], kbe[h]], axis=1) for h in range(nh)]
            vw["rhs"] = [_bf(jnp.concatenate([rhs[2 * p], rhs[2 * p + 1]], axis=0)) for p in range(npair)]
            vw["qe"] = heads(aq * egc_k, A_DK) if need_out else None
            vw["kdec"] = heads(_bf(ak * expand([jnp.exp(gtot[h] - gcc[h]) for h in range(nh)])), A_DK)
        for vw in views:
            vw["m"] = [_dot_nt(vw["kbq"][p], vw["kst"][p]) for p in range(npair)]
        for vw in views:
            lm = vw["lmask2"]
            vw["a"] = [vw["m"][p][:c] * vw["decay"][p] * vw["strict2"] for p in range(npair)]
            if need_out:
                vw["qk"] = [blockdiag(vw["m"][p][c:] * vw["decay"][p]) for p in range(npair)]
            vw["t"] = [lm[LEVELS] - vw["a"][p] * lm[LEVELS - 1] for p in range(npair)]

    def level(lv):
        m = c >> (lv + 1)
        cut = m % SUBLANES == 0

        def run():
            for vw in views:
                fwd = vw["d"] == 0
                rows = (lambda x: _later_rows(x, m, fwd)) if cut else (lambda x: x)
                vw["tl"] = [_dot(rows(vw["t"][p]), blockdiag(vw["a"][p] * vw["lmask2"][lv])) for p in range(npair)]
            for vw in views:
                fwd = vw["d"] == 0
                for p in range(npair):
                    t = vw["t"][p]
                    upd = _dot(vw["tl"][p], blockdiag(t))
                    vw["t"][p] = _merge_later_rows(t, _later_rows(t, m, fwd) - upd, m, fwd) if cut else t - upd
        return run

    def solve():
        for vw in views:
            rp = [_dot(blockdiag(vw["t"][p]), vw["rhs"][p]) for p in range(npair)]
            vw["r"] = [rp[h // 2][(h % 2) * c:(h % 2 + 1) * c] for h in range(nh)]
        for vw in views:
            vw["lhs"] = [_bf(jnp.concatenate([vw["r"][h][:, A_DV:], vw["qe"][h]], axis=0) if need_out
                             else vw["r"][h][:, A_DV:]) for h in range(nh)]

    def recur(pos, is_last):
        def run():
            cur = [vw for vw in views if vw["pos"] == pos]
            for vw in cur:
                d = vw["d"]
                for h in range(nh):
                    if (d, h) not in state:
                        state[(d, h)] = s_ref[d, h]
                vw["ws"] = [_dot(vw["lhs"][h], state[(d, h)]) for h in range(nh)]
            for vw in cur:
                vw["vn"] = [vw["r"][h][:, :A_DV] - vw["ws"][h][:c] for h in range(nh)]
            for vw in cur if need_out else []:
                for p in range(npair):
                    vn2 = _bf(jnp.concatenate([vw["vn"][2 * p], vw["vn"][2 * p + 1]], axis=0))
                    oq = _dot(vw["qk"][p], vn2)
                    for e in range(2):
                        h = 2 * p + e
                        vw["og"][0, vw["rows"], h * A_DV:(h + 1) * A_DV] = vw["ws"][h][c:] + oq[e * c:(e + 1) * c]
            for vw in cur:
                d = vw["d"]
                for h in range(nh):
                    state[(d, h)] = state[(d, h)] * vw["gl"][h] + _dot_tn(vw["kdec"][h], vw["vn"][h])
                    if is_last:
                        s_ref[d, h] = state[(d, h)]
        return run

    return [prep] + [level(lv) for lv in range(LEVELS - 2, -1, -1)] + [solve], recur


def _gla_stages(views, st_ref, state, need_out):
    c = CHUNK
    nh = B_HEADS
    lane_q = lax.broadcasted_iota(jnp.int32, (1, BQ_W), 1) // B_DK
    row = lax.broadcasted_iota(jnp.int32, (c, 1), 0)

    def stack_heads(x):
        return jnp.concatenate([jnp.where(lane_q == h, x, 0) for h in range(nh)], axis=0)

    def level_abs(vw, lv):
        m = c >> (lv + 1)
        fwd = vw["d"] == 0
        bc, na = vw["bc"], vw["na"]
        if m >= 4:
            refs = [s + (m - 1 if fwd else m) for s in range(0, c, 2 * m)]
            ref_rows = jnp.concatenate([jnp.broadcast_to(bc[r:r + 1, :], (2 * m, BQ_W)) for r in refs], axis=0)
            return jnp.abs(bc - ref_rows)
        r = row % (2 * m)
        up = pltpu.roll(na, c - 1, 0)
        dn = pltpu.roll(na, 1, 0)
        if m == 2:
            if fwd:
                return jnp.where(r == 0, up, jnp.where(r == 1, 0.0, jnp.where(r == 2, na, na + dn)))
            return jnp.where(r == 0, na + up, jnp.where(r == 1, na, jnp.where(r == 2, 0.0, dn)))
        return jnp.where(r == (1 if fwd else 0), na, 0.0)

    def prep():
        for vw in views:
            vw["bc"] = _dot_exact_lhs(vw["tri"], vw["gk"])
            vw["na"] = -vw["gk"]
            vw["sacc"] = [jnp.zeros((c, c), F32) for _ in range(nh)]

    def level(lv):
        def run():
            for vw in views:
                m = c >> (lv + 1)
                fwd = vw["d"] == 0
                if lv < LEVELS:
                    wgt = jnp.exp(-level_abs(vw, lv))
                    ql, kl = _bf(vw["bq"] * wgt), _bf(vw["bk"] * wgt)
                else:
                    ql, kl = _bf(vw["bq"]), _bf(vw["bk"])
                if lv < LEVELS and m % SUBLANES == 0:
                    sl = _dot_nt(stack_heads(_later_rows(ql, m, fwd)), kl)
                    mask = _later_rows(vw["lmaskv"][lv, :c, :], m, fwd)
                    hc = c // 2
                    for h in range(nh):
                        old = vw["sacc"][h]
                        new = _later_rows(old, m, fwd) + sl[h * hc:(h + 1) * hc] * mask
                        vw["sacc"][h] = _merge_later_rows(old, new, m, fwd)
                else:
                    sl = _dot_nt(stack_heads(ql), kl) * vw["lmaskv"][lv]
                    vw["sacc"] = [vw["sacc"][h] + sl[h * c:(h + 1) * c] for h in range(nh)]
        return run

    def intra():
        for vw in views:
            last = c - 1 if vw["d"] == 0 else 0
            bc = vw["bc"]
            bct = bc[last:last + 1]
            bvh = [_bf(vw["bv"][:, h * B_DV:(h + 1) * B_DV]) for h in range(nh)]
            kd = vw["bk"] * jnp.exp(bct - bc)
            kdg = jnp.concatenate([kd, jnp.broadcast_to(jnp.exp(bct), (SUBLANES, BQ_W))], axis=0).T
            kdT = _bf(kdg[:, :c])
            vw["lgl"] = kdg[:, c:c + 1]
            if need_out:
                both = [_dot(jnp.concatenate([_bf(vw["sacc"][h]), kdT[h * B_DK:(h + 1) * B_DK]], axis=0), bvh[h])
                        for h in range(nh)]
                vw["oi"] = [b[:c] for b in both]
                vw["upd"] = jnp.concatenate([b[c:] for b in both], axis=0)
                vw["qd"] = stack_heads(_bf(vw["bq"] * jnp.exp(bc)))
            else:
                vw["upd"] = jnp.concatenate([_dot(kdT[h * B_DK:(h + 1) * B_DK], bvh[h]) for h in range(nh)], axis=0)

    def recur(pos, is_last):
        def run():
            for vw in views:
                if vw["pos"] != pos:
                    continue
                d = vw["d"]
                if d not in state:
                    state[d] = st_ref[d]
                if need_out:
                    oint = _dot(vw["qd"], state[d])
                    for h in range(nh):
                        vw["ol"][0, vw["rows"], h * B_DV:(h + 1) * B_DV] = vw["oi"][h] + oint[h * c:(h + 1) * c]
                state[d] = state[d] * vw["lgl"] + vw["upd"]
                if is_last:
                    st_ref[d] = state[d]
        return run

    return [prep] + ([level(lv) for lv in range(LEVELS + 1)] if need_out else []) + [intra], recur


def _scan_kernel(need_out, nchunks, *refs):
    n_in = 9
    n_c = 6
    k = 2 * n_in + 2 * n_c
    if need_out:
        s_in, st_in, og_f, og_b, ol_f, ol_b, s_ref, st_ref = refs[k:]
    else:
        s_out, st_out, s_ref, st_ref = refs[k:]
        og_f = og_b = ol_f = ol_b = None
    c = CHUNK

    @pl.when(pl.program_id(1) == 0)
    def _():
        if need_out:
            s_ref[...] = s_in[0]
            st_ref[...] = st_in[0]
        else:
            s_ref[...] = jnp.zeros_like(s_ref)
            st_ref[...] = jnp.zeros_like(st_ref)

    views = []
    for d, (og, ol) in enumerate(((og_f, ol_f), (og_b, ol_b))):
        aq, ak, av, sm, smT, bq, bk, bv, gk = refs[d * n_in:(d + 1) * n_in]
        tri, tri2, lmask2, lmaskv, incl2, strict2 = refs[2 * n_in + d * n_c:2 * n_in + (d + 1) * n_c]
        order = range(nchunks) if d == 0 else range(nchunks - 1, -1, -1)
        for pos, j in enumerate(order):
            rows = pl.ds(j * c, c)
            views.append(dict(d=d, pos=pos, rows=rows, aq=aq[0, rows, :], ak=ak[0, rows, :], av=av[0, rows, :],
                              sm=sm[0, rows, :], smT=smT[j], bq=bq[0, rows, :], bk=bk[0, rows, :],
                              bv=bv[0, rows, :], gk=gk[0, rows, :], tri=tri[...], tri2=tri2[...], lmask2=lmask2,
                              lmaskv=lmaskv, incl2=incl2[...], strict2=strict2[...], og=og, ol=ol))
    gdn, gdn_recur = _gdn_stages(views, s_ref, {}, need_out)
    gla, gla_recur = _gla_stages(views, st_ref, {}, need_out)
    for i in range(max(len(gdn), len(gla))):
        if i < len(gdn):
            gdn[i]()
        if i < len(gla):
            gla[i]()
    for pos in range(nchunks):
        gdn_recur(pos, pos == nchunks - 1)()
        gla_recur(pos, pos == nchunks - 1)()

    if not need_out:
        @pl.when(pl.program_id(1) == pl.num_programs(1) - 1)
        def _():
            s_out[0] = s_ref[...]
            st_out[0] = st_ref[...]


def _scans(aq, ak, av, sm, smT, bq, bk, bv, gk, ctx_len):
    bsz, ttot, _ = aq.shape
    seq = ttot - ctx_len
    c = CHUNK
    consts = _scan_consts()
    const = lambda a: pl.BlockSpec(a.shape, lambda b, s: tuple(0 for _ in a.shape))
    cargs = []
    for d in range(NDIR):
        cd = consts[d]
        cargs += [cd["tri"], cd["tri2"], cd["lmask2"], cd["lmaskv"], cd["incl2"], cd["strict2"]]
    ins = [aq, ak, av, sm, smT, bq, bk, bv, gk]
    s_shape = (NDIR, A_HEADS, A_DK, A_DV)
    st_shape = (NDIR, BQ_W, B_DV)
    state_specs = [pl.BlockSpec((1,) + s_shape, lambda b, s: (b, 0, 0, 0, 0)),
                   pl.BlockSpec((1,) + st_shape, lambda b, s: (b, 0, 0, 0))]
    state_shapes = [jax.ShapeDtypeStruct((bsz,) + s_shape, F32), jax.ShapeDtypeStruct((bsz,) + st_shape, F32)]
    scratch = [pltpu.VMEM(s_shape, F32), pltpu.VMEM(st_shape, F32)]
    params = pltpu.CompilerParams(dimension_semantics=("arbitrary", "arbitrary"), vmem_limit_bytes=VMEM_LIMIT)

    def in_specs(nchunks, first, n):
        tb = nchunks * c

        def tok(w, idx, lane_blk=0):
            return pl.BlockSpec((1, tb, w), lambda b, s: (b, idx(s), lane_blk))

        def dir_specs(idx, d):
            return [tok(A_W, idx), tok(A_W, idx), tok(A_W, idx), tok(SMALL_W, idx),
                    pl.BlockSpec((None, nchunks, 16, c), lambda b, s: (b, idx(s), 0, 0)),
                    tok(BQ_W, idx), tok(BQ_W, idx), tok(BV_W, idx), tok(BQ_W, idx, d)]

        return dir_specs(lambda s: first + s, 0) + dir_specs(lambda s: first + n - 1 - s, 1) + [const(a) for a in cargs]

    tbc = CTX_STEP_CHUNKS * c
    ncs = ctx_len // tbc
    s_ctx, st_ctx = pl.pallas_call(
        functools.partial(_scan_kernel, False, CTX_STEP_CHUNKS),
        grid=(bsz, ncs),
        in_specs=in_specs(CTX_STEP_CHUNKS, seq // tbc, ncs),
        out_specs=state_specs,
        out_shape=state_shapes,
        scratch_shapes=scratch,
        compiler_params=params,
        name="scan_ctx",
    )(*ins, *ins, *cargs)

    tb = STEP_CHUNKS * c
    nxs = seq // tb

    def otok(w, idx):
        return pl.BlockSpec((1, tb, w), lambda b, s: (b, idx(s), 0))

    out_w = [A_HEADS * A_DV, A_HEADS * A_DV, BV_W, BV_W]
    fwd, bwd = (lambda s: s), (lambda s: nxs - 1 - s)
    return pl.pallas_call(
        functools.partial(_scan_kernel, True, STEP_CHUNKS),
        grid=(bsz, nxs),
        in_specs=in_specs(STEP_CHUNKS, 0, nxs) + state_specs,
        out_specs=[otok(out_w[0], fwd), otok(out_w[1], bwd), otok(out_w[2], fwd), otok(out_w[3], bwd)],
        out_shape=[jax.ShapeDtypeStruct((bsz, seq, w), F32) for w in out_w],
        scratch_shapes=scratch,
        compiler_params=params,
        name="scan",
    )(*ins, *ins, *cargs, s_ctx, st_ctx)


def _head_norm(o, w, z, n_heads, dv):
    parts = []
    for h in range(n_heads):
        oh = o[:, h * dv:(h + 1) * dv]
        parts.append(oh * lax.rsqrt(jnp.mean(oh * oh, axis=-1, keepdims=True) + EPS))
    return jnp.concatenate(parts, axis=-1) * w * _silu(z)


def _rms(x, w):
    return x * lax.rsqrt(jnp.mean(x * x, axis=-1, keepdims=True) + EPS) * w


def _post_kernel(x_ref, ogf_ref, ogb_ref, olf_ref, olb_ref, az_ref, bg_ref,
                 g1_ref, sh2_ref, sc2_ref, g2_ref, gn_ref, ln_ref, wo_ref, fn_ref,
                 wg_ref, wu_ref, wd_ref, fin_ref, o_ref):
    gdn = _head_norm(ogf_ref[0] + ogb_ref[0], gn_ref[...], az_ref[0], A_HEADS, A_DV)
    gla = _head_norm(olf_ref[0] + olb_ref[0], ln_ref[...], bg_ref[0], B_HEADS, B_DV)
    mix = jnp.concatenate([gdn, gla], axis=-1)
    x1 = x_ref[0] + g1_ref[...] * _dot(mix, wo_ref[...])
    h2b = _bf(_rms(x1, fn_ref[...]) * (1.0 + sc2_ref[...]) + sh2_ref[...])
    dff = wg_ref.shape[1]
    y = None
    for c0 in range(0, dff, FFN_CHUNK):
        c1 = min(c0 + FFN_CHUNK, dff)
        gate = jnp.dot(h2b, wg_ref[:, c0:c1], preferred_element_type=F32)
        up = jnp.dot(h2b, wu_ref[:, c0:c1], preferred_element_type=F32)
        part = _dot(_silu(gate) * up, wd_ref[c0:c1, :])
        y = part if y is None else y + part
    x2 = x1 + g2_ref[...] * y
    o_ref[0] = _rms(x2, fin_ref[...])


def _post(x, ogf, ogb, olf, olb, az, bg, mod4, gn, ln, wo, fn, wg, wu, wd, fin):
    bsz, seq, d = x.shape
    tm = POST_TILE
    dff = wg.shape[1]

    def tok(w):
        return pl.BlockSpec((1, tm, w), lambda b, j: (b, j, 0))

    def modspec(k):
        return pl.BlockSpec((None, None, 1, d), lambda b, j: (b, k, 0, 0))

    def const(shape, single=True):
        idx = lambda b, j: tuple(0 for _ in shape)
        if single:
            return pl.BlockSpec(shape, idx, pipeline_mode=pl.Buffered(1))
        return pl.BlockSpec(shape, idx)

    widths = [d, A_W, A_W, BV_W, BV_W, A_W, BV_W]
    return pl.pallas_call(
        _post_kernel,
        grid=(bsz, seq // tm),
        in_specs=[tok(w) for w in widths] + [
                  modspec(2), modspec(3), modspec(4), modspec(5),
                  const((1, A_W), False), const((1, BV_W), False), const((d, d)), const((1, d), False),
                  const((d, dff)), const((d, dff)), const((dff, d)), const((1, d), False)],
        out_specs=tok(d),
        out_shape=jax.ShapeDtypeStruct((bsz, seq, d), F32),
        compiler_params=pltpu.CompilerParams(dimension_semantics=("arbitrary", "arbitrary"),
                                             vmem_limit_bytes=VMEM_LIMIT),
        name="post",
    )(x, ogf, ogb, olf, olb, az, bg, mod4, mod4, mod4, mod4, gn, ln, wo, fn, wg, wu, wd, fin)


def kernel(x, c, ctx, c_ctx, w_mod, b_mod, attn_norm, w_in, conv_w, a_log, dt_bias, gdn_norm, gla_w2, gla_b,
           gla_norm, w_out, ffn_norm, w_gate, w_up, w_down, final_norm):
    bsz, seq, d = x.shape
    ctx_len = ctx.shape[1]
    assert w_mod.shape[0] == 1, "single-layer block"
    assert ctx_len == TOK_TILE and seq % TOK_TILE == 0 and seq % POST_TILE == 0
    assert TOK_TILE % GRID_W == 0 and ctx_len % (CHUNK * CTX_STEP_CHUNKS) == 0 and seq % (CHUNK * STEP_CHUNKS) == 0

    rows = -(-(bsz + 1) // 8) * 8
    cc = jnp.zeros((rows, d), F32).at[:bsz].set(c).at[rows - 1].set(c_ctx)
    mod = _modulation(cc, w_mod[0], b_mod[0][None, :])
    mod4 = mod.reshape(rows, 6, 1, d)

    w = w_in[0]
    o1 = 4 * A_W
    nb = NDIR * A_HEADS
    o2 = o1 + 2 * nb
    o3 = o2 + 2 * BQ_W + 2 * BV_W
    nlr = NDIR * GLA_RANK
    win = _regroup_weight(w.T, o1, o2, o3, nlr)
    gparams = jnp.zeros((2, SMALL_W), F32)
    gparams = gparams.at[0, nb:2 * nb].set(dt_bias[0].reshape(-1)).at[1, nb:2 * nb].set(a_log[0].reshape(-1))
    w2 = jnp.zeros((SMALL_W, NDIR * BQ_W), F32)
    for n in range(NDIR):
        w2 = w2.at[2 * nb + n * GLA_RANK:2 * nb + (n + 1) * GLA_RANK, n * BQ_W:(n + 1) * BQ_W].set(gla_w2[0, n])
    w2h = w2.astype(BF16)
    w2m = (w2 - w2h.astype(F32)).astype(BF16)

    aq, ak, av, az, bq, bk, bv, bg, gk, sm, smT = _projection(
        x, ctx, mod4, attn_norm, win, conv_w[0], gparams, w2h, w2m, gla_b[0].reshape(1, -1))
    ogf, ogb, olf, olb = _scans(aq, ak, av, sm, smT, bq, bk, bv, gk, ctx_len)

    tile_w = lambda v, n: jnp.tile(v.reshape(1, -1), (1, n))
    return _post(x, ogf, ogb, olf, olb, az, bg, mod4,
                 tile_w(gdn_norm[0], A_HEADS), tile_w(gla_norm[0], B_HEADS),
                 w_out[0].astype(BF16), ffn_norm, w_gate[0].astype(BF16), w_up[0].astype(BF16),
                 w_down[0].astype(BF16), final_norm.reshape(1, -1))
```

```python
import functools

import numpy as np
import jax
import jax.numpy as jnp
from jax import lax
from jax.experimental import pallas as pl
from jax.experimental.pallas import tpu as pltpu

F32 = jnp.float32
BF16 = jnp.bfloat16

GRID_W = 64
A_HEADS, A_DK, A_DV = 4, 128, 128
B_HEADS, B_DK, B_DV = 4, 64, 128
NDIR = 2
GLA_RANK = 16
GLA_NORMALIZER = 16.0
EPS = 1e-6

CHUNK = 64
CTX_STEP_CHUNKS = 4
STEP_CHUNKS = 8
SUBLANES = 8
TOK_TILE = 256
POST_TILE = 512
FFN_CHUNK = 1024
SMALL_W = 128
LEVELS = int(np.log2(CHUNK))

A_W = A_HEADS * A_DK
BQ_W = B_HEADS * B_DK
BV_W = B_HEADS * B_DV
P_COLS = 4 * A_W + 2 * BQ_W + 2 * BV_W + SMALL_W

V7X_VMEM_BYTES = 64 * 1024 * 1024
VMEM_LIMIT = V7X_VMEM_BYTES - 4 * 1024 * 1024


def _bf(x):
    return x.astype(BF16)


def _dot(a, b):
    return jnp.dot(_bf(a), _bf(b), preferred_element_type=F32)


def _dot_nt(a, b):
    return lax.dot_general(_bf(a), _bf(b), (((1,), (1,)), ((), ())), preferred_element_type=F32)


def _dot_tn(a, b):
    return lax.dot_general(_bf(a), _bf(b), (((0,), (0,)), ((), ())), preferred_element_type=F32)


def _split3(x):
    hi = _bf(x)
    r1 = x - hi.astype(F32)
    mid = _bf(r1)
    lo = _bf(r1 - mid.astype(F32))
    return hi, mid, lo


def _dot_exact_lhs(m, x, terms=3):
    d = lambda p: jnp.dot(m, p, preferred_element_type=F32)
    parts = _split3(x)[:terms]
    acc = d(parts[0])
    for p in parts[1:]:
        acc = acc + d(p)
    return acc


def _dot_exact_rhs_nt(x, m):
    hi, mid, lo = _split3(x)
    d = lambda p: lax.dot_general(p, m, (((1,), (1,)), ((), ())), preferred_element_type=F32)
    return d(hi) + d(mid) + d(lo)


def _sigmoid(x):
    return 0.5 * jnp.tanh(0.5 * x) + 0.5


def _silu(x):
    h = 0.5 * x
    return h + h * jnp.tanh(h)


def _softplus(x):
    return jnp.maximum(x, 0.0) + jnp.log(1.0 + jnp.exp(-jnp.abs(x)))


def _later_rows(x, m, fwd):
    n = x.shape[0]
    return jnp.concatenate([x[s + m:s + 2 * m] if fwd else x[s:s + m] for s in range(0, n, 2 * m)], axis=0)


def _merge_later_rows(full, later, m, fwd):
    n = full.shape[0]
    pieces = []
    for i, s in enumerate(range(0, n, 2 * m)):
        lat = later[i * m:(i + 1) * m]
        pieces += [full[s:s + m], lat] if fwd else [lat, full[s + m:s + 2 * m]]
    return jnp.concatenate(pieces, axis=0)


def _mod_kernel(cc_ref, w_ref, b_ref, o_ref):
    o_ref[...] = _dot(_silu(cc_ref[...]), w_ref[...]) + b_ref[...]


def _modulation(cc, w_mod, b_mod):
    rows, d = cc.shape
    n = w_mod.shape[1]
    tn = 1536
    return pl.pallas_call(
        _mod_kernel,
        grid=(n // tn,),
        in_specs=[pl.BlockSpec((rows, d), lambda j: (0, 0)),
                  pl.BlockSpec((d, tn), lambda j: (0, j)),
                  pl.BlockSpec((1, tn), lambda j: (0, j))],
        out_specs=pl.BlockSpec((rows, tn), lambda j: (0, j)),
        out_shape=jax.ShapeDtypeStruct((rows, n), F32),
        compiler_params=pltpu.CompilerParams(dimension_semantics=("arbitrary",),
                                             vmem_limit_bytes=VMEM_LIMIT),
        name="mod",
    )(cc, w_mod, b_mod)


def _regroup_kernel(o1, o2, o3, nlr, wt_ref, o_ref):
    pad = jnp.zeros((SMALL_W - (o2 - o1) - nlr, wt_ref.shape[1]), F32)
    rows = jnp.concatenate([wt_ref[:o1, :], wt_ref[o2:o2 + BQ_W, :] * B_DK ** -0.5, wt_ref[o2 + BQ_W:o3, :],
                            wt_ref[o1:o2, :], wt_ref[o3:o3 + nlr, :], pad], axis=0)
    o_ref[...] = _bf(rows.T)


def _regroup_weight(wt, o1, o2, o3, nlr):
    n, d = wt.shape
    tr = 256
    return pl.pallas_call(
        functools.partial(_regroup_kernel, o1, o2, o3, nlr),
        grid=(d // tr,),
        in_specs=[pl.BlockSpec((n, tr), lambda i: (0, i))],
        out_specs=pl.BlockSpec((tr, P_COLS), lambda i: (i, 0)),
        out_shape=jax.ShapeDtypeStruct((d, P_COLS), BF16),
        compiler_params=pltpu.CompilerParams(dimension_semantics=("arbitrary",), vmem_limit_bytes=VMEM_LIMIT),
        name="regroup",
    )(wt)


def _proj_kernel(n_ctx_tiles, ctx_len,
                 x_ref, ctx_ref, sh_ref, sc_ref, nw_ref, win_ref, cw_ref, gp_ref, w2h_ref, w2m_ref, gb_ref,
                 aq_ref, ak_ref, av_ref, az_ref, bq_ref, bk_ref, bv_ref, bg_ref, gk_ref, sm_ref, smt_ref):
    j = pl.program_id(1)
    is_ctx = j < n_ctx_tiles
    xin = jnp.where(is_ctx, ctx_ref[0], x_ref[0])
    ms = jnp.mean(xin * xin, axis=-1, keepdims=True)
    h = xin * lax.rsqrt(ms + EPS) * (nw_ref[...] * (1.0 + sc_ref[...])) + sh_ref[...]
    hb = _bf(h)
    tm = xin.shape[0]
    sub = SUBLANES
    sublane = lax.broadcasted_iota(jnp.int32, (1, sub, 1), 1)
    cw_half = 0.5 * cw_ref[...]

    def mm(c0, c1):
        return jnp.dot(hb, win_ref[:, c0:c1], preferred_element_type=F32)

    def conv_silu(u, c0):
        wu = u.shape[1]
        u3 = u.reshape(tm // sub, sub, wu)

        def shifted(down):
            rot = pltpu.roll(u3, 1 if down else sub - 1, 1)
            nt = tm // sub
            tiles_per_row = GRID_W // sub
            zero = jnp.zeros((1, sub, wu), F32)
            nbr = []
            for r in range(nt):
                src = r - 1 if down else r + 1
                row_start = (r if down else src) % tiles_per_row == 0
                if src < 0 or src >= nt:
                    nbr.append(zero)
                elif row_start:
                    nbr.append(jnp.where(is_ctx, rot[src:src + 1], 0.0))
                else:
                    nbr.append(rot[src:src + 1])
            edge = sublane == (0 if down else sub - 1)
            return jnp.where(edge, jnp.concatenate(nbr, axis=0), rot).reshape(tm, wu)

        w = cw_half[:, c0:c0 + wu]
        h = w[0:1] * shifted(True) + w[1:2] * u + w[2:3] * shifted(False)
        return h + h * jnp.tanh(h)

    def l2n(t, scale):
        parts = []
        for hd in range(A_HEADS):
            th = t[:, hd * A_DK:(hd + 1) * A_DK]
            parts.append(th * (lax.rsqrt(jnp.sum(th * th, axis=-1, keepdims=True) + EPS) * scale))
        return jnp.concatenate(parts, axis=-1)

    o = 3 * A_W
    uq = mm(0, A_W)
    p1 = mm(o, o + A_W + 2 * BQ_W)
    o += A_W + 2 * BQ_W
    uk = mm(A_W, 2 * A_W)
    p2 = mm(o, o + BV_W)
    o += BV_W
    uv = mm(2 * A_W, 3 * A_W)
    p3 = mm(o, P_COLS)
    aq_ref[0] = l2n(conv_silu(uq, 0), A_DK ** -0.5)
    az_ref[0] = p1[:, :A_W]
    bq_ref[0] = p1[:, A_W:A_W + BQ_W]
    bk_ref[0] = p1[:, A_W + BQ_W:]
    ak_ref[0] = l2n(conv_silu(uk, A_W), 1.0)
    bv_ref[0] = p2
    av_ref[0] = conv_silu(uv, 2 * A_W)
    bg_ref[0] = p3[:, :BV_W]

    ps = p3[:, BV_W:]
    gp = gp_ref[...]
    lane = lax.broadcasted_iota(jnp.int32, (1, SMALL_W), 1)
    beta = _sigmoid(ps)
    g = -jnp.exp(gp[1:2]) * _softplus(ps + gp[0:1])
    nb = NDIR * A_HEADS
    sm = jnp.where(lane < nb, beta, jnp.where(lane < 2 * nb, g, 0.0))
    sm_ref[0] = sm
    for jc in range(tm // CHUNK):
        smt_ref[0, jc] = sm[jc * CHUNK:(jc + 1) * CHUNK, :].T[:2 * nb, :]

    ph = _bf(ps)
    pm = _bf(ps - ph.astype(F32))
    d = lambda a, b: jnp.dot(a, b, preferred_element_type=F32)
    pre = d(ph, w2h_ref[...]) + d(ph, w2m_ref[...]) + d(pm, w2h_ref[...]) + gb_ref[...]
    gk_ref[0] = -_softplus(-pre) * (1.0 / GLA_NORMALIZER)


def _projection(x, ctx, mod4, attn_norm, win, conv_w, gparams, w2h, w2m, gla_b):
    bsz, seq, d = x.shape
    ctx_len = ctx.shape[1]
    tm = TOK_TILE
    nct = ctx_len // tm
    nxt = seq // tm
    ttot = ctx_len + seq
    mod_rows = mod4.shape[0]

    def tile(j):
        return jnp.where(j < nct, nxt + j, j - nct)

    def tok(w):
        return pl.BlockSpec((1, tm, w), lambda b, j: (b, tile(j), 0))

    cpt = tm // CHUNK
    nsm = 2 * NDIR * A_HEADS

    const = lambda shape: pl.BlockSpec(shape, lambda b, j: tuple(0 for _ in shape))
    widths = [A_W, A_W, A_W, A_W, BQ_W, BQ_W, BV_W, BV_W, NDIR * BQ_W, SMALL_W]
    return pl.pallas_call(
        functools.partial(_proj_kernel, nct, ctx_len),
        grid=(bsz, nct + nxt),
        in_specs=[
            pl.BlockSpec((1, tm, d), lambda b, j: (b, jnp.maximum(j - nct, 0), 0)),
            pl.BlockSpec((1, tm, d), lambda b, j: (b, jnp.minimum(j, nct - 1), 0)),
            pl.BlockSpec((None, None, 1, d), lambda b, j: (jnp.where(j < nct, mod_rows - 1, b), 0, 0, 0)),
            pl.BlockSpec((None, None, 1, d), lambda b, j: (jnp.where(j < nct, mod_rows - 1, b), 1, 0, 0)),
            const((1, d)), const((d, P_COLS)), const(conv_w.shape), const(gparams.shape),
            const(w2h.shape), const(w2m.shape), const(gla_b.shape),
        ],
        out_specs=[tok(w) for w in widths] + [
            pl.BlockSpec((1, cpt, nsm, CHUNK), lambda b, j: (b, tile(j), 0, 0))],
        out_shape=[jax.ShapeDtypeStruct((bsz, ttot, w), F32) for w in widths] + [
            jax.ShapeDtypeStruct((bsz, ttot // CHUNK, nsm, CHUNK), F32)],
        compiler_params=pltpu.CompilerParams(dimension_semantics=("arbitrary", "arbitrary"),
                                             vmem_limit_bytes=VMEM_LIMIT),
        name="proj",
    )(x, ctx, mod4, mod4, attn_norm, win, conv_w, gparams, w2h, w2m, gla_b)


def _scan_consts():
    c = CHUNK
    i = np.arange(c)[:, None]
    t = np.arange(c)[None, :]
    out = []
    for d in range(NDIR):
        incl = (t <= i) if d == 0 else (t >= i)
        strict = (t < i) if d == 0 else (t > i)
        tri = incl.astype(np.float32)
        masks = []
        for lv in range(LEVELS):
            m = c >> (lv + 1)
            start = (np.arange(c) // (2 * m)) * (2 * m)
            later = (np.arange(c) % (2 * m) >= m) if d == 0 else (np.arange(c) % (2 * m) < m)
            same = (start[:, None] == start[None, :])
            masks.append((same & later[:, None] & ~later[None, :]).astype(np.float32))
        masks.append(np.eye(c, dtype=np.float32))
        tile2 = lambda mk: np.tile(mk, (1, 2))
        out.append(dict(
            tri=jnp.asarray(tri, BF16),
            tri2=jnp.asarray(np.tile(tri, (2, 1)), BF16),
            lmask2=jnp.asarray(np.stack([tile2(mk) for mk in masks]), F32),
            lmaskv=jnp.asarray(np.stack([np.tile(mk, (B_HEADS, 1)) for mk in masks]), F32),
            incl2=jnp.asarray(tile2(incl.astype(np.float32))),
            strict2=jnp.asarray(tile2(strict.astype(np.float32))),
        ))
    return out


def _gdn_stages(views, s_ref, state, need_out):
    c = CHUNK
    nh = A_HEADS
    npair = nh // 2
    w2 = 2 * c
    first = lax.broadcasted_iota(jnp.int32, (1, w2), 1) < c
    first_k = lax.broadcasted_iota(jnp.int32, (1, 2 * A_DK), 1) < A_DK
    lane_k = lax.broadcasted_iota(jnp.int32, (1, A_W), 1) // A_DK
    km = [lane_k == h for h in range(nh)]

    def expand(cols):
        out = cols[nh - 1]
        for h in range(nh - 2, -1, -1):
            out = jnp.where(km[h], cols[h], out)
        return out

    def blockdiag(x):
        xb = _bf(x)
        return jnp.concatenate([jnp.where(first, xb, 0), jnp.where(first, 0, xb)], axis=0)

    def heads(x, w):
        return [x[:, h * w:(h + 1) * w] for h in range(nh)]

    def prep():
        for vw in views:
            d = vw["d"]
            last = c - 1 if d == 0 else 0
            base = NDIR * nh + nh * d
            sm = vw["sm"]
            gc_all = _dot_exact_lhs(vw["tri"], sm)
            gcr_all = _dot_exact_rhs_nt(vw["smT"], vw["tri2"])
            gcc = [gc_all[:, base + h:base + h + 1] for h in range(nh)]
            gtot = [gc_all[last:last + 1, base + h:base + h + 1] for h in range(nh)]
            beta = [sm[:, nh * d + h:nh * d + h + 1] for h in range(nh)]
            egc = [jnp.exp(g) for g in gcc]
            vw["gl"] = [jnp.exp(g) for g in gtot]
            inc = vw["incl2"] > 0.5
            ak, aq = vw["ak"], vw["aq"]
            beta_k = expand(beta)
            kb = ak * beta_k
            egc_k = expand(egc)
            kbq = _bf(jnp.concatenate([kb, aq], axis=0) if need_out else kb)
            akb = _bf(ak)
            vw["decay"], vw["kbq"], vw["kst"] = [], [], []
            for p in range(npair):
                h0, h1 = 2 * p, 2 * p + 1
                diff = (jnp.where(first, gcc[h0], gcc[h1])
                        - jnp.where(first, gcr_all[base + h0:base + h0 + 1, :], gcr_all[base + h1:base + h1 + 1, :]))
                vw["decay"].append(jnp.where(inc, jnp.exp(jnp.where(inc, diff, 0.0)), 0.0))
                kp = akb[:, 2 * A_DK * p:2 * A_DK * (p + 1)]
                vw["kst"].append(jnp.concatenate([jnp.where(first_k, kp, 0), jnp.where(first_k, 0, kp)], axis=0))
                vw["kbq"].append(kbq[:, 2 * A_DK * p:2 * A_DK * (p + 1)])
            vb = heads(vw["av"] * beta_k, A_DV)
            kbe = heads(kb * egc_k, A_DK)
            rhs = [jnp.concatenate([vb[h], kbe[h]], axis=1) for h in range(nh)]
            vw["rhs"] = [_bf(jnp.concatenate([rhs[2 * p], rhs[2 * p + 1]], axis=0)) for p in range(npair)]
            vw["qe"] = heads(aq * egc_k, A_DK) if need_out else None
            vw["kdec"] = heads(_bf(ak * expand([jnp.exp(gtot[h] - gcc[h]) for h in range(nh)])), A_DK)
        for vw in views:
            vw["m"] = [_dot_nt(vw["kbq"][p], vw["kst"][p]) for p in range(npair)]
        for vw in views:
            lm = vw["lmask2"]
            vw["a"] = [vw["m"][p][:c] * vw["decay"][p] * vw["strict2"] for p in range(npair)]
            if need_out:
                vw["qk"] = [blockdiag(vw["m"][p][c:] * vw["decay"][p]) for p in range(npair)]
            vw["t"] = [lm[LEVELS] - vw["a"][p] * lm[LEVELS - 1] for p in range(npair)]

    def level(lv):
        m = c >> (lv + 1)
        cut = m % SUBLANES == 0

        def run():
            for vw in views:
                fwd = vw["d"] == 0
                rows = (lambda x: _later_rows(x, m, fwd)) if cut else (lambda x: x)
                vw["tl"] = [_dot(rows(vw["t"][p]), blockdiag(vw["a"][p] * vw["lmask2"][lv])) for p in range(npair)]
            for vw in views:
                fwd = vw["d"] == 0
                for p in range(npair):
                    t = vw["t"][p]
                    upd = _dot(vw["tl"][p], blockdiag(t))
                    vw["t"][p] = _merge_later_rows(t, _later_rows(t, m, fwd) - upd, m, fwd) if cut else t - upd
        return run

    def solve():
        for vw in views:
            rp = [_dot(blockdiag(vw["t"][p]), vw["rhs"][p]) for p in range(npair)]
            vw["r"] = [rp[h // 2][(h % 2) * c:(h % 2 + 1) * c] for h in range(nh)]
        for vw in views:
            vw["lhs"] = [_bf(jnp.concatenate([vw["r"][h][:, A_DV:], vw["qe"][h]], axis=0) if need_out
                             else vw["r"][h][:, A_DV:]) for h in range(nh)]

    def recur(pos, is_last):
        def run():
            cur = [vw for vw in views if vw["pos"] == pos]
            for vw in cur:
                d = vw["d"]
                for h in range(nh):
                    if (d, h) not in state:
                        state[(d, h)] = s_ref[d, h]
                vw["ws"] = [_dot(vw["lhs"][h], state[(d, h)]) for h in range(nh)]
            for vw in cur:
                vw["vn"] = [vw["r"][h][:, :A_DV] - vw["ws"][h][:c] for h in range(nh)]
            for vw in cur if need_out else []:
                for p in range(npair):
                    vn2 = _bf(jnp.concatenate([vw["vn"][2 * p], vw["vn"][2 * p + 1]], axis=0))
                    oq = _dot(vw["qk"][p], vn2)
                    for e in range(2):
                        h = 2 * p + e
                        vw["og"][0, vw["rows"], h * A_DV:(h + 1) * A_DV] = vw["ws"][h][c:] + oq[e * c:(e + 1) * c]
            for vw in cur:
                d = vw["d"]
                for h in range(nh):
                    state[(d, h)] = state[(d, h)] * vw["gl"][h] + _dot_tn(vw["kdec"][h], vw["vn"][h])
                    if is_last:
                        s_ref[d, h] = state[(d, h)]
        return run

    return [prep] + [level(lv) for lv in range(LEVELS - 2, -1, -1)] + [solve], recur


def _gla_stages(views, st_ref, state, need_out):
    c = CHUNK
    nh = B_HEADS
    lane_q = lax.broadcasted_iota(jnp.int32, (1, BQ_W), 1) // B_DK
    row = lax.broadcasted_iota(jnp.int32, (c, 1), 0)

    def stack_heads(x):
        return jnp.concatenate([jnp.where(lane_q == h, x, 0) for h in range(nh)], axis=0)

    def level_abs(vw, lv):
        m = c >> (lv + 1)
        fwd = vw["d"] == 0
        bc, na = vw["bc"], vw["na"]
        if m >= 4:
            refs = [s + (m - 1 if fwd else m) for s in range(0, c, 2 * m)]
            ref_rows = jnp.concatenate([jnp.broadcast_to(bc[r:r + 1, :], (2 * m, BQ_W)) for r in refs], axis=0)
            return jnp.abs(bc - ref_rows)
        r = row % (2 * m)
        up = pltpu.roll(na, c - 1, 0)
        dn = pltpu.roll(na, 1, 0)
        if m == 2:
            if fwd:
                return jnp.where(r == 0, up, jnp.where(r == 1, 0.0, jnp.where(r == 2, na, na + dn)))
            return jnp.where(r == 0, na + up, jnp.where(r == 1, na, jnp.where(r == 2, 0.0, dn)))
        return jnp.where(r == (1 if fwd else 0), na, 0.0)

    def prep():
        for vw in views:
            vw["bc"] = _dot_exact_lhs(vw["tri"], vw["gk"])
            vw["na"] = -vw["gk"]
            vw["sacc"] = [jnp.zeros((c, c), F32) for _ in range(nh)]

    def level(lv):
        def run():
            for vw in views:
                m = c >> (lv + 1)
                fwd = vw["d"] == 0
                if lv >= LEVELS - 1:
                    if lv == LEVELS:
                        prod = vw["bq"] * vw["bk"]
                    else:
                        kn = pltpu.roll(vw["bk"], 1 if fwd else c - 1, 0)
                        prod = vw["bq"] * jnp.exp(-level_abs(vw, lv)) * kn
                    mask = vw["lmaskv"][lv, :c, :] > 0.5
                    for h in range(nh):
                        col = jnp.sum(jnp.where(lane_q == h, prod, 0.0), axis=-1, keepdims=True)
                        vw["sacc"][h] = vw["sacc"][h] + jnp.where(mask, col, 0.0)
                    continue
                wgt = jnp.exp(-level_abs(vw, lv))
                ql, kl = _bf(vw["bq"] * wgt), _bf(vw["bk"] * wgt)
                if m % SUBLANES == 0:
                    sl = _dot_nt(stack_heads(_later_rows(ql, m, fwd)), kl)
                    mask = _later_rows(vw["lmaskv"][lv, :c, :], m, fwd)
                    hc = c // 2
                    for h in range(nh):
                        old = vw["sacc"][h]
                        new = _later_rows(old, m, fwd) + sl[h * hc:(h + 1) * hc] * mask
                        vw["sacc"][h] = _merge_later_rows(old, new, m, fwd)
                else:
                    sl = _dot_nt(stack_heads(ql), kl) * vw["lmaskv"][lv]
                    vw["sacc"] = [vw["sacc"][h] + sl[h * c:(h + 1) * c] for h in range(nh)]
        return run

    def intra():
        for vw in views:
            last = c - 1 if vw["d"] == 0 else 0
            bc = vw["bc"]
            bct = bc[last:last + 1]
            bvh = [_bf(vw["bv"][:, h * B_DV:(h + 1) * B_DV]) for h in range(nh)]
            kd = vw["bk"] * jnp.exp(bct - bc)
            kdg = jnp.concatenate([kd, jnp.broadcast_to(jnp.exp(bct), (SUBLANES, BQ_W))], axis=0).T
            kdT = _bf(kdg[:, :c])
            vw["lgl"] = kdg[:, c:c + 1]
            if need_out:
                both = [_dot(jnp.concatenate([_bf(vw["sacc"][h]), kdT[h * B_DK:(h + 1) * B_DK]], axis=0), bvh[h])
                        for h in range(nh)]
                vw["oi"] = [b[:c] for b in both]
                vw["upd"] = jnp.concatenate([b[c:] for b in both], axis=0)
                vw["qd"] = stack_heads(_bf(vw["bq"] * jnp.exp(bc)))
            else:
                vw["upd"] = jnp.concatenate([_dot(kdT[h * B_DK:(h + 1) * B_DK], bvh[h]) for h in range(nh)], axis=0)

    def recur(pos, is_last):
        def run():
            for vw in views:
                if vw["pos"] != pos:
                    continue
                d = vw["d"]
                if d not in state:
                    state[d] = st_ref[d]
                if need_out:
                    oint = _dot(vw["qd"], state[d])
                    for h in range(nh):
                        vw["ol"][0, vw["rows"], h * B_DV:(h + 1) * B_DV] = vw["oi"][h] + oint[h * c:(h + 1) * c]
                state[d] = state[d] * vw["lgl"] + vw["upd"]
                if is_last:
                    st_ref[d] = state[d]
        return run

    return [prep] + ([level(lv) for lv in range(LEVELS + 1)] if need_out else []) + [intra], recur


def _scan_kernel(need_out, nchunks, *refs):
    n_in = 9
    n_c = 6
    k = 2 * n_in + 2 * n_c
    if need_out:
        s_in, st_in, og_f, og_b, ol_f, ol_b, s_ref, st_ref = refs[k:]
    else:
        s_out, st_out, s_ref, st_ref = refs[k:]
        og_f = og_b = ol_f = ol_b = None
    c = CHUNK

    @pl.when(pl.program_id(1) == 0)
    def _():
        if need_out:
            s_ref[...] = s_in[0]
            st_ref[...] = st_in[0]
        else:
            s_ref[...] = jnp.zeros_like(s_ref)
            st_ref[...] = jnp.zeros_like(st_ref)

    views = []
    for d, (og, ol) in enumerate(((og_f, ol_f), (og_b, ol_b))):
        aq, ak, av, sm, smT, bq, bk, bv, gk = refs[d * n_in:(d + 1) * n_in]
        tri, tri2, lmask2, lmaskv, incl2, strict2 = refs[2 * n_in + d * n_c:2 * n_in + (d + 1) * n_c]
        order = range(nchunks) if d == 0 else range(nchunks - 1, -1, -1)
        for pos, j in enumerate(order):
            rows = pl.ds(j * c, c)
            views.append(dict(d=d, pos=pos, rows=rows, aq=aq[0, rows, :], ak=ak[0, rows, :], av=av[0, rows, :],
                              sm=sm[0, rows, :], smT=smT[j], bq=bq[0, rows, :], bk=bk[0, rows, :],
                              bv=bv[0, rows, :], gk=gk[0, rows, :], tri=tri[...], tri2=tri2[...], lmask2=lmask2,
                              lmaskv=lmaskv, incl2=incl2[...], strict2=strict2[...], og=og, ol=ol))
    gdn, gdn_recur = _gdn_stages(views, s_ref, {}, need_out)
    gla, gla_recur = _gla_stages(views, st_ref, {}, need_out)
    for i in range(max(len(gdn), len(gla))):
        if i < len(gdn):
            gdn[i]()
        if i < len(gla):
            gla[i]()
    for pos in range(nchunks):
        gdn_recur(pos, pos == nchunks - 1)()
        gla_recur(pos, pos == nchunks - 1)()

    if not need_out:
        @pl.when(pl.program_id(1) == pl.num_programs(1) - 1)
        def _():
            s_out[0] = s_ref[...]
            st_out[0] = st_ref[...]


def _scans(aq, ak, av, sm, smT, bq, bk, bv, gk, ctx_len):
    bsz, ttot, _ = aq.shape
    seq = ttot - ctx_len
    c = CHUNK
    consts = _scan_consts()
    const = lambda a: pl.BlockSpec(a.shape, lambda b, s: tuple(0 for _ in a.shape))
    cargs = []
    for d in range(NDIR):
        cd = consts[d]
        cargs += [cd["tri"], cd["tri2"], cd["lmask2"], cd["lmaskv"], cd["incl2"], cd["strict2"]]
    ins = [aq, ak, av, sm, smT, bq, bk, bv, gk]
    s_shape = (NDIR, A_HEADS, A_DK, A_DV)
    st_shape = (NDIR, BQ_W, B_DV)
    state_specs = [pl.BlockSpec((1,) + s_shape, lambda b, s: (b, 0, 0, 0, 0)),
                   pl.BlockSpec((1,) + st_shape, lambda b, s: (b, 0, 0, 0))]
    state_shapes = [jax.ShapeDtypeStruct((bsz,) + s_shape, F32), jax.ShapeDtypeStruct((bsz,) + st_shape, F32)]
    scratch = [pltpu.VMEM(s_shape, F32), pltpu.VMEM(st_shape, F32)]
    params = pltpu.CompilerParams(dimension_semantics=("arbitrary", "arbitrary"), vmem_limit_bytes=VMEM_LIMIT)

    def in_specs(nchunks, first, n):
        tb = nchunks * c

        def tok(w, idx, lane_blk=0):
            return pl.BlockSpec((1, tb, w), lambda b, s: (b, idx(s), lane_blk))

        def dir_specs(idx, d):
            return [tok(A_W, idx), tok(A_W, idx), tok(A_W, idx), tok(SMALL_W, idx),
                    pl.BlockSpec((None, nchunks, 16, c), lambda b, s: (b, idx(s), 0, 0)),
                    tok(BQ_W, idx), tok(BQ_W, idx), tok(BV_W, idx), tok(BQ_W, idx, d)]

        return dir_specs(lambda s: first + s, 0) + dir_specs(lambda s: first + n - 1 - s, 1) + [const(a) for a in cargs]

    tbc = CTX_STEP_CHUNKS * c
    ncs = ctx_len // tbc
    s_ctx, st_ctx = pl.pallas_call(
        functools.partial(_scan_kernel, False, CTX_STEP_CHUNKS),
        grid=(bsz, ncs),
        in_specs=in_specs(CTX_STEP_CHUNKS, seq // tbc, ncs),
        out_specs=state_specs,
        out_shape=state_shapes,
        scratch_shapes=scratch,
        compiler_params=params,
        name="scan_ctx",
    )(*ins, *ins, *cargs)

    tb = STEP_CHUNKS * c
    nxs = seq // tb

    def otok(w, idx):
        return pl.BlockSpec((1, tb, w), lambda b, s: (b, idx(s), 0))

    out_w = [A_HEADS * A_DV, A_HEADS * A_DV, BV_W, BV_W]
    fwd, bwd = (lambda s: s), (lambda s: nxs - 1 - s)
    return pl.pallas_call(
        functools.partial(_scan_kernel, True, STEP_CHUNKS),
        grid=(bsz, nxs),
        in_specs=in_specs(STEP_CHUNKS, 0, nxs) + state_specs,
        out_specs=[otok(out_w[0], fwd), otok(out_w[1], bwd), otok(out_w[2], fwd), otok(out_w[3], bwd)],
        out_shape=[jax.ShapeDtypeStruct((bsz, seq, w), F32) for w in out_w],
        scratch_shapes=scratch,
        compiler_params=params,
        name="scan",
    )(*ins, *ins, *cargs, s_ctx, st_ctx)


def _head_norm(o, w, z, n_heads, dv):
    parts = []
    for h in range(n_heads):
        oh = o[:, h * dv:(h + 1) * dv]
        parts.append(oh * lax.rsqrt(jnp.mean(oh * oh, axis=-1, keepdims=True) + EPS))
    return jnp.concatenate(parts, axis=-1) * w * _silu(z)


def _rms(x, w):
    return x * lax.rsqrt(jnp.mean(x * x, axis=-1, keepdims=True) + EPS) * w


def _post_kernel(x_ref, ogf_ref, ogb_ref, olf_ref, olb_ref, az_ref, bg_ref,
                 g1_ref, sh2_ref, sc2_ref, g2_ref, gn_ref, ln_ref, wo_ref, fn_ref,
                 wg_ref, wu_ref, wd_ref, fin_ref, o_ref):
    gdn = _head_norm(ogf_ref[0] + ogb_ref[0], gn_ref[...], az_ref[0], A_HEADS, A_DV)
    gla = _head_norm(olf_ref[0] + olb_ref[0], ln_ref[...], bg_ref[0], B_HEADS, B_DV)
    mix = jnp.concatenate([gdn, gla], axis=-1)
    x1 = x_ref[0] + g1_ref[...] * _dot(mix, wo_ref[...])
    h2b = _bf(_rms(x1, fn_ref[...]) * (1.0 + sc2_ref[...]) + sh2_ref[...])
    dff = wg_ref.shape[1]
    y = None
    for c0 in range(0, dff, FFN_CHUNK):
        c1 = min(c0 + FFN_CHUNK, dff)
        gate = jnp.dot(h2b, wg_ref[:, c0:c1], preferred_element_type=F32)
        up = jnp.dot(h2b, wu_ref[:, c0:c1], preferred_element_type=F32)
        part = _dot(_silu(gate) * up, wd_ref[c0:c1, :])
        y = part if y is None else y + part
    x2 = x1 + g2_ref[...] * y
    o_ref[0] = _rms(x2, fin_ref[...])


def _post(x, ogf, ogb, olf, olb, az, bg, mod4, gn, ln, wo, fn, wg, wu, wd, fin):
    bsz, seq, d = x.shape
    tm = POST_TILE
    dff = wg.shape[1]

    def tok(w):
        return pl.BlockSpec((1, tm, w), lambda b, j: (b, j, 0))

    def modspec(k):
        return pl.BlockSpec((None, None, 1, d), lambda b, j: (b, k, 0, 0))

    def const(shape, single=True):
        idx = lambda b, j: tuple(0 for _ in shape)
        if single:
            return pl.BlockSpec(shape, idx, pipeline_mode=pl.Buffered(1))
        return pl.BlockSpec(shape, idx)

    widths = [d, A_W, A_W, BV_W, BV_W, A_W, BV_W]
    return pl.pallas_call(
        _post_kernel,
        grid=(bsz, seq // tm),
        in_specs=[tok(w) for w in widths] + [
                  modspec(2), modspec(3), modspec(4), modspec(5),
                  const((1, A_W), False), const((1, BV_W), False), const((d, d)), const((1, d), False),
                  const((d, dff)), const((d, dff)), const((dff, d)), const((1, d), False)],
        out_specs=tok(d),
        out_shape=jax.ShapeDtypeStruct((bsz, seq, d), F32),
        compiler_params=pltpu.CompilerParams(dimension_semantics=("arbitrary", "arbitrary"),
                                             vmem_limit_bytes=VMEM_LIMIT),
        name="post",
    )(x, ogf, ogb, olf, olb, az, bg, mod4, mod4, mod4, mod4, gn, ln, wo, fn, wg, wu, wd, fin)


def kernel(x, c, ctx, c_ctx, w_mod, b_mod, attn_norm, w_in, conv_w, a_log, dt_bias, gdn_norm, gla_w2, gla_b,
           gla_norm, w_out, ffn_norm, w_gate, w_up, w_down, final_norm):
    bsz, seq, d = x.shape
    ctx_len = ctx.shape[1]
    assert w_mod.shape[0] == 1, "single-layer block"
    assert ctx_len == TOK_TILE and seq % TOK_TILE == 0 and seq % POST_TILE == 0
    assert TOK_TILE % GRID_W == 0 and ctx_len % (CHUNK * CTX_STEP_CHUNKS) == 0 and seq % (CHUNK * STEP_CHUNKS) == 0

    rows = -(-(bsz + 1) // 8) * 8
    cc = jnp.zeros((rows, d), F32).at[:bsz].set(c).at[rows - 1].set(c_ctx)
    mod = _modulation(cc, w_mod[0], b_mod[0][None, :])
    mod4 = mod.reshape(rows, 6, 1, d)

    w = w_in[0]
    o1 = 4 * A_W
    nb = NDIR * A_HEADS
    o2 = o1 + 2 * nb
    o3 = o2 + 2 * BQ_W + 2 * BV_W
    nlr = NDIR * GLA_RANK
    win = _regroup_weight(w.T, o1, o2, o3, nlr)
    gparams = jnp.zeros((2, SMALL_W), F32)
    gparams = gparams.at[0, nb:2 * nb].set(dt_bias[0].reshape(-1)).at[1, nb:2 * nb].set(a_log[0].reshape(-1))
    w2 = jnp.zeros((SMALL_W, NDIR * BQ_W), F32)
    for n in range(NDIR):
        w2 = w2.at[2 * nb + n * GLA_RANK:2 * nb + (n + 1) * GLA_RANK, n * BQ_W:(n + 1) * BQ_W].set(gla_w2[0, n])
    w2h = w2.astype(BF16)
    w2m = (w2 - w2h.astype(F32)).astype(BF16)

    aq, ak, av, az, bq, bk, bv, bg, gk, sm, smT = _projection(
        x, ctx, mod4, attn_norm, win, conv_w[0], gparams, w2h, w2m, gla_b[0].reshape(1, -1))
    ogf, ogb, olf, olb = _scans(aq, ak, av, sm, smT, bq, bk, bv, gk, ctx_len)

    tile_w = lambda v, n: jnp.tile(v.reshape(1, -1), (1, n))
    return _post(x, ogf, ogb, olf, olb, az, bg, mod4,
                 tile_w(gdn_norm[0], A_HEADS), tile_w(gla_norm[0], B_HEADS),
                 w_out[0].astype(BF16), ffn_norm, w_gate[0].astype(BF16), w_up[0].astype(BF16),
                 w_down[0].astype(BF16), final_norm.reshape(1, -1))
```

```python
import functools

import numpy as np
import jax
import jax.numpy as jnp
from jax import lax
from jax.experimental import pallas as pl
from jax.experimental.pallas import tpu as pltpu

F32 = jnp.float32
BF16 = jnp.bfloat16

GRID_W = 64
A_HEADS, A_DK, A_DV = 4, 128, 128
B_HEADS, B_DK, B_DV = 4, 64, 128
NDIR = 2
GLA_RANK = 16
GLA_NORMALIZER = 16.0
EPS = 1e-6

CHUNK = 64
CTX_STEP_CHUNKS = 4
STEP_CHUNKS = 8
SUBLANES = 8
TOK_TILE = 256
POST_TILE = 512
FFN_CHUNK = 1024
SMALL_W = 128
LEVELS = int(np.log2(CHUNK))

A_W = A_HEADS * A_DK
BQ_W = B_HEADS * B_DK
BV_W = B_HEADS * B_DV
P_COLS = 4 * A_W + 2 * BQ_W + 2 * BV_W + SMALL_W

V7X_VMEM_BYTES = 64 * 1024 * 1024
VMEM_LIMIT = V7X_VMEM_BYTES - 4 * 1024 * 1024


def _bf(x):
    return x.astype(BF16)


def _dot(a, b):
    return jnp.dot(_bf(a), _bf(b), preferred_element_type=F32)


def _dot_nt(a, b):
    return lax.dot_general(_bf(a), _bf(b), (((1,), (1,)), ((), ())), preferred_element_type=F32)


def _dot_tn(a, b):
    return lax.dot_general(_bf(a), _bf(b), (((0,), (0,)), ((), ())), preferred_element_type=F32)


def _split3(x):
    hi = _bf(x)
    r1 = x - hi.astype(F32)
    mid = _bf(r1)
    lo = _bf(r1 - mid.astype(F32))
    return hi, mid, lo


def _dot_exact_lhs(m, x, terms=3):
    d = lambda p: jnp.dot(m, p, preferred_element_type=F32)
    parts = _split3(x)[:terms]
    acc = d(parts[0])
    for p in parts[1:]:
        acc = acc + d(p)
    return acc


def _dot_exact_rhs_nt(x, m):
    hi, mid, lo = _split3(x)
    d = lambda p: lax.dot_general(p, m, (((1,), (1,)), ((), ())), preferred_element_type=F32)
    return d(hi) + d(mid) + d(lo)


def _sigmoid(x):
    return 0.5 * jnp.tanh(0.5 * x) + 0.5


def _silu(x):
    h = 0.5 * x
    return h + h * jnp.tanh(h)


def _softplus(x):
    return jnp.maximum(x, 0.0) + jnp.log(1.0 + jnp.exp(-jnp.abs(x)))


def _later_rows(x, m, fwd):
    n = x.shape[0]
    return jnp.concatenate([x[s + m:s + 2 * m] if fwd else x[s:s + m] for s in range(0, n, 2 * m)], axis=0)


def _merge_later_rows(full, later, m, fwd):
    n = full.shape[0]
    pieces = []
    for i, s in enumerate(range(0, n, 2 * m)):
        lat = later[i * m:(i + 1) * m]
        pieces += [full[s:s + m], lat] if fwd else [lat, full[s + m:s + 2 * m]]
    return jnp.concatenate(pieces, axis=0)


def _mod_kernel(cc_ref, w_ref, b_ref, o_ref):
    o_ref[...] = _dot(_silu(cc_ref[...]), w_ref[...]) + b_ref[...]


def _modulation(cc, w_mod, b_mod):
    rows, d = cc.shape
    n = w_mod.shape[1]
    tn = 1536
    return pl.pallas_call(
        _mod_kernel,
        grid=(n // tn,),
        in_specs=[pl.BlockSpec((rows, d), lambda j: (0, 0)),
                  pl.BlockSpec((d, tn), lambda j: (0, j)),
                  pl.BlockSpec((1, tn), lambda j: (0, j))],
        out_specs=pl.BlockSpec((rows, tn), lambda j: (0, j)),
        out_shape=jax.ShapeDtypeStruct((rows, n), F32),
        compiler_params=pltpu.CompilerParams(dimension_semantics=("arbitrary",),
                                             vmem_limit_bytes=VMEM_LIMIT),
        name="mod",
    )(cc, w_mod, b_mod)


def _regroup_kernel(o1, o2, o3, nlr, wt_ref, o_ref):
    pad = jnp.zeros((SMALL_W - (o2 - o1) - nlr, wt_ref.shape[1]), F32)
    rows = jnp.concatenate([wt_ref[:o1, :], wt_ref[o2:o2 + BQ_W, :] * B_DK ** -0.5, wt_ref[o2 + BQ_W:o3, :],
                            wt_ref[o1:o2, :], wt_ref[o3:o3 + nlr, :], pad], axis=0)
    o_ref[...] = _bf(rows.T)


def _regroup_weight(wt, o1, o2, o3, nlr):
    n, d = wt.shape
    tr = 256
    return pl.pallas_call(
        functools.partial(_regroup_kernel, o1, o2, o3, nlr),
        grid=(d // tr,),
        in_specs=[pl.BlockSpec((n, tr), lambda i: (0, i))],
        out_specs=pl.BlockSpec((tr, P_COLS), lambda i: (i, 0)),
        out_shape=jax.ShapeDtypeStruct((d, P_COLS), BF16),
        compiler_params=pltpu.CompilerParams(dimension_semantics=("arbitrary",), vmem_limit_bytes=VMEM_LIMIT),
        name="regroup",
    )(wt)


def _proj_kernel(n_ctx_tiles, ctx_len,
                 x_ref, ctx_ref, sh_ref, sc_ref, nw_ref, win_ref, cw_ref, gp_ref, w2h_ref, w2m_ref, gb_ref,
                 aq_ref, ak_ref, av_ref, az_ref, bq_ref, bk_ref, bv_ref, bg_ref, gk_ref, sm_ref, smt_ref):
    j = pl.program_id(1)
    is_ctx = j < n_ctx_tiles
    xin = jnp.where(is_ctx, ctx_ref[0], x_ref[0])
    ms = jnp.mean(xin * xin, axis=-1, keepdims=True)
    h = xin * lax.rsqrt(ms + EPS) * (nw_ref[...] * (1.0 + sc_ref[...])) + sh_ref[...]
    hb = _bf(h)
    tm = xin.shape[0]
    sub = SUBLANES
    sublane = lax.broadcasted_iota(jnp.int32, (1, sub, 1), 1)
    cw_half = 0.5 * cw_ref[...]

    def mm(c0, c1):
        return jnp.dot(hb, win_ref[:, c0:c1], preferred_element_type=F32)

    def conv_silu(u, c0):
        wu = u.shape[1]
        u3 = u.reshape(tm // sub, sub, wu)

        def shifted(down):
            rot = pltpu.roll(u3, 1 if down else sub - 1, 1)
            nt = tm // sub
            tiles_per_row = GRID_W // sub
            zero = jnp.zeros((1, sub, wu), F32)
            nbr = []
            for r in range(nt):
                src = r - 1 if down else r + 1
                row_start = (r if down else src) % tiles_per_row == 0
                if src < 0 or src >= nt:
                    nbr.append(zero)
                elif row_start:
                    nbr.append(jnp.where(is_ctx, rot[src:src + 1], 0.0))
                else:
                    nbr.append(rot[src:src + 1])
            edge = sublane == (0 if down else sub - 1)
            return jnp.where(edge, jnp.concatenate(nbr, axis=0), rot).reshape(tm, wu)

        w = cw_half[:, c0:c0 + wu]
        h = w[0:1] * shifted(True) + w[1:2] * u + w[2:3] * shifted(False)
        return h + h * jnp.tanh(h)

    def l2n(t, scale):
        parts = []
        for hd in range(A_HEADS):
            th = t[:, hd * A_DK:(hd + 1) * A_DK]
            parts.append(th * (lax.rsqrt(jnp.sum(th * th, axis=-1, keepdims=True) + EPS) * scale))
        return jnp.concatenate(parts, axis=-1)

    o = 3 * A_W
    uq = mm(0, A_W)
    p1 = mm(o, o + A_W + 2 * BQ_W)
    o += A_W + 2 * BQ_W
    uk = mm(A_W, 2 * A_W)
    p2 = mm(o, o + BV_W)
    o += BV_W
    uv = mm(2 * A_W, 3 * A_W)
    p3 = mm(o, P_COLS)
    aq_ref[0] = _bf(l2n(conv_silu(uq, 0), A_DK ** -0.5))
    az_ref[0] = p1[:, :A_W]
    bq_ref[0] = _bf(p1[:, A_W:A_W + BQ_W])
    bk_ref[0] = _bf(p1[:, A_W + BQ_W:])
    ak_ref[0] = _bf(l2n(conv_silu(uk, A_W), 1.0))
    bv_ref[0] = _bf(p2)
    av_ref[0] = _bf(conv_silu(uv, 2 * A_W))
    bg_ref[0] = p3[:, :BV_W]

    ps = p3[:, BV_W:]
    gp = gp_ref[...]
    lane = lax.broadcasted_iota(jnp.int32, (1, SMALL_W), 1)
    beta = _sigmoid(ps)
    g = -jnp.exp(gp[1:2]) * _softplus(ps + gp[0:1])
    nb = NDIR * A_HEADS
    sm = jnp.where(lane < nb, beta, jnp.where(lane < 2 * nb, g, 0.0))
    sm_ref[0] = sm
    for jc in range(tm // CHUNK):
        smt_ref[0, jc] = sm[jc * CHUNK:(jc + 1) * CHUNK, :].T[:2 * nb, :]

    ph = _bf(ps)
    pm = _bf(ps - ph.astype(F32))
    d = lambda a, b: jnp.dot(a, b, preferred_element_type=F32)
    pre = d(ph, w2h_ref[...]) + d(ph, w2m_ref[...]) + d(pm, w2h_ref[...]) + gb_ref[...]
    gk_ref[0] = -_softplus(-pre) * (1.0 / GLA_NORMALIZER)


def _projection(x, ctx, mod4, attn_norm, win, conv_w, gparams, w2h, w2m, gla_b):
    bsz, seq, d = x.shape
    ctx_len = ctx.shape[1]
    tm = TOK_TILE
    nct = ctx_len // tm
    nxt = seq // tm
    ttot = ctx_len + seq
    mod_rows = mod4.shape[0]

    def tile(j):
        return jnp.where(j < nct, nxt + j, j - nct)

    def tok(w):
        return pl.BlockSpec((1, tm, w), lambda b, j: (b, tile(j), 0))

    cpt = tm // CHUNK
    nsm = 2 * NDIR * A_HEADS

    const = lambda shape: pl.BlockSpec(shape, lambda b, j: tuple(0 for _ in shape))
    widths = [A_W, A_W, A_W, A_W, BQ_W, BQ_W, BV_W, BV_W, NDIR * BQ_W, SMALL_W]
    dtypes = [BF16, BF16, BF16, F32, BF16, BF16, BF16, F32, F32, F32]
    return pl.pallas_call(
        functools.partial(_proj_kernel, nct, ctx_len),
        grid=(bsz, nct + nxt),
        in_specs=[
            pl.BlockSpec((1, tm, d), lambda b, j: (b, jnp.maximum(j - nct, 0), 0)),
            pl.BlockSpec((1, tm, d), lambda b, j: (b, jnp.minimum(j, nct - 1), 0)),
            pl.BlockSpec((None, None, 1, d), lambda b, j: (jnp.where(j < nct, mod_rows - 1, b), 0, 0, 0)),
            pl.BlockSpec((None, None, 1, d), lambda b, j: (jnp.where(j < nct, mod_rows - 1, b), 1, 0, 0)),
            const((1, d)), const((d, P_COLS)), const(conv_w.shape), const(gparams.shape),
            const(w2h.shape), const(w2m.shape), const(gla_b.shape),
        ],
        out_specs=[tok(w) for w in widths] + [
            pl.BlockSpec((1, cpt, nsm, CHUNK), lambda b, j: (b, tile(j), 0, 0))],
        out_shape=[jax.ShapeDtypeStruct((bsz, ttot, w), dt) for w, dt in zip(widths, dtypes)] + [
            jax.ShapeDtypeStruct((bsz, ttot // CHUNK, nsm, CHUNK), F32)],
        compiler_params=pltpu.CompilerParams(dimension_semantics=("arbitrary", "arbitrary"),
                                             vmem_limit_bytes=VMEM_LIMIT),
        name="proj",
    )(x, ctx, mod4, mod4, attn_norm, win, conv_w, gparams, w2h, w2m, gla_b)


def _scan_consts():
    c = CHUNK
    i = np.arange(c)[:, None]
    t = np.arange(c)[None, :]
    out = []
    for d in range(NDIR):
        incl = (t <= i) if d == 0 else (t >= i)
        strict = (t < i) if d == 0 else (t > i)
        tri = incl.astype(np.float32)
        masks = []
        for lv in range(LEVELS):
            m = c >> (lv + 1)
            start = (np.arange(c) // (2 * m)) * (2 * m)
            later = (np.arange(c) % (2 * m) >= m) if d == 0 else (np.arange(c) % (2 * m) < m)
            same = (start[:, None] == start[None, :])
            masks.append((same & later[:, None] & ~later[None, :]).astype(np.float32))
        masks.append(np.eye(c, dtype=np.float32))
        tile2 = lambda mk: np.tile(mk, (1, 2))
        out.append(dict(
            tri=jnp.asarray(tri, BF16),
            tri2=jnp.asarray(np.tile(tri, (2, 1)), BF16),
            lmask2=jnp.asarray(np.stack([tile2(mk) for mk in masks]), F32),
            lmaskv=jnp.asarray(np.stack([np.tile(mk, (B_HEADS, 1)) for mk in masks]), F32),
            incl2=jnp.asarray(tile2(incl.astype(np.float32))),
            strict2=jnp.asarray(tile2(strict.astype(np.float32))),
        ))
    return out


def _gdn_stages(views, s_ref, state, need_out):
    c = CHUNK
    nh = A_HEADS
    npair = nh // 2
    w2 = 2 * c
    first = lax.broadcasted_iota(jnp.int32, (1, w2), 1) < c
    first_k = lax.broadcasted_iota(jnp.int32, (1, 2 * A_DK), 1) < A_DK
    lane_k = lax.broadcasted_iota(jnp.int32, (1, A_W), 1) // A_DK
    km = [lane_k == h for h in range(nh)]

    def expand(cols):
        out = cols[nh - 1]
        for h in range(nh - 2, -1, -1):
            out = jnp.where(km[h], cols[h], out)
        return out

    def blockdiag(x):
        xb = _bf(x)
        return jnp.concatenate([jnp.where(first, xb, 0), jnp.where(first, 0, xb)], axis=0)

    def heads(x, w):
        return [x[:, h * w:(h + 1) * w] for h in range(nh)]

    def prep():
        for vw in views:
            d = vw["d"]
            last = c - 1 if d == 0 else 0
            base = NDIR * nh + nh * d
            sm = vw["sm"]
            gc_all = _dot_exact_lhs(vw["tri"], sm)
            gcr_all = _dot_exact_rhs_nt(vw["smT"], vw["tri2"])
            gcc = [gc_all[:, base + h:base + h + 1] for h in range(nh)]
            gtot = [gc_all[last:last + 1, base + h:base + h + 1] for h in range(nh)]
            beta = [sm[:, nh * d + h:nh * d + h + 1] for h in range(nh)]
            egc = [jnp.exp(g) for g in gcc]
            vw["gl"] = [jnp.exp(g) for g in gtot]
            inc = vw["incl2"] > 0.5
            ak, aq = vw["ak"], vw["aq"]
            beta_k = expand(beta)
            kb = ak * beta_k
            egc_k = expand(egc)
            kbq = _bf(jnp.concatenate([kb, aq], axis=0) if need_out else kb)
            akb = _bf(ak)
            vw["decay"], vw["kbq"], vw["kst"] = [], [], []
            for p in range(npair):
                h0, h1 = 2 * p, 2 * p + 1
                diff = (jnp.where(first, gcc[h0], gcc[h1])
                        - jnp.where(first, gcr_all[base + h0:base + h0 + 1, :], gcr_all[base + h1:base + h1 + 1, :]))
                vw["decay"].append(jnp.where(inc, jnp.exp(jnp.where(inc, diff, 0.0)), 0.0))
                kp = akb[:, 2 * A_DK * p:2 * A_DK * (p + 1)]
                vw["kst"].append(jnp.concatenate([jnp.where(first_k, kp, 0), jnp.where(first_k, 0, kp)], axis=0))
                vw["kbq"].append(kbq[:, 2 * A_DK * p:2 * A_DK * (p + 1)])
            vb = heads(vw["av"] * beta_k, A_DV)
            kbe = heads(kb * egc_k, A_DK)
            rhs = [jnp.concatenate([vb[h], kbe[h]], axis=1) for h in range(nh)]
            vw["rhs"] = [_bf(jnp.concatenate([rhs[2 * p], rhs[2 * p + 1]], axis=0)) for p in range(npair)]
            vw["qe"] = heads(aq * egc_k, A_DK) if need_out else None
            vw["kdec"] = heads(_bf(ak * expand([jnp.exp(gtot[h] - gcc[h]) for h in range(nh)])), A_DK)
        for vw in views:
            vw["m"] = [_dot_nt(vw["kbq"][p], vw["kst"][p]) for p in range(npair)]
        for vw in views:
            lm = vw["lmask2"]
            vw["a"] = [vw["m"][p][:c] * vw["decay"][p] * vw["strict2"] for p in range(npair)]
            if need_out:
                vw["qk"] = [blockdiag(vw["m"][p][c:] * vw["decay"][p]) for p in range(npair)]
            vw["t"] = [lm[LEVELS] - vw["a"][p] * lm[LEVELS - 1] for p in range(npair)]

    def level(lv):
        m = c >> (lv + 1)
        cut = m % SUBLANES == 0

        def run():
            for vw in views:
                fwd = vw["d"] == 0
                rows = (lambda x: _later_rows(x, m, fwd)) if cut else (lambda x: x)
                vw["tl"] = [_dot(rows(vw["t"][p]), blockdiag(vw["a"][p] * vw["lmask2"][lv])) for p in range(npair)]
            for vw in views:
                fwd = vw["d"] == 0
                for p in range(npair):
                    t = vw["t"][p]
                    upd = _dot(vw["tl"][p], blockdiag(t))
                    vw["t"][p] = _merge_later_rows(t, _later_rows(t, m, fwd) - upd, m, fwd) if cut else t - upd
        return run

    def solve():
        for vw in views:
            rp = [_dot(blockdiag(vw["t"][p]), vw["rhs"][p]) for p in range(npair)]
            vw["r"] = [rp[h // 2][(h % 2) * c:(h % 2 + 1) * c] for h in range(nh)]
        for vw in views:
            vw["lhs"] = [_bf(jnp.concatenate([vw["r"][h][:, A_DV:], vw["qe"][h]], axis=0) if need_out
                             else vw["r"][h][:, A_DV:]) for h in range(nh)]

    def recur(pos, is_last):
        def run():
            cur = [vw for vw in views if vw["pos"] == pos]
            for vw in cur:
                d = vw["d"]
                for h in range(nh):
                    if (d, h) not in state:
                        state[(d, h)] = s_ref[d, h]
                vw["ws"] = [_dot(vw["lhs"][h], state[(d, h)]) for h in range(nh)]
            for vw in cur:
                vw["vn"] = [vw["r"][h][:, :A_DV] - vw["ws"][h][:c] for h in range(nh)]
            for vw in cur if need_out else []:
                for p in range(npair):
                    vn2 = _bf(jnp.concatenate([vw["vn"][2 * p], vw["vn"][2 * p + 1]], axis=0))
                    oq = _dot(vw["qk"][p], vn2)
                    for e in range(2):
                        h = 2 * p + e
                        vw["og"][0, vw["rows"], h * A_DV:(h + 1) * A_DV] = vw["ws"][h][c:] + oq[e * c:(e + 1) * c]
            for vw in cur:
                d = vw["d"]
                for h in range(nh):
                    state[(d, h)] = state[(d, h)] * vw["gl"][h] + _dot_tn(vw["kdec"][h], vw["vn"][h])
                    if is_last:
                        s_ref[d, h] = state[(d, h)]
        return run

    return [prep] + [level(lv) for lv in range(LEVELS - 2, -1, -1)] + [solve], recur


def _gla_stages(views, st_ref, state, need_out):
    c = CHUNK
    nh = B_HEADS
    lane_q = lax.broadcasted_iota(jnp.int32, (1, BQ_W), 1) // B_DK
    row = lax.broadcasted_iota(jnp.int32, (c, 1), 0)

    def stack_heads(x):
        return jnp.concatenate([jnp.where(lane_q == h, x, 0) for h in range(nh)], axis=0)

    def level_abs(vw, lv):
        m = c >> (lv + 1)
        fwd = vw["d"] == 0
        bc, na = vw["bc"], vw["na"]
        if m >= 4:
            refs = [s + (m - 1 if fwd else m) for s in range(0, c, 2 * m)]
            ref_rows = jnp.concatenate([jnp.broadcast_to(bc[r:r + 1, :], (2 * m, BQ_W)) for r in refs], axis=0)
            return jnp.abs(bc - ref_rows)
        r = row % (2 * m)
        up = pltpu.roll(na, c - 1, 0)
        dn = pltpu.roll(na, 1, 0)
        if m == 2:
            if fwd:
                return jnp.where(r == 0, up, jnp.where(r == 1, 0.0, jnp.where(r == 2, na, na + dn)))
            return jnp.where(r == 0, na + up, jnp.where(r == 1, na, jnp.where(r == 2, 0.0, dn)))
        return jnp.where(r == (1 if fwd else 0), na, 0.0)

    def prep():
        for vw in views:
            vw["bc"] = _dot_exact_lhs(vw["tri"], vw["gk"])
            vw["na"] = -vw["gk"]
            vw["sacc"] = [jnp.zeros((c, c), F32) for _ in range(nh)]

    def level(lv):
        def run():
            for vw in views:
                m = c >> (lv + 1)
                fwd = vw["d"] == 0
                if lv >= LEVELS - 1:
                    if lv == LEVELS:
                        prod = vw["bq"] * vw["bk"]
                    else:
                        kn = pltpu.roll(vw["bk"], 1 if fwd else c - 1, 0)
                        prod = vw["bq"] * jnp.exp(-level_abs(vw, lv)) * kn
                    mask = vw["lmaskv"][lv, :c, :] > 0.5
                    for h in range(nh):
                        col = jnp.sum(jnp.where(lane_q == h, prod, 0.0), axis=-1, keepdims=True)
                        vw["sacc"][h] = vw["sacc"][h] + jnp.where(mask, col, 0.0)
                    continue
                wgt = jnp.exp(-level_abs(vw, lv))
                ql, kl = _bf(vw["bq"] * wgt), _bf(vw["bk"] * wgt)
                if m % SUBLANES == 0:
                    sl = _dot_nt(stack_heads(_later_rows(ql, m, fwd)), kl)
                    mask = _later_rows(vw["lmaskv"][lv, :c, :], m, fwd)
                    hc = c // 2
                    for h in range(nh):
                        old = vw["sacc"][h]
                        new = _later_rows(old, m, fwd) + sl[h * hc:(h + 1) * hc] * mask
                        vw["sacc"][h] = _merge_later_rows(old, new, m, fwd)
                else:
                    sl = _dot_nt(stack_heads(ql), kl) * vw["lmaskv"][lv]
                    vw["sacc"] = [vw["sacc"][h] + sl[h * c:(h + 1) * c] for h in range(nh)]
        return run

    def intra():
        for vw in views:
            last = c - 1 if vw["d"] == 0 else 0
            bc = vw["bc"]
            bct = bc[last:last + 1]
            bvh = [_bf(vw["bv"][:, h * B_DV:(h + 1) * B_DV]) for h in range(nh)]
            kd = vw["bk"] * jnp.exp(bct - bc)
            kdg = jnp.concatenate([kd, jnp.broadcast_to(jnp.exp(bct), (SUBLANES, BQ_W))], axis=0).T
            kdT = _bf(kdg[:, :c])
            vw["lgl"] = kdg[:, c:c + 1]
            if need_out:
                both = [_dot(jnp.concatenate([_bf(vw["sacc"][h]), kdT[h * B_DK:(h + 1) * B_DK]], axis=0), bvh[h])
                        for h in range(nh)]
                vw["oi"] = [b[:c] for b in both]
                vw["upd"] = jnp.concatenate([b[c:] for b in both], axis=0)
                vw["qd"] = stack_heads(_bf(vw["bq"] * jnp.exp(bc)))
            else:
                vw["upd"] = jnp.concatenate([_dot(kdT[h * B_DK:(h + 1) * B_DK], bvh[h]) for h in range(nh)], axis=0)

    def recur(pos, is_last):
        def run():
            for vw in views:
                if vw["pos"] != pos:
                    continue
                d = vw["d"]
                if d not in state:
                    state[d] = st_ref[d]
                if need_out:
                    oint = _dot(vw["qd"], state[d])
                    for h in range(nh):
                        vw["ol"][0, vw["rows"], h * B_DV:(h + 1) * B_DV] = vw["oi"][h] + oint[h * c:(h + 1) * c]
                state[d] = state[d] * vw["lgl"] + vw["upd"]
                if is_last:
                    st_ref[d] = state[d]
        return run

    return [prep] + ([level(lv) for lv in range(LEVELS + 1)] if need_out else []) + [intra], recur


def _scan_kernel(need_out, nchunks, *refs):
    n_in = 9
    n_c = 6
    k = 2 * n_in + 2 * n_c
    if need_out:
        s_in, st_in, og_f, og_b, ol_f, ol_b, s_ref, st_ref = refs[k:]
    else:
        s_out, st_out, s_ref, st_ref = refs[k:]
        og_f = og_b = ol_f = ol_b = None
    c = CHUNK

    @pl.when(pl.program_id(1) == 0)
    def _():
        if need_out:
            s_ref[...] = s_in[0]
            st_ref[...] = st_in[0]
        else:
            s_ref[...] = jnp.zeros_like(s_ref)
            st_ref[...] = jnp.zeros_like(st_ref)

    views = []
    for d, (og, ol) in enumerate(((og_f, ol_f), (og_b, ol_b))):
        aq, ak, av, sm, smT, bq, bk, bv, gk = refs[d * n_in:(d + 1) * n_in]
        tri, tri2, lmask2, lmaskv, incl2, strict2 = refs[2 * n_in + d * n_c:2 * n_in + (d + 1) * n_c]
        order = range(nchunks) if d == 0 else range(nchunks - 1, -1, -1)
        for pos, j in enumerate(order):
            rows = pl.ds(j * c, c)
            f32 = lambda r: r[0, rows, :].astype(F32)
            views.append(dict(d=d, pos=pos, rows=rows, aq=f32(aq), ak=f32(ak), av=f32(av),
                              sm=sm[0, rows, :], smT=smT[j], bq=f32(bq), bk=f32(bk),
                              bv=f32(bv), gk=gk[0, rows, :], tri=tri[...], tri2=tri2[...], lmask2=lmask2,
                              lmaskv=lmaskv, incl2=incl2[...], strict2=strict2[...], og=og, ol=ol))
    gdn, gdn_recur = _gdn_stages(views, s_ref, {}, need_out)
    gla, gla_recur = _gla_stages(views, st_ref, {}, need_out)
    for i in range(max(len(gdn), len(gla))):
        if i < len(gdn):
            gdn[i]()
        if i < len(gla):
            gla[i]()
    for pos in range(nchunks):
        gdn_recur(pos, pos == nchunks - 1)()
        gla_recur(pos, pos == nchunks - 1)()

    if not need_out:
        @pl.when(pl.program_id(1) == pl.num_programs(1) - 1)
        def _():
            s_out[0] = s_ref[...]
            st_out[0] = st_ref[...]


def _scans(aq, ak, av, sm, smT, bq, bk, bv, gk, ctx_len):
    bsz, ttot, _ = aq.shape
    seq = ttot - ctx_len
    c = CHUNK
    consts = _scan_consts()
    const = lambda a: pl.BlockSpec(a.shape, lambda b, s: tuple(0 for _ in a.shape))
    cargs = []
    for d in range(NDIR):
        cd = consts[d]
        cargs += [cd["tri"], cd["tri2"], cd["lmask2"], cd["lmaskv"], cd["incl2"], cd["strict2"]]
    ins = [aq, ak, av, sm, smT, bq, bk, bv, gk]
    s_shape = (NDIR, A_HEADS, A_DK, A_DV)
    st_shape = (NDIR, BQ_W, B_DV)
    state_specs = [pl.BlockSpec((1,) + s_shape, lambda b, s: (b, 0, 0, 0, 0)),
                   pl.BlockSpec((1,) + st_shape, lambda b, s: (b, 0, 0, 0))]
    state_shapes = [jax.ShapeDtypeStruct((bsz,) + s_shape, F32), jax.ShapeDtypeStruct((bsz,) + st_shape, F32)]
    scratch = [pltpu.VMEM(s_shape, F32), pltpu.VMEM(st_shape, F32)]
    params = pltpu.CompilerParams(dimension_semantics=("arbitrary", "arbitrary"), vmem_limit_bytes=VMEM_LIMIT)

    def in_specs(nchunks, first, n):
        tb = nchunks * c

        def tok(w, idx, lane_blk=0):
            return pl.BlockSpec((1, tb, w), lambda b, s: (b, idx(s), lane_blk))

        def dir_specs(idx, d):
            return [tok(A_W, idx), tok(A_W, idx), tok(A_W, idx), tok(SMALL_W, idx),
                    pl.BlockSpec((None, nchunks, 16, c), lambda b, s: (b, idx(s), 0, 0)),
                    tok(BQ_W, idx), tok(BQ_W, idx), tok(BV_W, idx), tok(BQ_W, idx, d)]

        return dir_specs(lambda s: first + s, 0) + dir_specs(lambda s: first + n - 1 - s, 1) + [const(a) for a in cargs]

    tbc = CTX_STEP_CHUNKS * c
    ncs = ctx_len // tbc
    s_ctx, st_ctx = pl.pallas_call(
        functools.partial(_scan_kernel, False, CTX_STEP_CHUNKS),
        grid=(bsz, ncs),
        in_specs=in_specs(CTX_STEP_CHUNKS, seq // tbc, ncs),
        out_specs=state_specs,
        out_shape=state_shapes,
        scratch_shapes=scratch,
        compiler_params=params,
        name="scan_ctx",
    )(*ins, *ins, *cargs)

    tb = STEP_CHUNKS * c
    nxs = seq // tb

    def otok(w, idx):
        return pl.BlockSpec((1, tb, w), lambda b, s: (b, idx(s), 0))

    out_w = [A_HEADS * A_DV, A_HEADS * A_DV, BV_W, BV_W]
    fwd, bwd = (lambda s: s), (lambda s: nxs - 1 - s)
    return pl.pallas_call(
        functools.partial(_scan_kernel, True, STEP_CHUNKS),
        grid=(bsz, nxs),
        in_specs=in_specs(STEP_CHUNKS, 0, nxs) + state_specs,
        out_specs=[otok(out_w[0], fwd), otok(out_w[1], bwd), otok(out_w[2], fwd), otok(out_w[3], bwd)],
        out_shape=[jax.ShapeDtypeStruct((bsz, seq, w), F32) for w in out_w],
        scratch_shapes=scratch,
        compiler_params=params,
        name="scan",
    )(*ins, *ins, *cargs, s_ctx, st_ctx)


def _head_norm(o, w, z, n_heads, dv):
    parts = []
    for h in range(n_heads):
        oh = o[:, h * dv:(h + 1) * dv]
        parts.append(oh * lax.rsqrt(jnp.mean(oh * oh, axis=-1, keepdims=True) + EPS))
    return jnp.concatenate(parts, axis=-1) * w * _silu(z)


def _rms(x, w):
    return x * lax.rsqrt(jnp.mean(x * x, axis=-1, keepdims=True) + EPS) * w


def _post_kernel(x_ref, ogf_ref, ogb_ref, olf_ref, olb_ref, az_ref, bg_ref,
                 g1_ref, sh2_ref, sc2_ref, g2_ref, gn_ref, ln_ref, wo_ref, fn_ref,
                 wg_ref, wu_ref, wd_ref, fin_ref, o_ref):
    gdn = _head_norm(ogf_ref[0] + ogb_ref[0], gn_ref[...], az_ref[0], A_HEADS, A_DV)
    gla = _head_norm(olf_ref[0] + olb_ref[0], ln_ref[...], bg_ref[0], B_HEADS, B_DV)
    mix = jnp.concatenate([gdn, gla], axis=-1)
    x1 = x_ref[0] + g1_ref[...] * _dot(mix, wo_ref[...])
    h2b = _bf(_rms(x1, fn_ref[...]) * (1.0 + sc2_ref[...]) + sh2_ref[...])
    dff = wg_ref.shape[1]
    y = None
    for c0 in range(0, dff, FFN_CHUNK):
        c1 = min(c0 + FFN_CHUNK, dff)
        gate = jnp.dot(h2b, wg_ref[:, c0:c1], preferred_element_type=F32)
        up = jnp.dot(h2b, wu_ref[:, c0:c1], preferred_element_type=F32)
        part = _dot(_silu(gate) * up, wd_ref[c0:c1, :])
        y = part if y is None else y + part
    x2 = x1 + g2_ref[...] * y
    o_ref[0] = _rms(x2, fin_ref[...])


def _post(x, ogf, ogb, olf, olb, az, bg, mod4, gn, ln, wo, fn, wg, wu, wd, fin):
    bsz, seq, d = x.shape
    tm = POST_TILE
    dff = wg.shape[1]

    def tok(w):
        return pl.BlockSpec((1, tm, w), lambda b, j: (b, j, 0))

    def modspec(k):
        return pl.BlockSpec((None, None, 1, d), lambda b, j: (b, k, 0, 0))

    def const(shape, single=True):
        idx = lambda b, j: tuple(0 for _ in shape)
        if single:
            return pl.BlockSpec(shape, idx, pipeline_mode=pl.Buffered(1))
        return pl.BlockSpec(shape, idx)

    widths = [d, A_W, A_W, BV_W, BV_W, A_W, BV_W]
    return pl.pallas_call(
        _post_kernel,
        grid=(bsz, seq // tm),
        in_specs=[tok(w) for w in widths] + [
                  modspec(2), modspec(3), modspec(4), modspec(5),
                  const((1, A_W), False), const((1, BV_W), False), const((d, d)), const((1, d), False),
                  const((d, dff)), const((d, dff)), const((dff, d)), const((1, d), False)],
        out_specs=tok(d),
        out_shape=jax.ShapeDtypeStruct((bsz, seq, d), F32),
        compiler_params=pltpu.CompilerParams(dimension_semantics=("arbitrary", "arbitrary"),
                                             vmem_limit_bytes=VMEM_LIMIT),
        name="post",
    )(x, ogf, ogb, olf, olb, az, bg, mod4, mod4, mod4, mod4, gn, ln, wo, fn, wg, wu, wd, fin)


def kernel(x, c, ctx, c_ctx, w_mod, b_mod, attn_norm, w_in, conv_w, a_log, dt_bias, gdn_norm, gla_w2, gla_b,
           gla_norm, w_out, ffn_norm, w_gate, w_up, w_down, final_norm):
    bsz, seq, d = x.shape
    ctx_len = ctx.shape[1]
    assert w_mod.shape[0] == 1, "single-layer block"
    assert ctx_len == TOK_TILE and seq % TOK_TILE == 0 and seq % POST_TILE == 0
    assert TOK_TILE % GRID_W == 0 and ctx_len % (CHUNK * CTX_STEP_CHUNKS) == 0 and seq % (CHUNK * STEP_CHUNKS) == 0

    rows = -(-(bsz + 1) // 8) * 8
    cc = jnp.zeros((rows, d), F32).at[:bsz].set(c).at[rows - 1].set(c_ctx)
    mod = _modulation(cc, w_mod[0], b_mod[0][None, :])
    mod4 = mod.reshape(rows, 6, 1, d)

    w = w_in[0]
    o1 = 4 * A_W
    nb = NDIR * A_HEADS
    o2 = o1 + 2 * nb
    o3 = o2 + 2 * BQ_W + 2 * BV_W
    nlr = NDIR * GLA_RANK
    win = _regroup_weight(w.T, o1, o2, o3, nlr)
    gparams = jnp.zeros((2, SMALL_W), F32)
    gparams = gparams.at[0, nb:2 * nb].set(dt_bias[0].reshape(-1)).at[1, nb:2 * nb].set(a_log[0].reshape(-1))
    w2 = jnp.zeros((SMALL_W, NDIR * BQ_W), F32)
    for n in range(NDIR):
        w2 = w2.at[2 * nb + n * GLA_RANK:2 * nb + (n + 1) * GLA_RANK, n * BQ_W:(n + 1) * BQ_W].set(gla_w2[0, n])
    w2h = w2.astype(BF16)
    w2m = (w2 - w2h.astype(F32)).astype(BF16)

    aq, ak, av, az, bq, bk, bv, bg, gk, sm, smT = _projection(
        x, ctx, mod4, attn_norm, win, conv_w[0], gparams, w2h, w2m, gla_b[0].reshape(1, -1))
    ogf, ogb, olf, olb = _scans(aq, ak, av, sm, smT, bq, bk, bv, gk, ctx_len)

    tile_w = lambda v, n: jnp.tile(v.reshape(1, -1), (1, n))
    return _post(x, ogf, ogb, olf, olb, az, bg, mod4,
                 tile_w(gdn_norm[0], A_HEADS), tile_w(gla_norm[0], B_HEADS),
                 w_out[0].astype(BF16), ffn_norm, w_gate[0].astype(BF16), w_up[0].astype(BF16),
                 w_down[0].astype(BF16), final_norm.reshape(1, -1))
```

```python
import functools

import numpy as np
import jax
import jax.numpy as jnp
from jax import lax
from jax.experimental import pallas as pl
from jax.experimental.pallas import tpu as pltpu

F32 = jnp.float32
BF16 = jnp.bfloat16

GRID_W = 64
A_HEADS, A_DK, A_DV = 4, 128, 128
B_HEADS, B_DK, B_DV = 4, 64, 128
NDIR = 2
GLA_RANK = 16
GLA_NORMALIZER = 16.0
EPS = 1e-6

CHUNK = 64
CTX_STEP_CHUNKS = 4
STEP_CHUNKS = 8
SUBLANES = 8
TOK_TILE = 256
POST_TILE = 512
FFN_CHUNK = 1024
SMALL_W = 128
LEVELS = int(np.log2(CHUNK))

A_W = A_HEADS * A_DK
BQ_W = B_HEADS * B_DK
BV_W = B_HEADS * B_DV
P_COLS = 4 * A_W + 2 * BQ_W + 2 * BV_W + SMALL_W

V7X_VMEM_BYTES = 64 * 1024 * 1024
VMEM_LIMIT = V7X_VMEM_BYTES - 4 * 1024 * 1024


def _bf(x):
    return x.astype(BF16)


def _dot(a, b):
    return jnp.dot(_bf(a), _bf(b), preferred_element_type=F32)


def _dot_nt(a, b):
    return lax.dot_general(_bf(a), _bf(b), (((1,), (1,)), ((), ())), preferred_element_type=F32)


def _dot_tn(a, b):
    return lax.dot_general(_bf(a), _bf(b), (((0,), (0,)), ((), ())), preferred_element_type=F32)


def _split3(x):
    hi = _bf(x)
    r1 = x - hi.astype(F32)
    mid = _bf(r1)
    lo = _bf(r1 - mid.astype(F32))
    return hi, mid, lo


def _dot_exact_lhs(m, x, terms=3):
    d = lambda p: jnp.dot(m, p, preferred_element_type=F32)
    parts = _split3(x)[:terms]
    acc = d(parts[0])
    for p in parts[1:]:
        acc = acc + d(p)
    return acc


def _dot_exact_rhs_nt(x, m):
    hi, mid, lo = _split3(x)
    d = lambda p: lax.dot_general(p, m, (((1,), (1,)), ((), ())), preferred_element_type=F32)
    return d(hi) + d(mid) + d(lo)


def _sigmoid(x):
    return 0.5 * jnp.tanh(0.5 * x) + 0.5


def _silu(x):
    h = 0.5 * x
    return h + h * jnp.tanh(h)


def _softplus(x):
    return jnp.maximum(x, 0.0) + jnp.log(1.0 + jnp.exp(-jnp.abs(x)))


def _later_rows(x, m, fwd):
    n = x.shape[0]
    return jnp.concatenate([x[s + m:s + 2 * m] if fwd else x[s:s + m] for s in range(0, n, 2 * m)], axis=0)


def _merge_later_rows(full, later, m, fwd):
    n = full.shape[0]
    pieces = []
    for i, s in enumerate(range(0, n, 2 * m)):
        lat = later[i * m:(i + 1) * m]
        pieces += [full[s:s + m], lat] if fwd else [lat, full[s + m:s + 2 * m]]
    return jnp.concatenate(pieces, axis=0)


def _mod_kernel(cc_ref, w_ref, b_ref, o_ref):
    o_ref[...] = _dot(_silu(cc_ref[...]), w_ref[...]) + b_ref[...]


def _modulation(cc, w_mod, b_mod):
    rows, d = cc.shape
    n = w_mod.shape[1]
    tn = 1536
    return pl.pallas_call(
        _mod_kernel,
        grid=(n // tn,),
        in_specs=[pl.BlockSpec((rows, d), lambda j: (0, 0)),
                  pl.BlockSpec((d, tn), lambda j: (0, j)),
                  pl.BlockSpec((1, tn), lambda j: (0, j))],
        out_specs=pl.BlockSpec((rows, tn), lambda j: (0, j)),
        out_shape=jax.ShapeDtypeStruct((rows, n), F32),
        compiler_params=pltpu.CompilerParams(dimension_semantics=("arbitrary",),
                                             vmem_limit_bytes=VMEM_LIMIT),
        name="mod",
    )(cc, w_mod, b_mod)


def _regroup_kernel(o1, o2, o3, nlr, wt_ref, o_ref):
    pad = jnp.zeros((SMALL_W - (o2 - o1) - nlr, wt_ref.shape[1]), F32)
    rows = jnp.concatenate([wt_ref[:o1, :], wt_ref[o2:o2 + BQ_W, :] * B_DK ** -0.5, wt_ref[o2 + BQ_W:o3, :],
                            wt_ref[o1:o2, :], wt_ref[o3:o3 + nlr, :], pad], axis=0)
    o_ref[...] = _bf(rows.T)


def _regroup_weight(wt, o1, o2, o3, nlr):
    n, d = wt.shape
    tr = 256
    return pl.pallas_call(
        functools.partial(_regroup_kernel, o1, o2, o3, nlr),
        grid=(d // tr,),
        in_specs=[pl.BlockSpec((n, tr), lambda i: (0, i))],
        out_specs=pl.BlockSpec((tr, P_COLS), lambda i: (i, 0)),
        out_shape=jax.ShapeDtypeStruct((d, P_COLS), BF16),
        compiler_params=pltpu.CompilerParams(dimension_semantics=("arbitrary",), vmem_limit_bytes=VMEM_LIMIT),
        name="regroup",
    )(wt)


def _proj_kernel(n_ctx_tiles, ctx_len,
                 x_ref, ctx_ref, sh_ref, sc_ref, nw_ref, win_ref, cw_ref, gp_ref, w2h_ref, w2m_ref, gb_ref,
                 aq_ref, ak_ref, av_ref, az_ref, bq_ref, bk_ref, bv_ref, bg_ref, gk_ref, sm_ref, smt_ref):
    j = pl.program_id(1)
    is_ctx = j < n_ctx_tiles
    xin = jnp.where(is_ctx, ctx_ref[0], x_ref[0])
    ms = jnp.mean(xin * xin, axis=-1, keepdims=True)
    h = xin * lax.rsqrt(ms + EPS) * (nw_ref[...] * (1.0 + sc_ref[...])) + sh_ref[...]
    hb = _bf(h)
    tm = xin.shape[0]
    sub = SUBLANES
    sublane = lax.broadcasted_iota(jnp.int32, (1, sub, 1), 1)
    cw_half = 0.5 * cw_ref[...]

    def mm(c0, c1):
        return jnp.dot(hb, win_ref[:, c0:c1], preferred_element_type=F32)

    def conv_silu(u, c0):
        wu = u.shape[1]
        u3 = u.reshape(tm // sub, sub, wu)

        def shifted(down):
            rot = pltpu.roll(u3, 1 if down else sub - 1, 1)
            nt = tm // sub
            tiles_per_row = GRID_W // sub
            zero = jnp.zeros((1, sub, wu), F32)
            nbr = []
            for r in range(nt):
                src = r - 1 if down else r + 1
                row_start = (r if down else src) % tiles_per_row == 0
                if src < 0 or src >= nt:
                    nbr.append(zero)
                elif row_start:
                    nbr.append(jnp.where(is_ctx, rot[src:src + 1], 0.0))
                else:
                    nbr.append(rot[src:src + 1])
            edge = sublane == (0 if down else sub - 1)
            return jnp.where(edge, jnp.concatenate(nbr, axis=0), rot).reshape(tm, wu)

        w = cw_half[:, c0:c0 + wu]
        h = w[0:1] * shifted(True) + w[1:2] * u + w[2:3] * shifted(False)
        return h + h * jnp.tanh(h)

    def l2n(t, scale):
        parts = []
        for hd in range(A_HEADS):
            th = t[:, hd * A_DK:(hd + 1) * A_DK]
            parts.append(th * (lax.rsqrt(jnp.sum(th * th, axis=-1, keepdims=True) + EPS) * scale))
        return jnp.concatenate(parts, axis=-1)

    o = 3 * A_W
    uq = mm(0, A_W)
    p1 = mm(o, o + A_W + 2 * BQ_W)
    o += A_W + 2 * BQ_W
    uk = mm(A_W, 2 * A_W)
    p2 = mm(o, o + BV_W)
    o += BV_W
    uv = mm(2 * A_W, 3 * A_W)
    p3 = mm(o, P_COLS)
    aq_ref[0] = l2n(conv_silu(uq, 0), A_DK ** -0.5)
    az_ref[0] = p1[:, :A_W]
    bq_ref[0] = p1[:, A_W:A_W + BQ_W]
    bk_ref[0] = p1[:, A_W + BQ_W:]
    ak_ref[0] = l2n(conv_silu(uk, A_W), 1.0)
    bv_ref[0] = p2
    av_ref[0] = conv_silu(uv, 2 * A_W)
    bg_ref[0] = p3[:, :BV_W]

    ps = p3[:, BV_W:]
    gp = gp_ref[...]
    lane = lax.broadcasted_iota(jnp.int32, (1, SMALL_W), 1)
    beta = _sigmoid(ps)
    g = -jnp.exp(gp[1:2]) * _softplus(ps + gp[0:1])
    nb = NDIR * A_HEADS
    sm = jnp.where(lane < nb, beta, jnp.where(lane < 2 * nb, g, 0.0))
    sm_ref[0] = sm
    for jc in range(tm // CHUNK):
        smt_ref[0, jc] = sm[jc * CHUNK:(jc + 1) * CHUNK, :].T[:2 * nb, :]

    ph = _bf(ps)
    pm = _bf(ps - ph.astype(F32))
    d = lambda a, b: jnp.dot(a, b, preferred_element_type=F32)
    pre = d(ph, w2h_ref[...]) + d(ph, w2m_ref[...]) + d(pm, w2h_ref[...]) + gb_ref[...]
    gk_ref[0] = -_softplus(-pre) * (1.0 / GLA_NORMALIZER)


def _projection(x, ctx, mod4, attn_norm, win, conv_w, gparams, w2h, w2m, gla_b):
    bsz, seq, d = x.shape
    ctx_len = ctx.shape[1]
    tm = TOK_TILE
    nct = ctx_len // tm
    nxt = seq // tm
    ttot = ctx_len + seq
    mod_rows = mod4.shape[0]

    def tile(j):
        return jnp.where(j < nct, nxt + j, j - nct)

    def tok(w):
        return pl.BlockSpec((1, tm, w), lambda b, j: (b, tile(j), 0))

    cpt = tm // CHUNK
    nsm = 2 * NDIR * A_HEADS

    const = lambda shape: pl.BlockSpec(shape, lambda b, j: tuple(0 for _ in shape))
    widths = [A_W, A_W, A_W, A_W, BQ_W, BQ_W, BV_W, BV_W, NDIR * BQ_W, SMALL_W]
    return pl.pallas_call(
        functools.partial(_proj_kernel, nct, ctx_len),
        grid=(bsz, nct + nxt),
        in_specs=[
            pl.BlockSpec((1, tm, d), lambda b, j: (b, jnp.maximum(j - nct, 0), 0)),
            pl.BlockSpec((1, tm, d), lambda b, j: (b, jnp.minimum(j, nct - 1), 0)),
            pl.BlockSpec((None, None, 1, d), lambda b, j: (jnp.where(j < nct, mod_rows - 1, b), 0, 0, 0)),
            pl.BlockSpec((None, None, 1, d), lambda b, j: (jnp.where(j < nct, mod_rows - 1, b), 1, 0, 0)),
            const((1, d)), const((d, P_COLS)), const(conv_w.shape), const(gparams.shape),
            const(w2h.shape), const(w2m.shape), const(gla_b.shape),
        ],
        out_specs=[tok(w) for w in widths] + [
            pl.BlockSpec((1, cpt, nsm, CHUNK), lambda b, j: (b, tile(j), 0, 0))],
        out_shape=[jax.ShapeDtypeStruct((bsz, ttot, w), F32) for w in widths] + [
            jax.ShapeDtypeStruct((bsz, ttot // CHUNK, nsm, CHUNK), F32)],
        compiler_params=pltpu.CompilerParams(dimension_semantics=("arbitrary", "arbitrary"),
                                             vmem_limit_bytes=VMEM_LIMIT),
        name="proj",
    )(x, ctx, mod4, mod4, attn_norm, win, conv_w, gparams, w2h, w2m, gla_b)


def _scan_consts():
    c = CHUNK
    i = np.arange(c)[:, None]
    t = np.arange(c)[None, :]
    out = []
    for d in range(NDIR):
        incl = (t <= i) if d == 0 else (t >= i)
        strict = (t < i) if d == 0 else (t > i)
        tri = incl.astype(np.float32)
        masks = []
        for lv in range(LEVELS):
            m = c >> (lv + 1)
            start = (np.arange(c) // (2 * m)) * (2 * m)
            later = (np.arange(c) % (2 * m) >= m) if d == 0 else (np.arange(c) % (2 * m) < m)
            same = (start[:, None] == start[None, :])
            masks.append((same & later[:, None] & ~later[None, :]).astype(np.float32))
        masks.append(np.eye(c, dtype=np.float32))
        tile2 = lambda mk: np.tile(mk, (1, 2))
        out.append(dict(
            tri=jnp.asarray(tri, BF16),
            tri2=jnp.asarray(np.tile(tri, (2, 1)), BF16),
            lmask2=jnp.asarray(np.stack([tile2(mk) for mk in masks]), F32),
            lmaskv=jnp.asarray(np.stack([np.tile(mk, (B_HEADS, 1)) for mk in masks]), F32),
            incl2=jnp.asarray(tile2(incl.astype(np.float32))),
            strict2=jnp.asarray(tile2(strict.astype(np.float32))),
        ))
    return out


def _gdn_stages(views, s_ref, state, need_out):
    c = CHUNK
    nh = A_HEADS
    npair = nh // 2
    w2 = 2 * c
    first = lax.broadcasted_iota(jnp.int32, (1, w2), 1) < c
    first_k = lax.broadcasted_iota(jnp.int32, (1, 2 * A_DK), 1) < A_DK
    lane_k = lax.broadcasted_iota(jnp.int32, (1, A_W), 1) // A_DK
    km = [lane_k == h for h in range(nh)]

    def expand(cols):
        out = cols[nh - 1]
        for h in range(nh - 2, -1, -1):
            out = jnp.where(km[h], cols[h], out)
        return out

    def blockdiag(x):
        xb = _bf(x)
        return jnp.concatenate([jnp.where(first, xb, 0), jnp.where(first, 0, xb)], axis=0)

    def heads(x, w):
        return [x[:, h * w:(h + 1) * w] for h in range(nh)]

    def prep():
        for vw in views:
            d = vw["d"]
            last = c - 1 if d == 0 else 0
            base = NDIR * nh + nh * d
            sm = vw["sm"]
            gc_all = _dot_exact_lhs(vw["tri"], sm)
            gcr_all = _dot_exact_rhs_nt(vw["smT"], vw["tri2"])
            gcc = [gc_all[:, base + h:base + h + 1] for h in range(nh)]
            gtot = [gc_all[last:last + 1, base + h:base + h + 1] for h in range(nh)]
            beta = [sm[:, nh * d + h:nh * d + h + 1] for h in range(nh)]
            egc = [jnp.exp(g) for g in gcc]
            vw["gl"] = [jnp.exp(g) for g in gtot]
            inc = vw["incl2"] > 0.5
            ak, aq = vw["ak"], vw["aq"]
            beta_k = expand(beta)
            kb = ak * beta_k
            egc_k = expand(egc)
            kbq = _bf(jnp.concatenate([kb, aq], axis=0) if need_out else kb)
            akb = _bf(ak)
            vw["decay"], vw["kbq"], vw["kst"] = [], [], []
            for p in range(npair):
                h0, h1 = 2 * p, 2 * p + 1
                diff = (jnp.where(first, gcc[h0], gcc[h1])
                        - jnp.where(first, gcr_all[base + h0:base + h0 + 1, :], gcr_all[base + h1:base + h1 + 1, :]))
                vw["decay"].append(jnp.where(inc, jnp.exp(jnp.where(inc, diff, 0.0)), 0.0))
                kp = akb[:, 2 * A_DK * p:2 * A_DK * (p + 1)]
                vw["kst"].append(jnp.concatenate([jnp.where(first_k, kp, 0), jnp.where(first_k, 0, kp)], axis=0))
                vw["kbq"].append(kbq[:, 2 * A_DK * p:2 * A_DK * (p + 1)])
            vb = heads(vw["av"] * beta_k, A_DV)
            kbe = heads(kb * egc_k, A_DK)
            rhs = [jnp.concatenate([vb[h], kbe[h]], axis=1) for h in range(nh)]
            vw["rhs"] = [_bf(jnp.concatenate([rhs[2 * p], rhs[2 * p + 1]], axis=0)) for p in range(npair)]
            vw["qe"] = heads(aq * egc_k, A_DK) if need_out else None
            vw["kdec"] = heads(_bf(ak * expand([jnp.exp(gtot[h] - gcc[h]) for h in range(nh)])), A_DK)
        for vw in views:
            vw["m"] = [_dot_nt(vw["kbq"][p], vw["kst"][p]) for p in range(npair)]
        for vw in views:
            lm = vw["lmask2"]
            vw["a"] = [vw["m"][p][:c] * vw["decay"][p] * vw["strict2"] for p in range(npair)]
            if need_out:
                vw["qk"] = [blockdiag(vw["m"][p][c:] * vw["decay"][p]) for p in range(npair)]
            vw["t"] = [lm[LEVELS] - vw["a"][p] * lm[LEVELS - 1] for p in range(npair)]

    def level(lv):
        m = c >> (lv + 1)
        cut = m % SUBLANES == 0

        def run():
            for vw in views:
                fwd = vw["d"] == 0
                rows = (lambda x: _later_rows(x, m, fwd)) if cut else (lambda x: x)
                vw["tl"] = [_dot(rows(vw["t"][p]), blockdiag(vw["a"][p] * vw["lmask2"][lv])) for p in range(npair)]
            for vw in views:
                fwd = vw["d"] == 0
                for p in range(npair):
                    t = vw["t"][p]
                    upd = _dot(vw["tl"][p], blockdiag(t))
                    vw["t"][p] = _merge_later_rows(t, _later_rows(t, m, fwd) - upd, m, fwd) if cut else t - upd
        return run

    def solve():
        for vw in views:
            rp = [_dot(blockdiag(vw["t"][p]), vw["rhs"][p]) for p in range(npair)]
            vw["r"] = [rp[h // 2][(h % 2) * c:(h % 2 + 1) * c] for h in range(nh)]
        for vw in views:
            vw["lhs"] = [_bf(jnp.concatenate([vw["r"][h][:, A_DV:], vw["qe"][h]], axis=0) if need_out
                             else vw["r"][h][:, A_DV:]) for h in range(nh)]

    def recur(pos, is_last):
        def run():
            cur = [vw for vw in views if vw["pos"] == pos]
            for vw in cur:
                d = vw["d"]
                for h in range(nh):
                    if (d, h) not in state:
                        state[(d, h)] = s_ref[d, h]
                vw["ws"] = [_dot(vw["lhs"][h], state[(d, h)]) for h in range(nh)]
            for vw in cur:
                vw["vn"] = [vw["r"][h][:, :A_DV] - vw["ws"][h][:c] for h in range(nh)]
            for vw in cur if need_out else []:
                for p in range(npair):
                    vn2 = _bf(jnp.concatenate([vw["vn"][2 * p], vw["vn"][2 * p + 1]], axis=0))
                    oq = _dot(vw["qk"][p], vn2)
                    for e in range(2):
                        h = 2 * p + e
                        vw["og"][0, vw["rows"], h * A_DV:(h + 1) * A_DV] = vw["ws"][h][c:] + oq[e * c:(e + 1) * c]
            for vw in cur:
                d = vw["d"]
                for h in range(nh):
                    state[(d, h)] = state[(d, h)] * vw["gl"][h] + _dot_tn(vw["kdec"][h], vw["vn"][h])
                    if is_last:
                        s_ref[d, h] = state[(d, h)]
        return run

    return [prep] + [level(lv) for lv in range(LEVELS - 2, -1, -1)] + [solve], recur


def _gla_stages(views, st_ref, state, need_out):
    c = CHUNK
    nh = B_HEADS
    lane_q = lax.broadcasted_iota(jnp.int32, (1, BQ_W), 1) // B_DK
    row = lax.broadcasted_iota(jnp.int32, (c, 1), 0)

    def stack_heads(x):
        return jnp.concatenate([jnp.where(lane_q == h, x, 0) for h in range(nh)], axis=0)

    def level_abs(vw, lv):
        m = c >> (lv + 1)
        fwd = vw["d"] == 0
        bc, na = vw["bc"], vw["na"]
        if m >= 4:
            refs = [s + (m - 1 if fwd else m) for s in range(0, c, 2 * m)]
            ref_rows = jnp.concatenate([jnp.broadcast_to(bc[r:r + 1, :], (2 * m, BQ_W)) for r in refs], axis=0)
            return jnp.abs(bc - ref_rows)
        r = row % (2 * m)
        up = pltpu.roll(na, c - 1, 0)
        dn = pltpu.roll(na, 1, 0)
        if m == 2:
            if fwd:
                return jnp.where(r == 0, up, jnp.where(r == 1, 0.0, jnp.where(r == 2, na, na + dn)))
            return jnp.where(r == 0, na + up, jnp.where(r == 1, na, jnp.where(r == 2, 0.0, dn)))
        return jnp.where(r == (1 if fwd else 0), na, 0.0)

    def prep():
        for vw in views:
            vw["bc"] = _dot_exact_lhs(vw["tri"], vw["gk"])
            vw["na"] = -vw["gk"]
            vw["sacc"] = [jnp.zeros((c, c), F32) for _ in range(nh)]

    def level(lv):
        def run():
            for vw in views:
                m = c >> (lv + 1)
                fwd = vw["d"] == 0
                if lv >= LEVELS - 1:
                    if lv == LEVELS:
                        prod = vw["bq"] * vw["bk"]
                    else:
                        kn = pltpu.roll(vw["bk"], 1 if fwd else c - 1, 0)
                        prod = vw["bq"] * jnp.exp(-level_abs(vw, lv)) * kn
                    mask = vw["lmaskv"][lv, :c, :] > 0.5
                    for h in range(nh):
                        col = jnp.sum(jnp.where(lane_q == h, prod, 0.0), axis=-1, keepdims=True)
                        vw["sacc"][h] = vw["sacc"][h] + jnp.where(mask, col, 0.0)
                    continue
                wgt = jnp.exp(-level_abs(vw, lv))
                ql, kl = _bf(vw["bq"] * wgt), _bf(vw["bk"] * wgt)
                if m % SUBLANES == 0:
                    sl = _dot_nt(stack_heads(_later_rows(ql, m, fwd)), kl)
                    mask = _later_rows(vw["lmaskv"][lv, :c, :], m, fwd)
                    hc = c // 2
                    for h in range(nh):
                        old = vw["sacc"][h]
                        new = _later_rows(old, m, fwd) + sl[h * hc:(h + 1) * hc] * mask
                        vw["sacc"][h] = _merge_later_rows(old, new, m, fwd)
                else:
                    sl = _dot_nt(stack_heads(ql), kl) * vw["lmaskv"][lv]
                    vw["sacc"] = [vw["sacc"][h] + sl[h * c:(h + 1) * c] for h in range(nh)]
        return run

    def intra():
        for vw in views:
            last = c - 1 if vw["d"] == 0 else 0
            bc = vw["bc"]
            bct = bc[last:last + 1]
            bvh = [_bf(vw["bv"][:, h * B_DV:(h + 1) * B_DV]) for h in range(nh)]
            kd = vw["bk"] * jnp.exp(bct - bc)
            kdg = jnp.concatenate([kd, jnp.broadcast_to(jnp.exp(bct), (SUBLANES, BQ_W))], axis=0).T
            kdT = _bf(kdg[:, :c])
            vw["lgl"] = kdg[:, c:c + 1]
            if need_out:
                both = [_dot(jnp.concatenate([_bf(vw["sacc"][h]), kdT[h * B_DK:(h + 1) * B_DK]], axis=0), bvh[h])
                        for h in range(nh)]
                vw["oi"] = [b[:c] for b in both]
                vw["upd"] = jnp.concatenate([b[c:] for b in both], axis=0)
                vw["qd"] = stack_heads(_bf(vw["bq"] * jnp.exp(bc)))
            else:
                vw["upd"] = jnp.concatenate([_dot(kdT[h * B_DK:(h + 1) * B_DK], bvh[h]) for h in range(nh)], axis=0)

    def recur(pos, is_last):
        def run():
            for vw in views:
                if vw["pos"] != pos:
                    continue
                d = vw["d"]
                if d not in state:
                    state[d] = st_ref[d]
                if need_out:
                    oint = _dot(vw["qd"], state[d])
                    for h in range(nh):
                        vw["ol"][0, vw["rows"], h * B_DV:(h + 1) * B_DV] = vw["oi"][h] + oint[h * c:(h + 1) * c]
                state[d] = state[d] * vw["lgl"] + vw["upd"]
                if is_last:
                    st_ref[d] = state[d]
        return run

    return [prep] + ([level(lv) for lv in range(LEVELS + 1)] if need_out else []) + [intra], recur


class _View(dict):
    def __getitem__(self, key):
        val = dict.__getitem__(self, key)
        return val() if callable(val) else val


def _scan_kernel(need_out, nchunks, *refs):
    n_in = 9
    n_c = 6
    k = 2 * n_in + 2 * n_c
    if need_out:
        s_in, st_in, og_f, og_b, ol_f, ol_b, s_ref, st_ref = refs[k:]
    else:
        s_out, st_out, s_ref, st_ref = refs[k:]
        og_f = og_b = ol_f = ol_b = None
    c = CHUNK

    @pl.when(pl.program_id(1) == 0)
    def _():
        if need_out:
            s_ref[...] = s_in[0]
            st_ref[...] = st_in[0]
        else:
            s_ref[...] = jnp.zeros_like(s_ref)
            st_ref[...] = jnp.zeros_like(st_ref)

    views = []
    for d, (og, ol) in enumerate(((og_f, ol_f), (og_b, ol_b))):
        aq, ak, av, sm, smT, bq, bk, bv, gk = refs[d * n_in:(d + 1) * n_in]
        tri, tri2, lmask2, lmaskv, incl2, strict2 = refs[2 * n_in + d * n_c:2 * n_in + (d + 1) * n_c]
        order = range(nchunks) if d == 0 else range(nchunks - 1, -1, -1)
        for pos, j in enumerate(order):
            rows = pl.ds(j * c, c)
            tok = lambda r, rows=rows: (lambda: r[0, rows, :])
            views.append(_View(d=d, pos=pos, rows=rows, aq=tok(aq), ak=tok(ak), av=tok(av),
                               sm=tok(sm), smT=smT[j], bq=tok(bq), bk=tok(bk),
                               bv=tok(bv), gk=tok(gk), tri=tri[...], tri2=tri2[...], lmask2=lmask2,
                               lmaskv=lmaskv, incl2=incl2[...], strict2=strict2[...], og=og, ol=ol))
    gdn, gdn_recur = _gdn_stages(views, s_ref, {}, need_out)
    gla, gla_recur = _gla_stages(views, st_ref, {}, need_out)
    for i in range(max(len(gdn), len(gla))):
        if i < len(gdn):
            gdn[i]()
        if i < len(gla):
            gla[i]()
    for pos in range(nchunks):
        gdn_recur(pos, pos == nchunks - 1)()
        gla_recur(pos, pos == nchunks - 1)()

    if not need_out:
        @pl.when(pl.program_id(1) == pl.num_programs(1) - 1)
        def _():
            s_out[0] = s_ref[...]
            st_out[0] = st_ref[...]


def _scans(aq, ak, av, sm, smT, bq, bk, bv, gk, ctx_len):
    bsz, ttot, _ = aq.shape
    seq = ttot - ctx_len
    c = CHUNK
    consts = _scan_consts()
    const = lambda a: pl.BlockSpec(a.shape, lambda b, s: tuple(0 for _ in a.shape))
    cargs = []
    for d in range(NDIR):
        cd = consts[d]
        cargs += [cd["tri"], cd["tri2"], cd["lmask2"], cd["lmaskv"], cd["incl2"], cd["strict2"]]
    ins = [aq, ak, av, sm, smT, bq, bk, bv, gk]
    s_shape = (NDIR, A_HEADS, A_DK, A_DV)
    st_shape = (NDIR, BQ_W, B_DV)
    state_specs = [pl.BlockSpec((1,) + s_shape, lambda b, s: (b, 0, 0, 0, 0)),
                   pl.BlockSpec((1,) + st_shape, lambda b, s: (b, 0, 0, 0))]
    state_shapes = [jax.ShapeDtypeStruct((bsz,) + s_shape, F32), jax.ShapeDtypeStruct((bsz,) + st_shape, F32)]
    scratch = [pltpu.VMEM(s_shape, F32), pltpu.VMEM(st_shape, F32)]
    params = pltpu.CompilerParams(dimension_semantics=("arbitrary", "arbitrary"), vmem_limit_bytes=VMEM_LIMIT)

    def in_specs(nchunks, first, n):
        tb = nchunks * c

        def tok(w, idx, lane_blk=0):
            return pl.BlockSpec((1, tb, w), lambda b, s: (b, idx(s), lane_blk))

        def dir_specs(idx, d):
            return [tok(A_W, idx), tok(A_W, idx), tok(A_W, idx), tok(SMALL_W, idx),
                    pl.BlockSpec((None, nchunks, 16, c), lambda b, s: (b, idx(s), 0, 0)),
                    tok(BQ_W, idx), tok(BQ_W, idx), tok(BV_W, idx), tok(BQ_W, idx, d)]

        return dir_specs(lambda s: first + s, 0) + dir_specs(lambda s: first + n - 1 - s, 1) + [const(a) for a in cargs]

    tbc = CTX_STEP_CHUNKS * c
    ncs = ctx_len // tbc
    s_ctx, st_ctx = pl.pallas_call(
        functools.partial(_scan_kernel, False, CTX_STEP_CHUNKS),
        grid=(bsz, ncs),
        in_specs=in_specs(CTX_STEP_CHUNKS, seq // tbc, ncs),
        out_specs=state_specs,
        out_shape=state_shapes,
        scratch_shapes=scratch,
        compiler_params=params,
        name="scan_ctx",
    )(*ins, *ins, *cargs)

    tb = STEP_CHUNKS * c
    nxs = seq // tb

    def otok(w, idx):
        return pl.BlockSpec((1, tb, w), lambda b, s: (b, idx(s), 0))

    out_w = [A_HEADS * A_DV, A_HEADS * A_DV, BV_W, BV_W]
    fwd, bwd = (lambda s: s), (lambda s: nxs - 1 - s)
    return pl.pallas_call(
        functools.partial(_scan_kernel, True, STEP_CHUNKS),
        grid=(bsz, nxs),
        in_specs=in_specs(STEP_CHUNKS, 0, nxs) + state_specs,
        out_specs=[otok(out_w[0], fwd), otok(out_w[1], bwd), otok(out_w[2], fwd), otok(out_w[3], bwd)],
        out_shape=[jax.ShapeDtypeStruct((bsz, seq, w), F32) for w in out_w],
        scratch_shapes=scratch,
        compiler_params=params,
        name="scan",
    )(*ins, *ins, *cargs, s_ctx, st_ctx)


def _head_norm(o, w, z, n_heads, dv):
    parts = []
    for h in range(n_heads):
        oh = o[:, h * dv:(h + 1) * dv]
        parts.append(oh * lax.rsqrt(jnp.mean(oh * oh, axis=-1, keepdims=True) + EPS))
    return jnp.concatenate(parts, axis=-1) * w * _silu(z)


def _rms(x, w):
    return x * lax.rsqrt(jnp.mean(x * x, axis=-1, keepdims=True) + EPS) * w


def _post_kernel(x_ref, ogf_ref, ogb_ref, olf_ref, olb_ref, az_ref, bg_ref,
                 g1_ref, sh2_ref, sc2_ref, g2_ref, gn_ref, ln_ref, wo_ref, fn_ref,
                 wg_ref, wu_ref, wd_ref, fin_ref, o_ref):
    gdn = _head_norm(ogf_ref[0] + ogb_ref[0], gn_ref[...], az_ref[0], A_HEADS, A_DV)
    gla = _head_norm(olf_ref[0] + olb_ref[0], ln_ref[...], bg_ref[0], B_HEADS, B_DV)
    mix = jnp.concatenate([gdn, gla], axis=-1)
    x1 = x_ref[0] + g1_ref[...] * _dot(mix, wo_ref[...])
    h2b = _bf(_rms(x1, fn_ref[...]) * (1.0 + sc2_ref[...]) + sh2_ref[...])
    dff = wg_ref.shape[1]
    y = None
    for c0 in range(0, dff, FFN_CHUNK):
        c1 = min(c0 + FFN_CHUNK, dff)
        gate = jnp.dot(h2b, wg_ref[:, c0:c1], preferred_element_type=F32)
        up = jnp.dot(h2b, wu_ref[:, c0:c1], preferred_element_type=F32)
        part = _dot(_silu(gate) * up, wd_ref[c0:c1, :])
        y = part if y is None else y + part
    x2 = x1 + g2_ref[...] * y
    o_ref[0] = _rms(x2, fin_ref[...])


def _post(x, ogf, ogb, olf, olb, az, bg, mod4, gn, ln, wo, fn, wg, wu, wd, fin):
    bsz, seq, d = x.shape
    tm = POST_TILE
    dff = wg.shape[1]

    def tok(w):
        return pl.BlockSpec((1, tm, w), lambda b, j: (b, j, 0))

    def modspec(k):
        return pl.BlockSpec((None, None, 1, d), lambda b, j: (b, k, 0, 0))

    def const(shape, single=True):
        idx = lambda b, j: tuple(0 for _ in shape)
        if single:
            return pl.BlockSpec(shape, idx, pipeline_mode=pl.Buffered(1))
        return pl.BlockSpec(shape, idx)

    widths = [d, A_W, A_W, BV_W, BV_W, A_W, BV_W]
    return pl.pallas_call(
        _post_kernel,
        grid=(bsz, seq // tm),
        in_specs=[tok(w) for w in widths] + [
                  modspec(2), modspec(3), modspec(4), modspec(5),
                  const((1, A_W), False), const((1, BV_W), False), const((d, d)), const((1, d), False),
                  const((d, dff)), const((d, dff)), const((dff, d)), const((1, d), False)],
        out_specs=tok(d),
        out_shape=jax.ShapeDtypeStruct((bsz, seq, d), F32),
        compiler_params=pltpu.CompilerParams(dimension_semantics=("arbitrary", "arbitrary"),
                                             vmem_limit_bytes=VMEM_LIMIT),
        name="post",
    )(x, ogf, ogb, olf, olb, az, bg, mod4, mod4, mod4, mod4, gn, ln, wo, fn, wg, wu, wd, fin)


def kernel(x, c, ctx, c_ctx, w_mod, b_mod, attn_norm, w_in, conv_w, a_log, dt_bias, gdn_norm, gla_w2, gla_b,
           gla_norm, w_out, ffn_norm, w_gate, w_up, w_down, final_norm):
    bsz, seq, d = x.shape
    ctx_len = ctx.shape[1]
    assert w_mod.shape[0] == 1, "single-layer block"
    assert ctx_len == TOK_TILE and seq % TOK_TILE == 0 and seq % POST_TILE == 0
    assert TOK_TILE % GRID_W == 0 and ctx_len % (CHUNK * CTX_STEP_CHUNKS) == 0 and seq % (CHUNK * STEP_CHUNKS) == 0

    rows = -(-(bsz + 1) // 8) * 8
    cc = jnp.zeros((rows, d), F32).at[:bsz].set(c).at[rows - 1].set(c_ctx)
    mod = _modulation(cc, w_mod[0], b_mod[0][None, :])
    mod4 = mod.reshape(rows, 6, 1, d)

    w = w_in[0]
    o1 = 4 * A_W
    nb = NDIR * A_HEADS
    o2 = o1 + 2 * nb
    o3 = o2 + 2 * BQ_W + 2 * BV_W
    nlr = NDIR * GLA_RANK
    win = _regroup_weight(w.T, o1, o2, o3, nlr)
    gparams = jnp.zeros((2, SMALL_W), F32)
    gparams = gparams.at[0, nb:2 * nb].set(dt_bias[0].reshape(-1)).at[1, nb:2 * nb].set(a_log[0].reshape(-1))
    w2 = jnp.zeros((SMALL_W, NDIR * BQ_W), F32)
    for n in range(NDIR):
        w2 = w2.at[2 * nb + n * GLA_RANK:2 * nb + (n + 1) * GLA_RANK, n * BQ_W:(n + 1) * BQ_W].set(gla_w2[0, n])
    w2h = w2.astype(BF16)
    w2m = (w2 - w2h.astype(F32)).astype(BF16)

    aq, ak, av, az, bq, bk, bv, bg, gk, sm, smT = _projection(
        x, ctx, mod4, attn_norm, win, conv_w[0], gparams, w2h, w2m, gla_b[0].reshape(1, -1))
    ogf, ogb, olf, olb = _scans(aq, ak, av, sm, smT, bq, bk, bv, gk, ctx_len)

    tile_w = lambda v, n: jnp.tile(v.reshape(1, -1), (1, n))
    return _post(x, ogf, ogb, olf, olb, az, bg, mod4,
                 tile_w(gdn_norm[0], A_HEADS), tile_w(gla_norm[0], B_HEADS),
                 w_out[0].astype(BF16), ffn_norm, w_gate[0].astype(BF16), w_up[0].astype(BF16),
                 w_down[0].astype(BF16), final_norm.reshape(1, -1))
```
